```python
import functools
import jax, jax.numpy as jnp
from jax import lax
import numpy as np

D_MODEL = 1024
BATCH = 8
SEQ = 2048
DEPTH = 2
DEC_BATCH = 8
DEC_SEQ = 32
PAST_LEN = 4096

CHUNK = 64
HEAD_DIM = 64
H_RET = 4
H_GLA = 4
H_ATT = 8
W_RET = H_RET * HEAD_DIM
W_GLA = H_GLA * HEAD_DIM
W_ATT = H_ATT * HEAD_DIM
MIX_WIDTH = W_RET + W_GLA + W_ATT
GLA_RANK = 16
GLA_TAU = 16.0
BAND_CHUNKS = 8
ATT_REACH = BAND_CHUNKS * CHUNK
REL_MAX = 256
N_REL = CHUNK + REL_MAX
D_FF = 4 * D_MODEL
ROPE_BASE = 10000.0
EPS = 1e-6
IN_COLS = 4 * W_RET + 4 * W_GLA + GLA_RANK + 3 * W_ATT

kernel_name = 'hybrid_stream_encoder_step'


def rmsnorm(x, w):
    xf = x.astype(jnp.float32)
    y = xf * lax.rsqrt(jnp.mean(xf * xf, axis=-1, keepdims=True) + EPS)
    return (y * w.astype(jnp.float32)).astype(x.dtype)


def rotary(x, pos):
    half = HEAD_DIM // 2
    inv_freq = ROPE_BASE ** (-jnp.arange(half, dtype=jnp.float32) / half)
    ang = pos.astype(jnp.float32)[:, None] * inv_freq[None, :]
    cos = jnp.cos(ang)[None, :, None, :]
    sin = jnp.sin(ang)[None, :, None, :]
    x1, x2 = x[..., :half], x[..., half:]
    return jnp.concatenate([x1 * cos - x2 * sin, x1 * sin + x2 * cos], axis=-1)


def retention_log_decay():
    return jnp.log1p(-jnp.exp2(-5.0 - jnp.arange(H_RET, dtype=jnp.float32)))


def project(hn, w_in_l, w_a2_l, b_a_l, pos):
    B, L, _ = hn.shape
    proj = (hn @ w_in_l).astype(jnp.float32)
    sizes = (W_RET,) * 4 + (W_GLA,) * 4 + (GLA_RANK,) + (W_ATT,) * 3
    offsets, acc = [], 0
    for s in sizes[:-1]:
        acc += s
        offsets.append(acc)
    qa, ka, va, ga, qb, kb, vb, gb, low, qc, kc, vc = jnp.split(proj, offsets, axis=-1)
    heads = lambda t: t.reshape(B, L, -1, HEAD_DIM)
    scale = HEAD_DIM ** -0.5
    qa = rotary(heads(qa), pos)
    ka = rotary(heads(ka), pos) * scale
    qb = heads(qb) * scale
    log_f = jax.nn.log_sigmoid(low @ w_a2_l.astype(jnp.float32) + b_a_l.astype(jnp.float32)) / GLA_TAU
    return (qa, ka, heads(va), ga, qb, heads(kb), heads(vb), gb, heads(log_f),
            heads(qc), heads(kc), heads(vc))


def retention_chunk(S, q, k, v, log_gamma):
    L = q.shape[1]
    idx = jnp.arange(L)
    diff = idx[:, None] - idx[None, :]
    causal = diff >= 0
    dpos = jnp.where(causal, diff, 0).astype(jnp.float32)
    decay = jnp.where(causal[None], jnp.exp(dpos[None] * log_gamma[:, None, None]), 0.0)
    scores = jnp.einsum('bthd,bshd->bhts', q, k) * decay[None]
    intra = jnp.einsum('bhts,bshv->bthv', scores, v)
    q_decay = jnp.exp((idx + 1).astype(jnp.float32)[:, None] * log_gamma[None, :])
    inter = jnp.einsum('bthd,bhdv->bthv', q, S) * q_decay[None, :, :, None]
    k_decay = jnp.exp((L - 1 - idx).astype(jnp.float32)[:, None] * log_gamma[None, :])
    S_new = (jnp.exp(L * log_gamma)[None, :, None, None] * S
             + jnp.einsum('bshd,bshv->bhdv', k * k_decay[None, :, :, None], v))
    return S_new, intra + inter


def gla_chunk(S, q, k, v, log_f):
    L = q.shape[1]
    idx = jnp.arange(L)
    b = jnp.cumsum(log_f, axis=1)
    mask = (idx[:, None] >= idx[None, :])[None, :, :, None, None]
    diff = b[:, :, None] - b[:, None, :]
    w = jnp.where(mask, jnp.exp(jnp.where(mask, diff, 0.0)), 0.0)
    scores = jnp.sum(q[:, :, None] * k[:, None, :] * w, axis=-1)
    intra = jnp.einsum('btsh,bshv->bthv', scores, v)
    inter = jnp.einsum('bthd,bhdv->bthv', q * jnp.exp(b), S)
    b_last = b[:, -1]
    S_new = (jnp.exp(b_last)[..., None] * S
             + jnp.einsum('bshd,bshv->bhdv', k * jnp.exp(b_last[:, None] - b), v))
    return S_new, intra + inter


def scan_chunks(step, S0, *seqs):
    B, L = seqs[0].shape[:2]
    n = L // CHUNK
    xs = tuple(jnp.moveaxis(t.reshape(B, n, CHUNK, *t.shape[2:]), 1, 0) for t in seqs)
    S, outs = lax.scan(lambda c, xt: step(c, *xt), S0, xs)
    return S, jnp.moveaxis(outs, 0, 1).reshape(B, L, *outs.shape[3:])


def band_attention(q, k, v, q_pos, k_pos, k_valid, rel_bias_l):
    s = jnp.einsum('bnqhd,bnkhd->bhnqk', q, k) * (HEAD_DIM ** -0.5)
    dist = q_pos[:, :, None] - k_pos[:, None, :]
    ridx = jnp.clip(dist, -(CHUNK - 1), REL_MAX) + (CHUNK - 1)
    s = s + rel_bias_l.astype(jnp.float32)[:, ridx][None]
    s = jnp.where(k_valid[None, None, :, None, :], s, -1e30)
    p = jax.nn.softmax(s, axis=-1)
    return jnp.einsum('bhnqk,bnkhd->bnqhd', p, v)


def prompt_band_attention(q, k, v, rel_bias_l):
    B, L, H, d = q.shape
    n = L // CHUNK

    def band(t):
        tp = jnp.pad(t, ((0, 0), (ATT_REACH, 0), (0, 0), (0, 0))).reshape(B, n + BAND_CHUNKS, CHUNK, H, d)
        return jnp.concatenate([tp[:, j:j + n] for j in range(BAND_CHUNKS + 1)], axis=2)

    q_pos = jnp.arange(L).reshape(n, CHUNK)
    k_pos = (jnp.arange(n)[:, None] - BAND_CHUNKS) * CHUNK + jnp.arange((BAND_CHUNKS + 1) * CHUNK)[None, :]
    out = band_attention(q.reshape(B, n, CHUNK, H, d), band(k), band(v), q_pos, k_pos, k_pos >= 0, rel_bias_l)
    return out.reshape(B, L, H, d)


def sample_band_attention(q, k_new, v_new, k_cache, v_cache, rel_bias_l):
    B, L, H, d = q.shape
    R = k_cache.shape[1]
    k_all = jnp.concatenate([k_cache.astype(jnp.float32), k_new], axis=1)[:, None]
    v_all = jnp.concatenate([v_cache.astype(jnp.float32), v_new], axis=1)[:, None]
    q_pos = (PAST_LEN + jnp.arange(L))[None]
    k_pos = jnp.concatenate([PAST_LEN - R + jnp.arange(R), PAST_LEN + jnp.arange(L)])[None]
    out = band_attention(q[:, None], k_all, v_all, q_pos, k_pos, k_pos >= 0, rel_bias_l)
    return out[:, 0]


def merge_heads(oa, ga, ob, gb, oc, gn_w, gla_w, w_out_l, dtype):
    B, L = oa.shape[:2]
    mu = jnp.mean(oa, axis=-1, keepdims=True)
    var = jnp.mean(jnp.square(oa - mu), axis=-1, keepdims=True)
    a = ((oa - mu) * lax.rsqrt(var + EPS)).reshape(B, L, W_RET) * gn_w.astype(jnp.float32) * jax.nn.silu(ga)
    b = (ob * lax.rsqrt(jnp.mean(ob * ob, axis=-1, keepdims=True) + EPS)).reshape(B, L, W_GLA)
    b = b * gla_w.astype(jnp.float32) * jax.nn.silu(gb)
    cat = jnp.concatenate([a, b, oc.reshape(B, L, W_ATT)], axis=-1)
    return (cat.astype(dtype) @ w_out_l).astype(dtype)


def channel_mix(h, n2, w_up_l, w_down_l):
    u = rmsnorm(h, n2) @ w_up_l
    return (jnp.square(jax.nn.relu(u)) @ w_down_l).astype(h.dtype)


def setup_inputs(seed: int = 0) -> dict:
    key = jax.random.key(seed)
    ks = jax.random.split(key, 20)
    f32 = jnp.float32
    nrm = lambda k, shape, s: jax.random.normal(k, shape, f32) * s
    cache_rows = min(ATT_REACH, PAST_LEN)
    return {
        'x_prompt': nrm(ks[0], (BATCH, SEQ, D_MODEL), 1.0),
        'x_sample': nrm(ks[1], (DEC_BATCH, DEC_SEQ, D_MODEL), 1.0),
        'state_ret': nrm(ks[2], (DEPTH, DEC_BATCH, H_RET, HEAD_DIM, HEAD_DIM), 0.5),
        'state_gla': nrm(ks[3], (DEPTH, DEC_BATCH, H_GLA, HEAD_DIM, HEAD_DIM), 0.5),
        'cache_attn_k': nrm(ks[4], (DEPTH, DEC_BATCH, cache_rows, H_ATT, HEAD_DIM), 1.0),
        'cache_attn_v': nrm(ks[5], (DEPTH, DEC_BATCH, cache_rows, H_ATT, HEAD_DIM), 1.0),
        'norm1_w': 1.0 + nrm(ks[6], (DEPTH, D_MODEL), 0.02),
        'norm2_w': 1.0 + nrm(ks[7], (DEPTH, D_MODEL), 0.02),
        'final_norm_w': 1.0 + nrm(ks[8], (D_MODEL,), 0.02),
        'w_in': nrm(ks[9], (DEPTH, D_MODEL, IN_COLS), D_MODEL ** -0.5),
        'w_gla_a2': nrm(ks[10], (DEPTH, GLA_RANK, W_GLA), GLA_RANK ** -0.5),
        'b_gla_a': nrm(ks[11], (DEPTH, W_GLA), 0.1),
        'ret_gn_w': 1.0 + nrm(ks[12], (DEPTH, W_RET), 0.02),
        'gla_norm_w': 1.0 + nrm(ks[13], (DEPTH, W_GLA), 0.02),
        'rel_bias': nrm(ks[14], (DEPTH, H_ATT, N_REL), 0.1),
        'w_out': nrm(ks[15], (DEPTH, MIX_WIDTH, D_MODEL), MIX_WIDTH ** -0.5),
        'w_up': nrm(ks[16], (DEPTH, D_MODEL, D_FF), D_MODEL ** -0.5),
        'w_down': nrm(ks[17], (DEPTH, D_FF, D_MODEL), D_FF ** -0.5),
    }


def reference(x_prompt, x_sample, state_ret, state_gla, cache_attn_k, cache_attn_v,
              norm1_w, norm2_w, final_norm_w, w_in, w_gla_a2, b_gla_a, ret_gn_w,
              gla_norm_w, rel_bias, w_out, w_up, w_down):
    f32 = jnp.float32
    log_gamma = retention_log_decay()
    ret_step = functools.partial(retention_chunk, log_gamma=log_gamma)
    bp, lp = x_prompt.shape[0], x_prompt.shape[1]
    pos_p = jnp.arange(lp)
    pos_s = PAST_LEN + jnp.arange(x_sample.shape[1])
    keep = min(ATT_REACH, lp)
    hp, hs = x_prompt, x_sample
    ret_p, gla_p, kp, vp, ret_s, gla_s, k_s, v_s = [], [], [], [], [], [], [], []
    for l in range(DEPTH):
        qa, ka, va, ga, qb, kb, vb, gb, lf, qc, kc, vc = project(
            rmsnorm(hp, norm1_w[l]), w_in[l], w_gla_a2[l], b_gla_a[l], pos_p)
        Sa, oa = scan_chunks(ret_step, jnp.zeros((bp, H_RET, HEAD_DIM, HEAD_DIM), f32), qa, ka, va)
        Sb, ob = scan_chunks(gla_chunk, jnp.zeros((bp, H_GLA, HEAD_DIM, HEAD_DIM), f32), qb, kb, vb, lf)
        oc = prompt_band_attention(qc, kc, vc, rel_bias[l])
        hp = hp + merge_heads(oa, ga, ob, gb, oc, ret_gn_w[l], gla_norm_w[l], w_out[l], hp.dtype)
        hp = hp + channel_mix(hp, norm2_w[l], w_up[l], w_down[l])
        ret_p.append(Sa)
        gla_p.append(Sb)
        kp.append(kc[:, lp - keep:])
        vp.append(vc[:, lp - keep:])
        qa, ka, va, ga, qb, kb, vb, gb, lf, qc, kc, vc = project(
            rmsnorm(hs, norm1_w[l]), w_in[l], w_gla_a2[l], b_gla_a[l], pos_s)
        Sa, oa = retention_chunk(state_ret[l].astype(f32), qa, ka, va, log_gamma)
        Sb, ob = gla_chunk(state_gla[l].astype(f32), qb, kb, vb, lf)
        oc = sample_band_attention(qc, kc, vc, cache_attn_k[l], cache_attn_v[l], rel_bias[l])
        hs = hs + merge_heads(oa, ga, ob, gb, oc, ret_gn_w[l], gla_norm_w[l], w_out[l], hs.dtype)
        hs = hs + channel_mix(hs, norm2_w[l], w_up[l], w_down[l])
        ret_s.append(Sa)
        gla_s.append(Sb)
        k_s.append(kc)
        v_s.append(vc)
    y_prompt = rmsnorm(hp, final_norm_w)
    y_sample = rmsnorm(hs, final_norm_w)
    return (y_prompt, y_sample, jnp.stack(ret_p), jnp.stack(gla_p), jnp.stack(kp), jnp.stack(vp),
            jnp.stack(ret_s), jnp.stack(gla_s), jnp.stack(k_s), jnp.stack(v_s))
```

```python
import functools

import jax
import jax.numpy as jnp
from jax import lax
from jax.experimental import pallas as pl
from jax.experimental.pallas import tpu as pltpu

D_MODEL = 1024
CHUNK = 64
HEAD_DIM = 64
H_RET = 4
H_GLA = 4
H_ATT = 8
W_RET = H_RET * HEAD_DIM
W_GLA = H_GLA * HEAD_DIM
W_ATT = H_ATT * HEAD_DIM
GLA_RANK = 16
GLA_TAU = 16.0
ATT_REACH = 512
REL_MAX = 256
N_REL = CHUNK + REL_MAX
D_FF = 4 * D_MODEL
ROPE_BASE = 10000.0
EPS = 1e-6
PAST_LEN = 4096

LANES = 128
LOW_PAD = LANES
BIAS_W = 640
NEG_BIG = -1e30
VMEM_LIMIT = 56 * 1024 * 1024

F32 = jnp.float32
BF16 = jnp.bfloat16

_NT = (((1,), (1,)), ((), ()))
_TN = (((0,), (0,)), ((), ()))


def _dot(a, b):
    return jnp.dot(a, b, preferred_element_type=F32)


def _dot_nt(a, b):
    return lax.dot_general(a, b, _NT, preferred_element_type=F32)


def _dot_tn(a, b):
    return lax.dot_general(a, b, _TN, preferred_element_type=F32)


def _rmsnorm(x, w):
    return x * lax.rsqrt(jnp.mean(x * x, axis=-1, keepdims=True) + EPS) * w


def _params(n_grid_dims):
    return pltpu.CompilerParams(
        dimension_semantics=("arbitrary",) * n_grid_dims,
        vmem_limit_bytes=VMEM_LIMIT,
    )


def _block_diag_rows(x, n_heads):
    lane_head = lax.broadcasted_iota(jnp.int32, x.shape, 1) // HEAD_DIM
    zero = jnp.zeros_like(x)
    return jnp.concatenate(
        [jnp.where(lane_head == h, x, zero) for h in range(n_heads)], axis=0)


def _in_proj_kernel(x_ref, n1w_ref, wret_ref, wgla_ref, watt_ref, wa2_ref, ba_ref,
                    cos_ref, sin_ref,
                    ret_ref, gla_ref, lf_ref, att_ref, kc_ref, vc_ref):
    hn = _rmsnorm(x_ref[...], n1w_ref[...]).astype(BF16)
    cos = cos_ref[...]
    sin = sin_ref[...]
    first_half = (lax.broadcasted_iota(jnp.int32, cos.shape, 1) % HEAD_DIM) < (HEAD_DIM // 2)

    def rotary(t):
        swapped = jnp.where(first_half, -pltpu.roll(t, LANES - HEAD_DIM // 2, 1),
                            pltpu.roll(t, HEAD_DIM // 2, 1))
        return t * cos + swapped * sin

    scale = HEAD_DIM ** -0.5

    acc = _dot(hn, wret_ref[...])
    for j in range(W_RET // LANES):
        sl = slice(j * LANES, (j + 1) * LANES)
        ret_ref[:, sl] = rotary(acc[:, sl]).astype(BF16)
        sk = slice(W_RET + j * LANES, W_RET + (j + 1) * LANES)
        ret_ref[:, sk] = (rotary(acc[:, sk]) * scale).astype(BF16)
    ret_ref[:, 2 * W_RET:3 * W_RET] = acc[:, 2 * W_RET:3 * W_RET].astype(BF16)
    g = acc[:, 3 * W_RET:4 * W_RET]
    ret_ref[:, 3 * W_RET:4 * W_RET] = (g * jax.nn.sigmoid(g)).astype(BF16)

    acc = _dot(hn, wgla_ref[...])
    gla_ref[:, 0:W_GLA] = (acc[:, 0:W_GLA] * scale).astype(BF16)
    gla_ref[:, W_GLA:3 * W_GLA] = acc[:, W_GLA:3 * W_GLA].astype(BF16)
    g = acc[:, 3 * W_GLA:4 * W_GLA]
    gla_ref[:, 3 * W_GLA:4 * W_GLA] = (g * jax.nn.sigmoid(g)).astype(BF16)
    low = acc[:, 4 * W_GLA:4 * W_GLA + LOW_PAD].astype(BF16)
    z = _dot(low, wa2_ref[...]) + ba_ref[...]
    log_sig = jnp.minimum(z, 0.0) - jnp.log(1.0 + jnp.exp(-jnp.abs(z)))
    lf_ref[...] = log_sig * (1.0 / GLA_TAU)

    acc = _dot(hn, watt_ref[...])
    att_ref[:, 0:W_ATT] = (acc[:, 0:W_ATT] * scale).astype(BF16)
    att_ref[:, W_ATT:3 * W_ATT] = acc[:, W_ATT:3 * W_ATT].astype(BF16)
    kc_ref[...] = acc[:, W_ATT:2 * W_ATT]
    vc_ref[...] = acc[:, 2 * W_ATT:3 * W_ATT]


def _in_proj(x2, n1w, wret, wgla, watt, wa2, ba, cos_t, sin_t, *, tm, seq_len):
    m = x2.shape[0]
    n_tiles = m // tm
    tab_blocks = cos_t.shape[0] // tm
    tiles_per_batch = max(seq_len // tm, 1)
    cache_rows = m // tiles_per_batch
    full = lambda a: pl.BlockSpec(a.shape, lambda i: (0,) * a.ndim)
    row = lambda w: pl.BlockSpec((tm, w), lambda i: (i, 0))
    tab = pl.BlockSpec((tm, LANES), lambda i: (i % tab_blocks, 0))
    cache = pl.BlockSpec((tm, W_ATT), lambda i: (i // tiles_per_batch, 0))
    return pl.pallas_call(
        _in_proj_kernel,
        grid=(n_tiles,),
        in_specs=[row(D_MODEL), full(n1w), full(wret), full(wgla), full(watt),
                  full(wa2), full(ba), tab, tab],
        out_specs=[row(4 * W_RET), row(4 * W_GLA), row(W_GLA), row(3 * W_ATT), cache, cache],
        out_shape=[
            jax.ShapeDtypeStruct((m, 4 * W_RET), BF16),
            jax.ShapeDtypeStruct((m, 4 * W_GLA), BF16),
            jax.ShapeDtypeStruct((m, W_GLA), F32),
            jax.ShapeDtypeStruct((m, 3 * W_ATT), BF16),
            jax.ShapeDtypeStruct((cache_rows, W_ATT), F32),
            jax.ShapeDtypeStruct((cache_rows, W_ATT), F32),
        ],
        compiler_params=_params(1),
        name="in_proj",
    )(x2, n1w, wret, wgla, watt, wa2, ba, cos_t, sin_t)


def _retention_kernel(q_ref, k_ref, v_ref, s0_ref, dtab_ref, qdec_ref, kdec_ref,
                      gtab_ref, bd_ref, o_ref, sfin_ref, s_scr, *, lc, n_chunks):
    g = pl.program_id(1)

    @pl.when(g == 0)
    def _():
        s_scr[...] = s0_ref[0]

    dtab = dtab_ref[...]
    qdec = qdec_ref[...]
    kdec = kdec_ref[...]
    for c in range(n_chunks):
        rows = slice(c * lc, (c + 1) * lc)
        q = q_ref[rows, :]
        k = k_ref[rows, :]
        v = v_ref[rows, :]
        scores = _dot_nt(q, _block_diag_rows(k, H_RET))
        p = (scores * dtab).astype(BF16)
        intra = _dot(p, _block_diag_rows(v, H_RET))
        s_old = s_scr[...]
        inter = _dot(q, s_old.astype(BF16)) * qdec
        kd = (k.astype(F32) * kdec).astype(BF16)
        upd = _dot_tn(kd, v)
        s_scr[...] = s_old * gtab_ref[...] + upd * bd_ref[...]
        o_ref[rows, :] = intra + inter

    @pl.when(g == pl.num_programs(1) - 1)
    def _():
        sfin_ref[0] = s_scr[...]


def _retention(ret, s0, tabs, *, batch, seq_len, lc, n_chunks):
    rows = lc * n_chunks
    ng = seq_len // rows
    m = batch * seq_len
    col = lambda c: pl.BlockSpec((rows, W_RET), lambda b, g: (b * ng + g, c))
    full = lambda a: pl.BlockSpec(a.shape, lambda b, g: (0,) * a.ndim)
    state = pl.BlockSpec((1, W_RET, W_RET), lambda b, g: (b, 0, 0))
    kern = functools.partial(_retention_kernel, lc=lc, n_chunks=n_chunks)
    return pl.pallas_call(
        kern,
        grid=(batch, ng),
        in_specs=[col(0), col(1), col(2), state] + [full(t) for t in tabs],
        out_specs=[pl.BlockSpec((rows, W_RET), lambda b, g: (b * ng + g, 0)), state],
        out_shape=[jax.ShapeDtypeStruct((m, W_RET), F32),
                   jax.ShapeDtypeStruct((batch, W_RET, W_RET), F32)],
        scratch_shapes=[pltpu.VMEM((W_RET, W_RET), F32)],
        compiler_params=_params(2),
        name="retention",
    )(ret, ret, ret, s0, *tabs)


def _retention_tables(lc):
    lg = jnp.log1p(-jnp.exp2(-5.0 - jnp.arange(H_RET, dtype=F32)))
    idx = jnp.arange(lc)
    diff = idx[:, None] - idx[None, :]
    causal = diff >= 0
    dpos = jnp.where(causal, diff, 0).astype(F32)
    decay = jnp.where(causal[None], jnp.exp(dpos[None] * lg[:, None, None]), 0.0)
    dtab = jnp.transpose(decay, (1, 0, 2)).reshape(lc, H_RET * lc)
    q_decay = jnp.exp((idx + 1).astype(F32)[:, None] * lg[None, :])
    k_decay = jnp.exp((lc - 1 - idx).astype(F32)[:, None] * lg[None, :])
    qdec = jnp.repeat(q_decay, HEAD_DIM, axis=1)
    kdec = jnp.repeat(k_decay, HEAD_DIM, axis=1)
    gtab = jnp.broadcast_to(jnp.repeat(jnp.exp(lc * lg), HEAD_DIM)[:, None], (W_RET, W_RET))
    head = jnp.arange(W_RET) // HEAD_DIM
    bd = (head[:, None] == head[None, :]).astype(F32)
    return dtab, qdec, kdec, gtab, bd


def _split_bf16(x):
    hi = x.astype(BF16)
    lo = (x - hi.astype(F32)).astype(BF16)
    return jnp.concatenate([hi, lo], axis=1)


def _gla_kernel(q_ref, k_ref, v_ref, lf_ref, st0_ref, tri_ref, psel_ref, bd_ref,
                o_ref, stfin_ref, st_scr, *, lc, n_chunks):
    g = pl.program_id(1)

    @pl.when(g == 0)
    def _():
        st_scr[...] = st0_ref[0]

    n_lev = lc.bit_length() - 1
    w = W_GLA
    row = lax.broadcasted_iota(jnp.int32, (lc, w), 0)
    srow = lax.broadcasted_iota(jnp.int32, (lc, H_GLA * lc), 0)
    scol = lax.broadcasted_iota(jnp.int32, (lc, H_GLA * lc), 1) % lc
    tri = tri_ref[...]
    psel = psel_ref[...]
    for c in range(n_chunks):
        rows = slice(c * lc, (c + 1) * lc)
        q = q_ref[rows, :]
        k = k_ref[rows, :]
        v = v_ref[rows, :]
        q32 = q.astype(F32)
        k32 = k.astype(F32)
        cs = _dot(tri, _split_bf16(lf_ref[rows, :]))
        b = cs[:, :w] + cs[:, w:]
        rs = _dot(psel, _split_bf16(b))
        scores = jnp.where(srow == scol, _dot_nt(q, _block_diag_rows(k, H_GLA)), 0.0)
        for lev in range(n_lev):
            half = lc >> (lev + 1)
            r = rs[lev * lc:(lev + 1) * lc, :w] + rs[lev * lc:(lev + 1) * lc, w:]
            upper = (row & half) != 0
            d = b - r
            qt = (q32 * jnp.exp(jnp.where(upper, d, NEG_BIG))).astype(BF16)
            kt = (k32 * jnp.exp(jnp.where(upper, NEG_BIG, -d))).astype(BF16)
            sc = _dot_nt(qt, _block_diag_rows(kt, H_GLA))
            shift = half.bit_length()
            same_group = (srow >> shift) == (scol >> shift)
            scores = scores + jnp.where(same_group, sc, 0.0)
        intra = _dot(scores.astype(BF16), _block_diag_rows(v, H_GLA))
        st_old = st_scr[...]
        inter = _dot_nt((q32 * jnp.exp(b)).astype(BF16), st_old.astype(BF16))
        b_last = b[lc - 1:lc, :]
        kd = (k32 * jnp.exp(b_last - b)).astype(BF16)
        upd_t = _dot_tn(v, kd)
        st_scr[...] = st_old * jnp.exp(b_last) + upd_t * bd_ref[...]
        o_ref[rows, :] = intra + inter

    @pl.when(g == pl.num_programs(1) - 1)
    def _():
        stfin_ref[0] = st_scr[...]


def _gla(gla, lf, st0, tabs, *, batch, seq_len, lc, n_chunks):
    rows = lc * n_chunks
    ng = seq_len // rows
    m = batch * seq_len
    col = lambda c: pl.BlockSpec((rows, W_GLA), lambda b, g: (b * ng + g, c))
    full = lambda a: pl.BlockSpec(a.shape, lambda b, g: (0,) * a.ndim)
    state = pl.BlockSpec((1, W_GLA, W_GLA), lambda b, g: (b, 0, 0))
    kern = functools.partial(_gla_kernel, lc=lc, n_chunks=n_chunks)
    return pl.pallas_call(
        kern,
        grid=(batch, ng),
        in_specs=[col(0), col(1), col(2), col(0), state] + [full(t) for t in tabs],
        out_specs=[pl.BlockSpec((rows, W_GLA), lambda b, g: (b * ng + g, 0)), state],
        out_shape=[jax.ShapeDtypeStruct((m, W_GLA), F32),
                   jax.ShapeDtypeStruct((batch, W_GLA, W_GLA), F32)],
        scratch_shapes=[pltpu.VMEM((W_GLA, W_GLA), F32)],
        compiler_params=_params(2),
        name="gla",
    )(gla, gla, gla, lf, st0, *tabs)


def _gla_tables(lc):
    idx = jnp.arange(lc)
    tri = (idx[:, None] >= idx[None, :]).astype(BF16)
    sels = []
    n_lev = lc.bit_length() - 1
    for lev in range(n_lev):
        half = lc >> (lev + 1)
        p = (idx // (2 * half)) * (2 * half) + half - 1
        sels.append((p[:, None] == idx[None, :]).astype(BF16))
    psel = jnp.concatenate(sels, axis=0)
    head = jnp.arange(W_GLA) // HEAD_DIM
    bd = (head[:, None] == head[None, :]).astype(F32)
    return tri, psel, bd


def _band_attn_kernel(q_ref, kc_ref, vc_ref, kp_ref, vp_ref, gv_ref, o_ref,
                      kwin, vwin, bias_scr, *, lc, n_chunks, mask_prefix):
    b_id = pl.program_id(0)
    g = pl.program_id(1)
    win = ATT_REACH + lc
    rows = lc * n_chunks

    @pl.when((b_id == 0) & (g == 0))
    def _():
        for h in range(H_ATT):
            x = jnp.broadcast_to(gv_ref[h:h + 1, :], (lc, BIAS_W))
            bias_scr[h] = pltpu.roll(x, BIAS_W - (lc - 1), 1, stride=1, stride_axis=0)

    kwin[0:ATT_REACH, :] = kp_ref[...].astype(BF16)
    vwin[0:ATT_REACH, :] = vp_ref[...].astype(BF16)
    kwin[ATT_REACH:ATT_REACH + rows, :] = kc_ref[...]
    vwin[ATT_REACH:ATT_REACH + rows, :] = vc_ref[...]

    lane = lax.broadcasted_iota(jnp.int32, (lc, LANES), 1)
    col = lax.broadcasted_iota(jnp.int32, (lc, win), 1)

    def chunk(c, carry):
        r0 = pl.multiple_of(c * lc, lc)
        first_valid = ATT_REACH - (g * rows + r0)
        for pair in range(H_ATT // 2):
            lanes = slice(pair * LANES, (pair + 1) * LANES)
            qp = q_ref[pl.ds(r0, lc), lanes]
            kp = kwin[pl.ds(r0, win), lanes]
            vp = vwin[pl.ds(r0, win), lanes]
            outs = []
            for hh in range(2):
                h = 2 * pair + hh
                qh = jnp.where((lane // HEAD_DIM) == hh, qp, jnp.zeros_like(qp))
                s = _dot_nt(qh, kp) + bias_scr[h][:, :win]
                if mask_prefix:
                    s = jnp.where(col >= first_valid, s, NEG_BIG)
                mx = jnp.max(s, axis=-1, keepdims=True)
                e = jnp.exp(s - mx)
                den = jnp.sum(e, axis=-1, keepdims=True)
                outs.append(_dot(e.astype(BF16), vp) / den)
            o_ref[pl.ds(r0, lc), lanes] = jnp.where(
                lane < HEAD_DIM, outs[0], outs[1]).astype(BF16)
        return carry

    lax.fori_loop(0, n_chunks, chunk, 0)


def _band_attn(att, k_prev, v_prev, gv, *, batch, seq_len, lc, n_chunks, prev_from_att):
    rows = lc * n_chunks
    ng = seq_len // rows
    m = batch * seq_len
    cur = lambda c: pl.BlockSpec((rows, W_ATT), lambda b, g: (b * ng + g, c))
    if prev_from_att:
        prev = lambda c: pl.BlockSpec(
            (ATT_REACH, W_ATT), lambda b, g: (b * ng + jnp.maximum(g - 1, 0), c))
        prev_specs = [prev(1), prev(2)]
    else:
        prev_specs = [pl.BlockSpec((ATT_REACH, W_ATT), lambda b, g: (b, 0))] * 2
    kern = functools.partial(_band_attn_kernel, lc=lc, n_chunks=n_chunks,
                             mask_prefix=prev_from_att)
    return pl.pallas_call(
        kern,
        grid=(batch, ng),
        in_specs=[cur(0), cur(1), cur(2)] + prev_specs
                 + [pl.BlockSpec(gv.shape, lambda b, g: (0, 0))],
        out_specs=pl.BlockSpec((rows, W_ATT), lambda b, g: (b * ng + g, 0)),
        out_shape=jax.ShapeDtypeStruct((m, W_ATT), BF16),
        scratch_shapes=[pltpu.VMEM((ATT_REACH + rows, W_ATT), BF16),
                        pltpu.VMEM((ATT_REACH + rows, W_ATT), BF16),
                        pltpu.VMEM((H_ATT, lc, BIAS_W), F32)],
        compiler_params=_params(2),
        name="band_attn",
    )(att, att, att, k_prev, v_prev, gv)


def _bias_rows(rel_bias_l, lc):
    wv = jnp.arange(BIAS_W)
    dist = ATT_REACH + lc - 1 - wv
    ridx = jnp.clip(dist, -(CHUNK - 1), REL_MAX) + (CHUNK - 1)
    return rel_bias_l.astype(F32)[:, ridx]


def _out_ffn_kernel(x_ref, oa_ref, ob_ref, oc_ref, sga_ref, sgb_ref, gnw_ref, glw_ref,
                    n2w_ref, fnw_ref, ones_ref, wout_ref, wup_ref, wdown_ref, y_ref,
                    *, final_norm):
    ones = ones_ref[...]

    def head_sum(t):
        hi = t.astype(BF16)
        lo = (t - hi.astype(F32)).astype(BF16)
        return _dot(hi, ones) + _dot(lo, ones)

    inv_d = 1.0 / HEAD_DIM
    oa = oa_ref[...]
    mu = head_sum(oa) * inv_d
    da = oa - mu
    var = head_sum(da * da) * inv_d
    a = da * lax.rsqrt(var + EPS) * gnw_ref[...] * sga_ref[...].astype(F32)
    ob = ob_ref[...]
    ms = head_sum(ob * ob) * inv_d
    bb = ob * lax.rsqrt(ms + EPS) * glw_ref[...] * sgb_ref[...].astype(F32)
    cat = jnp.concatenate([a.astype(BF16), bb.astype(BF16), oc_ref[...]], axis=1)
    h1 = x_ref[...] + _dot(cat, wout_ref[...])
    hn = _rmsnorm(h1, n2w_ref[...]).astype(BF16)
    acc = jnp.zeros_like(h1)
    n_slab = D_FF // D_MODEL
    for j in range(n_slab):
        sl = slice(j * D_MODEL, (j + 1) * D_MODEL)
        u = jnp.maximum(_dot(hn, wup_ref[:, sl]), 0.0)
        acc = acc + _dot((u * u).astype(BF16), wdown_ref[sl, :])
    h2 = h1 + acc
    if final_norm:
        h2 = _rmsnorm(h2, fnw_ref[...])
    y_ref[...] = h2


def _out_ffn(x2, oa, ob, oc, ret, gla, gnw, glw, n2w, fnw, ones, wout, wup, wdown,
             *, tm, final_norm):
    m = x2.shape[0]
    full = lambda a: pl.BlockSpec(a.shape, lambda i: (0,) * a.ndim)
    row = lambda w, c=0: pl.BlockSpec((tm, w), lambda i: (i, c))
    kern = functools.partial(_out_ffn_kernel, final_norm=final_norm)
    return pl.pallas_call(
        kern,
        grid=(m // tm,),
        in_specs=[row(D_MODEL), row(W_RET), row(W_GLA), row(W_ATT), row(W_RET, 3),
                  row(W_GLA, 3), full(gnw), full(glw), full(n2w), full(fnw), full(ones),
                  full(wout), full(wup), full(wdown)],
        out_specs=row(D_MODEL),
        out_shape=jax.ShapeDtypeStruct((m, D_MODEL), F32),
        compiler_params=_params(1),
        name="out_ffn",
    )(x2, oa, ob, oc, ret, gla, gnw, glw, n2w, fnw, ones, wout, wup, wdown)


def _rotary_tables(pos, rows):
    half = HEAD_DIM // 2
    inv_freq = ROPE_BASE ** (-jnp.arange(half, dtype=F32) / half)
    ang = pos.astype(F32)[:, None] * inv_freq[None, :]
    reps = (rows // pos.shape[0], LANES // half)
    return jnp.tile(jnp.cos(ang), reps), jnp.tile(jnp.sin(ang), reps)


def _block_diag_state(s):
    bsz, h, a, b = s.shape
    eye = jnp.eye(h, dtype=s.dtype)
    return (s[:, :, :, None, :] * eye[None, :, None, :, None]).reshape(bsz, h * a, h * b)


def _diag_blocks(s, h):
    bsz, ha, hb = s.shape
    s5 = s.reshape(bsz, h, ha // h, h, hb // h)
    return jnp.stack([s5[:, i, :, i, :] for i in range(h)], axis=1)


def _layer_weights(w_in_l, w_a2_l, b_a_l):
    n_main = 4 * W_RET + 4 * W_GLA
    wret = w_in_l[:, :4 * W_RET].astype(BF16)
    low = w_in_l[:, n_main:n_main + GLA_RANK]
    wgla = jnp.concatenate(
        [w_in_l[:, 4 * W_RET:n_main], low,
         jnp.zeros((D_MODEL, LOW_PAD - GLA_RANK), w_in_l.dtype)], axis=1).astype(BF16)
    watt = w_in_l[:, n_main + GLA_RANK:].astype(BF16)
    wa2 = jnp.concatenate(
        [w_a2_l, jnp.zeros((LOW_PAD - GLA_RANK, W_GLA), w_a2_l.dtype)], axis=0).astype(BF16)
    return wret, wgla, watt, wa2, b_a_l.astype(F32)[None, :]


def _run_path(x2, weights, *, batch, seq_len, lc, n_chunks, tm, pos, s_ret0, s_gla0,
              k_prev, v_prev, final_norm):
    (n1w, n2w, fnw, wret, wgla, watt, wa2, ba, gnw, glw, gv, wout, wup, wdown, ones) = weights
    cos_t, sin_t = _rotary_tables(pos, max(seq_len, tm))
    ret, gla, lf, att, kc32, vc32 = _in_proj(
        x2, n1w, wret, wgla, watt, wa2, ba, cos_t, sin_t, tm=tm, seq_len=seq_len)
    oa, s_ret = _retention(ret, s_ret0, _retention_tables(lc),
                           batch=batch, seq_len=seq_len, lc=lc, n_chunks=n_chunks)
    ob, st_gla = _gla(gla, lf, s_gla0, _gla_tables(lc),
                      batch=batch, seq_len=seq_len, lc=lc, n_chunks=n_chunks)
    if k_prev is None:
        oc = _band_attn(att, att, att, gv, batch=batch, seq_len=seq_len, lc=lc,
                        n_chunks=n_chunks, prev_from_att=True)
    else:
        oc = _band_attn(att, k_prev, v_prev, gv, batch=batch, seq_len=seq_len, lc=lc,
                        n_chunks=n_chunks, prev_from_att=False)
    y = _out_ffn(x2, oa, ob, oc, ret, gla, gnw, glw, n2w, fnw, ones, wout, wup, wdown,
                 tm=tm, final_norm=final_norm)
    return y, s_ret, st_gla, kc32, vc32


def kernel(x_prompt, x_sample, state_ret, state_gla, cache_attn_k, cache_attn_v, norm1_w, norm2_w, final_norm_w, w_in, w_gla_a2, b_gla_a, ret_gn_w, gla_norm_w, rel_bias, w_out, w_up, w_down):
    depth = w_in.shape[0]
    bp, lp, _ = x_prompt.shape
    bs, ls, _ = x_sample.shape
    keep = min(ATT_REACH, lp)
    head = jnp.arange(W_RET) // HEAD_DIM
    ones = (head[:, None] == head[None, :]).astype(BF16)
    hp = x_prompt.reshape(bp * lp, D_MODEL)
    hs = x_sample.reshape(bs * ls, D_MODEL)
    pos_p = jnp.arange(lp)
    pos_s = PAST_LEN + jnp.arange(ls)
    zeros_state = jnp.zeros((bp, W_RET, W_RET), F32)
    outs = {k: [] for k in ("ret_p", "gla_p", "kp", "vp", "ret_s", "gla_s", "ks", "vs")}
    for l in range(depth):
        last = l == depth - 1
        wret, wgla, watt, wa2, ba = _layer_weights(w_in[l], w_gla_a2[l], b_gla_a[l])
        row = lambda v: v.astype(F32)[None, :]
        common = (row(norm1_w[l]), row(norm2_w[l]), row(final_norm_w), wret, wgla, watt, wa2, ba,
                  row(ret_gn_w[l]), row(gla_norm_w[l]))
        tail = (w_out[l].astype(BF16), w_up[l].astype(BF16), w_down[l].astype(BF16), ones)
        wts = common + (_bias_rows(rel_bias[l], CHUNK),) + tail
        hp, s_ret, st_gla, kc32, vc32 = _run_path(
            hp, wts, batch=bp, seq_len=lp, lc=CHUNK, n_chunks=ATT_REACH // CHUNK, tm=512,
            pos=pos_p, s_ret0=zeros_state, s_gla0=zeros_state, k_prev=None, v_prev=None,
            final_norm=last)
        outs["ret_p"].append(_diag_blocks(s_ret, H_RET))
        outs["gla_p"].append(jnp.swapaxes(_diag_blocks(st_gla, H_GLA), -1, -2))
        outs["kp"].append(kc32.reshape(bp, keep, H_ATT, HEAD_DIM))
        outs["vp"].append(vc32.reshape(bp, keep, H_ATT, HEAD_DIM))
        wts = common + (_bias_rows(rel_bias[l], ls),) + tail
        s_ret0 = _block_diag_state(state_ret[l].astype(F32))
        s_gla0 = _block_diag_state(jnp.swapaxes(state_gla[l].astype(F32), -1, -2))
        k_prev = cache_attn_k[l].reshape(bs * ATT_REACH, W_ATT)
        v_prev = cache_attn_v[l].reshape(bs * ATT_REACH, W_ATT)
        hs, s_ret, st_gla, kc32, vc32 = _run_path(
            hs, wts, batch=bs, seq_len=ls, lc=ls, n_chunks=1, tm=bs * ls,
            pos=pos_s, s_ret0=s_ret0, s_gla0=s_gla0, k_prev=k_prev, v_prev=v_prev,
            final_norm=last)
        outs["ret_s"].append(_diag_blocks(s_ret, H_RET))
        outs["gla_s"].append(jnp.swapaxes(_diag_blocks(st_gla, H_GLA), -1, -2))
        outs["ks"].append(kc32.reshape(bs, ls, H_ATT, HEAD_DIM))
        outs["vs"].append(vc32.reshape(bs, ls, H_ATT, HEAD_DIM))
    y_prompt = hp.reshape(bp, lp, D_MODEL)
    y_sample = hs.reshape(bs, ls, D_MODEL)
    st = lambda k: jnp.stack(outs[k])
    return (y_prompt, y_sample, st("ret_p"), st("gla_p"), st("kp"), st("vp"),
            st("ret_s"), st("gla_s"), st("ks"), st("vs"))
```

```python
import functools

import jax
import jax.numpy as jnp
from jax import lax
from jax.experimental import pallas as pl
from jax.experimental.pallas import tpu as pltpu

D_MODEL = 1024
CHUNK = 64
HEAD_DIM = 64
H_RET = 4
H_GLA = 4
H_ATT = 8
W_RET = H_RET * HEAD_DIM
W_GLA = H_GLA * HEAD_DIM
W_ATT = H_ATT * HEAD_DIM
GLA_RANK = 16
GLA_TAU = 16.0
ATT_REACH = 512
REL_MAX = 256
N_REL = CHUNK + REL_MAX
D_FF = 4 * D_MODEL
ROPE_BASE = 10000.0
EPS = 1e-6
PAST_LEN = 4096

LANES = 128
LOW_PAD = LANES
BIAS_W = 640
ATT_ROWS_PER_STEP = 128
LOG2E = 1.4426950408889634
NEG_BIG = -1e30
VMEM_LIMIT = 56 * 1024 * 1024

F32 = jnp.float32
BF16 = jnp.bfloat16

_NT = (((1,), (1,)), ((), ()))
_TN = (((0,), (0,)), ((), ()))


def _dot(a, b):
    return jnp.dot(a, b, preferred_element_type=F32)


def _dot_nt(a, b):
    return lax.dot_general(a, b, _NT, preferred_element_type=F32)


def _dot_tn(a, b):
    return lax.dot_general(a, b, _TN, preferred_element_type=F32)


def _rmsnorm(x, w):
    return x * lax.rsqrt(jnp.mean(x * x, axis=-1, keepdims=True) + EPS) * w


def _params(n_grid_dims):
    return pltpu.CompilerParams(
        dimension_semantics=("arbitrary",) * n_grid_dims,
        vmem_limit_bytes=VMEM_LIMIT,
    )


def _block_diag_rows(x, n_heads):
    lane_head = lax.broadcasted_iota(jnp.int32, x.shape, 1) // HEAD_DIM
    zero = jnp.zeros_like(x)
    return jnp.concatenate(
        [jnp.where(lane_head == h, x, zero) for h in range(n_heads)], axis=0)


def _in_proj_kernel(x_ref, n1w_ref, wret_ref, wgla_ref, watt_ref, wa2_ref, ba_ref,
                    cos_ref, sin_ref,
                    ret_ref, gla_ref, lf_ref, att_ref, kc_ref, vc_ref):
    hn = _rmsnorm(x_ref[...], n1w_ref[...]).astype(BF16)
    cos = cos_ref[...]
    sin = sin_ref[...]
    first_half = (lax.broadcasted_iota(jnp.int32, cos.shape, 1) % HEAD_DIM) < (HEAD_DIM // 2)

    def rotary(t):
        swapped = jnp.where(first_half, -pltpu.roll(t, LANES - HEAD_DIM // 2, 1),
                            pltpu.roll(t, HEAD_DIM // 2, 1))
        return t * cos + swapped * sin

    scale = HEAD_DIM ** -0.5

    acc = _dot(hn, wret_ref[...])
    for j in range(W_RET // LANES):
        sl = slice(j * LANES, (j + 1) * LANES)
        ret_ref[:, sl] = rotary(acc[:, sl]).astype(BF16)
        sk = slice(W_RET + j * LANES, W_RET + (j + 1) * LANES)
        ret_ref[:, sk] = (rotary(acc[:, sk]) * scale).astype(BF16)
    ret_ref[:, 2 * W_RET:3 * W_RET] = acc[:, 2 * W_RET:3 * W_RET].astype(BF16)
    g = acc[:, 3 * W_RET:4 * W_RET]
    ret_ref[:, 3 * W_RET:4 * W_RET] = (g * jax.nn.sigmoid(g)).astype(BF16)

    acc = _dot(hn, wgla_ref[...])
    gla_ref[:, 0:W_GLA] = (acc[:, 0:W_GLA] * scale).astype(BF16)
    gla_ref[:, W_GLA:3 * W_GLA] = acc[:, W_GLA:3 * W_GLA].astype(BF16)
    g = acc[:, 3 * W_GLA:4 * W_GLA]
    gla_ref[:, 3 * W_GLA:4 * W_GLA] = (g * jax.nn.sigmoid(g)).astype(BF16)
    low = acc[:, 4 * W_GLA:4 * W_GLA + LOW_PAD].astype(BF16)
    z = _dot(low, wa2_ref[...]) + ba_ref[...]
    log_sig = jnp.minimum(z, 0.0) - jnp.log(1.0 + jnp.exp(-jnp.abs(z)))
    lf_ref[...] = log_sig * (1.0 / GLA_TAU)

    acc = _dot(hn, watt_ref[...])
    att_ref[:, 0:W_ATT] = (acc[:, 0:W_ATT] * (scale * LOG2E)).astype(BF16)
    att_ref[:, W_ATT:3 * W_ATT] = acc[:, W_ATT:3 * W_ATT].astype(BF16)
    kc_ref[...] = acc[:, W_ATT:2 * W_ATT]
    vc_ref[...] = acc[:, 2 * W_ATT:3 * W_ATT]


def _in_proj(x2, n1w, wret, wgla, watt, wa2, ba, cos_t, sin_t, *, tm, seq_len):
    m = x2.shape[0]
    n_tiles = m // tm
    tab_blocks = cos_t.shape[0] // tm
    tiles_per_batch = max(seq_len // tm, 1)
    cache_rows = m // tiles_per_batch
    full = lambda a: pl.BlockSpec(a.shape, lambda i: (0,) * a.ndim)
    row = lambda w: pl.BlockSpec((tm, w), lambda i: (i, 0))
    tab = pl.BlockSpec((tm, LANES), lambda i: (i % tab_blocks, 0))
    cache = pl.BlockSpec((tm, W_ATT), lambda i: (i // tiles_per_batch, 0))
    return pl.pallas_call(
        _in_proj_kernel,
        grid=(n_tiles,),
        in_specs=[row(D_MODEL), full(n1w), full(wret), full(wgla), full(watt),
                  full(wa2), full(ba), tab, tab],
        out_specs=[row(4 * W_RET), row(4 * W_GLA), row(W_GLA), row(3 * W_ATT), cache, cache],
        out_shape=[
            jax.ShapeDtypeStruct((m, 4 * W_RET), BF16),
            jax.ShapeDtypeStruct((m, 4 * W_GLA), BF16),
            jax.ShapeDtypeStruct((m, W_GLA), F32),
            jax.ShapeDtypeStruct((m, 3 * W_ATT), BF16),
            jax.ShapeDtypeStruct((cache_rows, W_ATT), F32),
            jax.ShapeDtypeStruct((cache_rows, W_ATT), F32),
        ],
        compiler_params=_params(1),
        name="in_proj",
    )(x2, n1w, wret, wgla, watt, wa2, ba, cos_t, sin_t)


def _retention_kernel(q_ref, k_ref, v_ref, s0_ref, dtab_ref, qdec_ref, kdec_ref,
                      gtab_ref, bd_ref, o_ref, sfin_ref, s_scr, *, lc, n_chunks):
    g = pl.program_id(1)

    @pl.when(g == 0)
    def _():
        s_scr[...] = s0_ref[0]

    dtab = dtab_ref[...]
    qdec = qdec_ref[...]
    kdec = kdec_ref[...]
    for c in range(n_chunks):
        rows = slice(c * lc, (c + 1) * lc)
        q = q_ref[rows, :]
        k = k_ref[rows, :]
        v = v_ref[rows, :]
        scores = _dot_nt(q, _block_diag_rows(k, H_RET))
        p = (scores * dtab).astype(BF16)
        intra = _dot(p, _block_diag_rows(v, H_RET))
        s_old = s_scr[...]
        inter = _dot(q, s_old.astype(BF16)) * qdec
        kd = (k.astype(F32) * kdec).astype(BF16)
        upd = _dot_tn(kd, v)
        s_scr[...] = s_old * gtab_ref[...] + upd * bd_ref[...]
        o_ref[rows, :] = intra + inter

    @pl.when(g == pl.num_programs(1) - 1)
    def _():
        sfin_ref[0] = s_scr[...]


def _retention(ret, s0, tabs, *, batch, seq_len, lc, n_chunks):
    rows = lc * n_chunks
    ng = seq_len // rows
    m = batch * seq_len
    col = lambda c: pl.BlockSpec((rows, W_RET), lambda b, g: (b * ng + g, c))
    full = lambda a: pl.BlockSpec(a.shape, lambda b, g: (0,) * a.ndim)
    state = pl.BlockSpec((1, W_RET, W_RET), lambda b, g: (b, 0, 0))
    kern = functools.partial(_retention_kernel, lc=lc, n_chunks=n_chunks)
    return pl.pallas_call(
        kern,
        grid=(batch, ng),
        in_specs=[col(0), col(1), col(2), state] + [full(t) for t in tabs],
        out_specs=[pl.BlockSpec((rows, W_RET), lambda b, g: (b * ng + g, 0)), state],
        out_shape=[jax.ShapeDtypeStruct((m, W_RET), F32),
                   jax.ShapeDtypeStruct((batch, W_RET, W_RET), F32)],
        scratch_shapes=[pltpu.VMEM((W_RET, W_RET), F32)],
        compiler_params=_params(2),
        name="retention",
    )(ret, ret, ret, s0, *tabs)


def _retention_tables(lc):
    lg = jnp.log1p(-jnp.exp2(-5.0 - jnp.arange(H_RET, dtype=F32)))
    idx = jnp.arange(lc)
    diff = idx[:, None] - idx[None, :]
    causal = diff >= 0
    dpos = jnp.where(causal, diff, 0).astype(F32)
    decay = jnp.where(causal[None], jnp.exp(dpos[None] * lg[:, None, None]), 0.0)
    dtab = jnp.transpose(decay, (1, 0, 2)).reshape(lc, H_RET * lc)
    q_decay = jnp.exp((idx + 1).astype(F32)[:, None] * lg[None, :])
    k_decay = jnp.exp((lc - 1 - idx).astype(F32)[:, None] * lg[None, :])
    qdec = jnp.repeat(q_decay, HEAD_DIM, axis=1)
    kdec = jnp.repeat(k_decay, HEAD_DIM, axis=1)
    gtab = jnp.broadcast_to(jnp.repeat(jnp.exp(lc * lg), HEAD_DIM)[:, None], (W_RET, W_RET))
    head = jnp.arange(W_RET) // HEAD_DIM
    bd = (head[:, None] == head[None, :]).astype(F32)
    return dtab, qdec, kdec, gtab, bd


def _split_bf16(x):
    hi = x.astype(BF16)
    lo = (x - hi.astype(F32)).astype(BF16)
    return jnp.concatenate([hi, lo], axis=1)


def _gla_kernel(q_ref, k_ref, v_ref, lf_ref, st0_ref, tri_ref, psel_ref, bd_ref,
                o_ref, stfin_ref, st_scr, *, lc, n_chunks):
    g = pl.program_id(1)

    @pl.when(g == 0)
    def _():
        st_scr[...] = st0_ref[0]

    n_lev = lc.bit_length() - 1
    w = W_GLA
    row = lax.broadcasted_iota(jnp.int32, (lc, w), 0)
    srow = lax.broadcasted_iota(jnp.int32, (lc, H_GLA * lc), 0)
    scol = lax.broadcasted_iota(jnp.int32, (lc, H_GLA * lc), 1) % lc
    tri = tri_ref[...]
    psel = psel_ref[...]
    for c in range(n_chunks):
        rows = slice(c * lc, (c + 1) * lc)
        q = q_ref[rows, :]
        k = k_ref[rows, :]
        v = v_ref[rows, :]
        q32 = q.astype(F32)
        k32 = k.astype(F32)
        cs = _dot(tri, _split_bf16(lf_ref[rows, :]))
        b = cs[:, :w] + cs[:, w:]
        rs = _dot(psel, _split_bf16(b))
        scores = jnp.where(srow == scol, _dot_nt(q, _block_diag_rows(k, H_GLA)), 0.0)
        for lev in range(n_lev):
            half = lc >> (lev + 1)
            r = rs[lev * lc:(lev + 1) * lc, :w] + rs[lev * lc:(lev + 1) * lc, w:]
            upper = (row & half) != 0
            d = b - r
            qt = (q32 * jnp.exp(jnp.where(upper, d, NEG_BIG))).astype(BF16)
            kt = (k32 * jnp.exp(jnp.where(upper, NEG_BIG, -d))).astype(BF16)
            sc = _dot_nt(qt, _block_diag_rows(kt, H_GLA))
            shift = half.bit_length()
            same_group = (srow >> shift) == (scol >> shift)
            scores = scores + jnp.where(same_group, sc, 0.0)
        intra = _dot(scores.astype(BF16), _block_diag_rows(v, H_GLA))
        st_old = st_scr[...]
        inter = _dot_nt((q32 * jnp.exp(b)).astype(BF16), st_old.astype(BF16))
        b_last = b[lc - 1:lc, :]
        kd = (k32 * jnp.exp(b_last - b)).astype(BF16)
        upd_t = _dot_tn(v, kd)
        st_scr[...] = st_old * jnp.exp(b_last) + upd_t * bd_ref[...]
        o_ref[rows, :] = intra + inter

    @pl.when(g == pl.num_programs(1) - 1)
    def _():
        stfin_ref[0] = st_scr[...]


def _gla(gla, lf, st0, tabs, *, batch, seq_len, lc, n_chunks):
    rows = lc * n_chunks
    ng = seq_len // rows
    m = batch * seq_len
    col = lambda c: pl.BlockSpec((rows, W_GLA), lambda b, g: (b * ng + g, c))
    full = lambda a: pl.BlockSpec(a.shape, lambda b, g: (0,) * a.ndim)
    state = pl.BlockSpec((1, W_GLA, W_GLA), lambda b, g: (b, 0, 0))
    kern = functools.partial(_gla_kernel, lc=lc, n_chunks=n_chunks)
    return pl.pallas_call(
        kern,
        grid=(batch, ng),
        in_specs=[col(0), col(1), col(2), col(0), state] + [full(t) for t in tabs],
        out_specs=[pl.BlockSpec((rows, W_GLA), lambda b, g: (b * ng + g, 0)), state],
        out_shape=[jax.ShapeDtypeStruct((m, W_GLA), F32),
                   jax.ShapeDtypeStruct((batch, W_GLA, W_GLA), F32)],
        scratch_shapes=[pltpu.VMEM((W_GLA, W_GLA), F32)],
        compiler_params=_params(2),
        name="gla",
    )(gla, gla, gla, lf, st0, *tabs)


def _gla_tables(lc):
    idx = jnp.arange(lc)
    tri = (idx[:, None] >= idx[None, :]).astype(BF16)
    sels = []
    n_lev = lc.bit_length() - 1
    for lev in range(n_lev):
        half = lc >> (lev + 1)
        p = (idx // (2 * half)) * (2 * half) + half - 1
        sels.append((p[:, None] == idx[None, :]).astype(BF16))
    psel = jnp.concatenate(sels, axis=0)
    head = jnp.arange(W_GLA) // HEAD_DIM
    bd = (head[:, None] == head[None, :]).astype(F32)
    return tri, psel, bd


def _band_attn_kernel(q_ref, kc_ref, vc_ref, kp_ref, vp_ref, gv_ref, o_ref,
                      kwin, vwin, bias_scr, s_scr, m_scr, *, lc, rpi, n_iter, mask_prefix):
    b_id = pl.program_id(0)
    g = pl.program_id(1)
    win = ATT_REACH + rpi
    rows = rpi * n_iter

    @pl.when((b_id == 0) & (g == 0))
    def _():
        row = lax.broadcasted_iota(jnp.int32, (rpi, BIAS_W), 0)
        rel = lax.broadcasted_iota(jnp.int32, (rpi, BIAS_W), 1) - (row // lc) * lc
        in_band = (rel >= 0) & (rel < ATT_REACH + lc)
        for h in range(H_ATT):
            x = jnp.broadcast_to(gv_ref[h:h + 1, :], (rpi, BIAS_W))
            x = pltpu.roll(x, BIAS_W - (lc - 1), 1, stride=1, stride_axis=0)
            bias_scr[h] = jnp.where(in_band, x * LOG2E, NEG_BIG)

    kwin[0:ATT_REACH, :] = kp_ref[...].astype(BF16)
    vwin[0:ATT_REACH, :] = vp_ref[...].astype(BF16)
    kwin[ATT_REACH:ATT_REACH + rows, :] = kc_ref[...]
    vwin[ATT_REACH:ATT_REACH + rows, :] = vc_ref[...]

    lane = lax.broadcasted_iota(jnp.int32, (rpi, LANES), 1)
    col = lax.broadcasted_iota(jnp.int32, (rpi, win), 1)

    def step(i, carry, *, masked):
        r0 = pl.multiple_of(i * rpi, rpi)
        first_valid = ATT_REACH - (g * rows + r0)
        for pair in range(H_ATT // 2):
            lanes = slice(pair * LANES, (pair + 1) * LANES)
            qp = q_ref[pl.ds(r0, rpi), lanes]
            kp = kwin[pl.ds(r0, win), lanes]
            for hh in range(2):
                h = 2 * pair + hh
                qh = jnp.where((lane // HEAD_DIM) == hh, qp, jnp.zeros_like(qp))
                s = _dot_nt(qh, kp) + bias_scr[h][:, :win]
                if masked:
                    s = jnp.where(col >= first_valid, s, NEG_BIG)
                s_scr[h, :, :win] = s
                m_scr[h] = jnp.broadcast_to(jnp.max(s, axis=-1, keepdims=True), (rpi, LANES))
        for pair in range(H_ATT // 2):
            lanes = slice(pair * LANES, (pair + 1) * LANES)
            vp = vwin[pl.ds(r0, win), lanes]
            outs = []
            for hh in range(2):
                h = 2 * pair + hh
                e = jnp.exp2(s_scr[h, :, :win] - m_scr[h][:, :1])
                den = jnp.sum(e, axis=-1, keepdims=True)
                outs.append(_dot(e.astype(BF16), vp) / den)
            o_ref[pl.ds(r0, rpi), lanes] = jnp.where(
                lane < HEAD_DIM, outs[0], outs[1]).astype(BF16)
        return carry

    if mask_prefix:
        @pl.when(g == 0)
        def _():
            lax.fori_loop(0, n_iter, functools.partial(step, masked=True), 0)

        @pl.when(g != 0)
        def _():
            lax.fori_loop(0, n_iter, functools.partial(step, masked=False), 0)
    else:
        lax.fori_loop(0, n_iter, functools.partial(step, masked=False), 0)


def _band_attn(att, k_prev, v_prev, gv, *, batch, seq_len, lc, rpi, n_iter, prev_from_att):
    rows = rpi * n_iter
    ng = seq_len // rows
    m = batch * seq_len
    cur = lambda c: pl.BlockSpec((rows, W_ATT), lambda b, g: (b * ng + g, c))
    if prev_from_att:
        prev = lambda c: pl.BlockSpec(
            (ATT_REACH, W_ATT), lambda b, g: (b * ng + jnp.maximum(g - 1, 0), c))
        prev_specs = [prev(1), prev(2)]
    else:
        prev_specs = [pl.BlockSpec((ATT_REACH, W_ATT), lambda b, g: (b, 0))] * 2
    kern = functools.partial(_band_attn_kernel, lc=lc, rpi=rpi, n_iter=n_iter,
                             mask_prefix=prev_from_att)
    return pl.pallas_call(
        kern,
        grid=(batch, ng),
        in_specs=[cur(0), cur(1), cur(2)] + prev_specs
                 + [pl.BlockSpec(gv.shape, lambda b, g: (0, 0))],
        out_specs=pl.BlockSpec((rows, W_ATT), lambda b, g: (b * ng + g, 0)),
        out_shape=jax.ShapeDtypeStruct((m, W_ATT), BF16),
        scratch_shapes=[pltpu.VMEM((ATT_REACH + rows, W_ATT), BF16),
                        pltpu.VMEM((ATT_REACH + rows, W_ATT), BF16),
                        pltpu.VMEM((H_ATT, rpi, BIAS_W), F32),
                        pltpu.VMEM((H_ATT, rpi, BIAS_W), F32),
                        pltpu.VMEM((H_ATT, rpi, LANES), F32)],
        compiler_params=_params(2),
        name="band_attn",
    )(att, att, att, k_prev, v_prev, gv)


def _bias_rows(rel_bias_l, lc):
    wv = jnp.arange(BIAS_W)
    dist = ATT_REACH + lc - 1 - wv
    ridx = jnp.clip(dist, -(CHUNK - 1), REL_MAX) + (CHUNK - 1)
    return rel_bias_l.astype(F32)[:, ridx]


def _out_ffn_kernel(x_ref, oa_ref, ob_ref, oc_ref, sga_ref, sgb_ref, gnw_ref, glw_ref,
                    n2w_ref, fnw_ref, ones_ref, wout_ref, wup_ref, wdown_ref, y_ref,
                    *, final_norm):
    ones = ones_ref[...]

    def head_sum(t):
        hi = t.astype(BF16)
        lo = (t - hi.astype(F32)).astype(BF16)
        return _dot(hi, ones) + _dot(lo, ones)

    inv_d = 1.0 / HEAD_DIM
    oa = oa_ref[...]
    mu = head_sum(oa) * inv_d
    da = oa - mu
    var = head_sum(da * da) * inv_d
    a = da * lax.rsqrt(var + EPS) * gnw_ref[...] * sga_ref[...].astype(F32)
    ob = ob_ref[...]
    ms = head_sum(ob * ob) * inv_d
    bb = ob * lax.rsqrt(ms + EPS) * glw_ref[...] * sgb_ref[...].astype(F32)
    cat = jnp.concatenate([a.astype(BF16), bb.astype(BF16), oc_ref[...]], axis=1)
    h1 = x_ref[...] + _dot(cat, wout_ref[...])
    hn = _rmsnorm(h1, n2w_ref[...]).astype(BF16)
    acc = jnp.zeros_like(h1)
    n_slab = D_FF // D_MODEL
    for j in range(n_slab):
        sl = slice(j * D_MODEL, (j + 1) * D_MODEL)
        u = jnp.maximum(_dot(hn, wup_ref[:, sl]), 0.0)
        acc = acc + _dot((u * u).astype(BF16), wdown_ref[sl, :])
    h2 = h1 + acc
    if final_norm:
        h2 = _rmsnorm(h2, fnw_ref[...])
    y_ref[...] = h2


def _out_ffn(x2, oa, ob, oc, ret, gla, gnw, glw, n2w, fnw, ones, wout, wup, wdown,
             *, tm, final_norm):
    m = x2.shape[0]
    full = lambda a: pl.BlockSpec(a.shape, lambda i: (0,) * a.ndim)
    row = lambda w, c=0: pl.BlockSpec((tm, w), lambda i: (i, c))
    kern = functools.partial(_out_ffn_kernel, final_norm=final_norm)
    return pl.pallas_call(
        kern,
        grid=(m // tm,),
        in_specs=[row(D_MODEL), row(W_RET), row(W_GLA), row(W_ATT), row(W_RET, 3),
                  row(W_GLA, 3), full(gnw), full(glw), full(n2w), full(fnw), full(ones),
                  full(wout), full(wup), full(wdown)],
        out_specs=row(D_MODEL),
        out_shape=jax.ShapeDtypeStruct((m, D_MODEL), F32),
        compiler_params=_params(1),
        name="out_ffn",
    )(x2, oa, ob, oc, ret, gla, gnw, glw, n2w, fnw, ones, wout, wup, wdown)


def _rotary_tables(pos, rows):
    half = HEAD_DIM // 2
    inv_freq = ROPE_BASE ** (-jnp.arange(half, dtype=F32) / half)
    ang = pos.astype(F32)[:, None] * inv_freq[None, :]
    reps = (rows // pos.shape[0], LANES // half)
    return jnp.tile(jnp.cos(ang), reps), jnp.tile(jnp.sin(ang), reps)


def _block_diag_state(s):
    bsz, h, a, b = s.shape
    eye = jnp.eye(h, dtype=s.dtype)
    return (s[:, :, :, None, :] * eye[None, :, None, :, None]).reshape(bsz, h * a, h * b)


def _diag_blocks(s, h):
    bsz, ha, hb = s.shape
    s5 = s.reshape(bsz, h, ha // h, h, hb // h)
    return jnp.stack([s5[:, i, :, i, :] for i in range(h)], axis=1)


def _layer_weights(w_in_l, w_a2_l, b_a_l):
    n_main = 4 * W_RET + 4 * W_GLA
    wret = w_in_l[:, :4 * W_RET].astype(BF16)
    low = w_in_l[:, n_main:n_main + GLA_RANK]
    wgla = jnp.concatenate(
        [w_in_l[:, 4 * W_RET:n_main], low,
         jnp.zeros((D_MODEL, LOW_PAD - GLA_RANK), w_in_l.dtype)], axis=1).astype(BF16)
    watt = w_in_l[:, n_main + GLA_RANK:].astype(BF16)
    wa2 = jnp.concatenate(
        [w_a2_l, jnp.zeros((LOW_PAD - GLA_RANK, W_GLA), w_a2_l.dtype)], axis=0).astype(BF16)
    return wret, wgla, watt, wa2, b_a_l.astype(F32)[None, :]


def _run_path(x2, weights, *, batch, seq_len, lc, n_chunks, tm, pos, s_ret0, s_gla0,
              k_prev, v_prev, final_norm):
    (n1w, n2w, fnw, wret, wgla, watt, wa2, ba, gnw, glw, gv, wout, wup, wdown, ones) = weights
    cos_t, sin_t = _rotary_tables(pos, max(seq_len, tm))
    ret, gla, lf, att, kc32, vc32 = _in_proj(
        x2, n1w, wret, wgla, watt, wa2, ba, cos_t, sin_t, tm=tm, seq_len=seq_len)
    oa, s_ret = _retention(ret, s_ret0, _retention_tables(lc),
                           batch=batch, seq_len=seq_len, lc=lc, n_chunks=n_chunks)
    ob, st_gla = _gla(gla, lf, s_gla0, _gla_tables(lc),
                      batch=batch, seq_len=seq_len, lc=lc, n_chunks=n_chunks)
    rpi = min(lc * n_chunks, ATT_ROWS_PER_STEP)
    n_iter = lc * n_chunks // rpi
    if k_prev is None:
        oc = _band_attn(att, att, att, gv, batch=batch, seq_len=seq_len, lc=lc,
                        rpi=rpi, n_iter=n_iter, prev_from_att=True)
    else:
        oc = _band_attn(att, k_prev, v_prev, gv, batch=batch, seq_len=seq_len, lc=lc,
                        rpi=rpi, n_iter=n_iter, prev_from_att=False)
    y = _out_ffn(x2, oa, ob, oc, ret, gla, gnw, glw, n2w, fnw, ones, wout, wup, wdown,
                 tm=tm, final_norm=final_norm)
    return y, s_ret, st_gla, kc32, vc32


def kernel(x_prompt, x_sample, state_ret, state_gla, cache_attn_k, cache_attn_v, norm1_w, norm2_w, final_norm_w, w_in, w_gla_a2, b_gla_a, ret_gn_w, gla_norm_w, rel_bias, w_out, w_up, w_down):
    depth = w_in.shape[0]
    bp, lp, _ = x_prompt.shape
    bs, ls, _ = x_sample.shape
    keep = min(ATT_REACH, lp)
    head = jnp.arange(W_RET) // HEAD_DIM
    ones = (head[:, None] == head[None, :]).astype(BF16)
    hp = x_prompt.reshape(bp * lp, D_MODEL)
    hs = x_sample.reshape(bs * ls, D_MODEL)
    pos_p = jnp.arange(lp)
    pos_s = PAST_LEN + jnp.arange(ls)
    zeros_state = jnp.zeros((bp, W_RET, W_RET), F32)
    outs = {k: [] for k in ("ret_p", "gla_p", "kp", "vp", "ret_s", "gla_s", "ks", "vs")}
    for l in range(depth):
        last = l == depth - 1
        wret, wgla, watt, wa2, ba = _layer_weights(w_in[l], w_gla_a2[l], b_gla_a[l])
        row = lambda v: v.astype(F32)[None, :]
        common = (row(norm1_w[l]), row(norm2_w[l]), row(final_norm_w), wret, wgla, watt, wa2, ba,
                  row(ret_gn_w[l]), row(gla_norm_w[l]))
        tail = (w_out[l].astype(BF16), w_up[l].astype(BF16), w_down[l].astype(BF16), ones)
        wts = common + (_bias_rows(rel_bias[l], CHUNK),) + tail
        hp, s_ret, st_gla, kc32, vc32 = _run_path(
            hp, wts, batch=bp, seq_len=lp, lc=CHUNK, n_chunks=ATT_REACH // CHUNK, tm=512,
            pos=pos_p, s_ret0=zeros_state, s_gla0=zeros_state, k_prev=None, v_prev=None,
            final_norm=last)
        outs["ret_p"].append(_diag_blocks(s_ret, H_RET))
        outs["gla_p"].append(jnp.swapaxes(_diag_blocks(st_gla, H_GLA), -1, -2))
        outs["kp"].append(kc32.reshape(bp, keep, H_ATT, HEAD_DIM))
        outs["vp"].append(vc32.reshape(bp, keep, H_ATT, HEAD_DIM))
        wts = common + (_bias_rows(rel_bias[l], ls),) + tail
        s_ret0 = _block_diag_state(state_ret[l].astype(F32))
        s_gla0 = _block_diag_state(jnp.swapaxes(state_gla[l].astype(F32), -1, -2))
        k_prev = cache_attn_k[l].reshape(bs * ATT_REACH, W_ATT)
        v_prev = cache_attn_v[l].reshape(bs * ATT_REACH, W_ATT)
        hs, s_ret, st_gla, kc32, vc32 = _run_path(
            hs, wts, batch=bs, seq_len=ls, lc=ls, n_chunks=1, tm=bs * ls,
            pos=pos_s, s_ret0=s_ret0, s_gla0=s_gla0, k_prev=k_prev, v_prev=v_prev,
            final_norm=last)
        outs["ret_s"].append(_diag_blocks(s_ret, H_RET))
        outs["gla_s"].append(jnp.swapaxes(_diag_blocks(st_gla, H_GLA), -1, -2))
        outs["ks"].append(kc32.reshape(bs, ls, H_ATT, HEAD_DIM))
        outs["vs"].append(vc32.reshape(bs, ls, H_ATT, HEAD_DIM))
    y_prompt = hp.reshape(bp, lp, D_MODEL)
    y_sample = hs.reshape(bs, ls, D_MODEL)
    st = lambda k: jnp.stack(outs[k])
    return (y_prompt, y_sample, st("ret_p"), st("gla_p"), st("kp"), st("vp"),
            st("ret_s"), st("gla_s"), st("ks"), st("vs"))
```

```python
import functools

import jax
import jax.numpy as jnp
from jax import lax
from jax.experimental import pallas as pl
from jax.experimental.pallas import tpu as pltpu

D_MODEL = 1024
CHUNK = 64
HEAD_DIM = 64
H_RET = 4
H_GLA = 4
H_ATT = 8
W_RET = H_RET * HEAD_DIM
W_GLA = H_GLA * HEAD_DIM
W_ATT = H_ATT * HEAD_DIM
GLA_RANK = 16
GLA_TAU = 16.0
ATT_REACH = 512
REL_MAX = 256
N_REL = CHUNK + REL_MAX
D_FF = 4 * D_MODEL
ROPE_BASE = 10000.0
EPS = 1e-6
PAST_LEN = 4096

LANES = 128
LOW_PAD = LANES
BIAS_W = 640
GLA_SAFE_LOG_DECAY = 50.0
ATT_ROWS_PER_STEP = 128
LOG2E = 1.4426950408889634
NEG_BIG = -1e30
VMEM_LIMIT = 56 * 1024 * 1024

F32 = jnp.float32
BF16 = jnp.bfloat16

_NT = (((1,), (1,)), ((), ()))
_TN = (((0,), (0,)), ((), ()))


def _dot(a, b):
    return jnp.dot(a, b, preferred_element_type=F32)


def _dot_nt(a, b):
    return lax.dot_general(a, b, _NT, preferred_element_type=F32)


def _dot_tn(a, b):
    return lax.dot_general(a, b, _TN, preferred_element_type=F32)


def _rmsnorm(x, w):
    return x * lax.rsqrt(jnp.mean(x * x, axis=-1, keepdims=True) + EPS) * w


def _params(n_grid_dims):
    return pltpu.CompilerParams(
        dimension_semantics=("arbitrary",) * n_grid_dims,
        vmem_limit_bytes=VMEM_LIMIT,
    )


def _block_diag_rows(x, n_heads):
    lane_head = lax.broadcasted_iota(jnp.int32, x.shape, 1) // HEAD_DIM
    zero = jnp.zeros_like(x)
    return jnp.concatenate(
        [jnp.where(lane_head == h, x, zero) for h in range(n_heads)], axis=0)


def _in_proj_kernel(x_ref, n1w_ref, wret_ref, wgla_ref, watt_ref, wa2_ref, ba_ref,
                    cos_ref, sin_ref,
                    ret_ref, gla_ref, lf_ref, att_ref, kc_ref, vc_ref):
    hn = _rmsnorm(x_ref[...], n1w_ref[...]).astype(BF16)
    cos = cos_ref[...]
    sin = sin_ref[...]
    first_half = (lax.broadcasted_iota(jnp.int32, cos.shape, 1) % HEAD_DIM) < (HEAD_DIM // 2)

    def rotary(t):
        swapped = jnp.where(first_half, -pltpu.roll(t, LANES - HEAD_DIM // 2, 1),
                            pltpu.roll(t, HEAD_DIM // 2, 1))
        return t * cos + swapped * sin

    scale = HEAD_DIM ** -0.5

    acc = _dot(hn, wret_ref[...])
    for j in range(W_RET // LANES):
        sl = slice(j * LANES, (j + 1) * LANES)
        ret_ref[:, sl] = rotary(acc[:, sl]).astype(BF16)
        sk = slice(W_RET + j * LANES, W_RET + (j + 1) * LANES)
        ret_ref[:, sk] = (rotary(acc[:, sk]) * scale).astype(BF16)
    ret_ref[:, 2 * W_RET:3 * W_RET] = acc[:, 2 * W_RET:3 * W_RET].astype(BF16)
    g = acc[:, 3 * W_RET:4 * W_RET]
    ret_ref[:, 3 * W_RET:4 * W_RET] = (g * jax.nn.sigmoid(g)).astype(BF16)

    acc = _dot(hn, wgla_ref[...])
    gla_ref[:, 0:W_GLA] = (acc[:, 0:W_GLA] * scale).astype(BF16)
    gla_ref[:, W_GLA:3 * W_GLA] = acc[:, W_GLA:3 * W_GLA].astype(BF16)
    g = acc[:, 3 * W_GLA:4 * W_GLA]
    gla_ref[:, 3 * W_GLA:4 * W_GLA] = (g * jax.nn.sigmoid(g)).astype(BF16)
    low = acc[:, 4 * W_GLA:4 * W_GLA + LOW_PAD].astype(BF16)
    z = _dot(low, wa2_ref[...]) + ba_ref[...]
    log_sig = jnp.minimum(z, 0.0) - jnp.log(1.0 + jnp.exp(-jnp.abs(z)))
    lf_ref[...] = log_sig * (1.0 / GLA_TAU)

    acc = _dot(hn, watt_ref[...])
    att_ref[:, 0:W_ATT] = (acc[:, 0:W_ATT] * (scale * LOG2E)).astype(BF16)
    att_ref[:, W_ATT:3 * W_ATT] = acc[:, W_ATT:3 * W_ATT].astype(BF16)
    kc_ref[...] = acc[:, W_ATT:2 * W_ATT]
    vc_ref[...] = acc[:, 2 * W_ATT:3 * W_ATT]


def _in_proj(x2, n1w, wret, wgla, watt, wa2, ba, cos_t, sin_t, *, tm, seq_len):
    m = x2.shape[0]
    n_tiles = m // tm
    tab_blocks = cos_t.shape[0] // tm
    tiles_per_batch = max(seq_len // tm, 1)
    cache_rows = m // tiles_per_batch
    full = lambda a: pl.BlockSpec(a.shape, lambda i: (0,) * a.ndim)
    row = lambda w: pl.BlockSpec((tm, w), lambda i: (i, 0))
    tab = pl.BlockSpec((tm, LANES), lambda i: (i % tab_blocks, 0))
    cache = pl.BlockSpec((tm, W_ATT), lambda i: (i // tiles_per_batch, 0))
    return pl.pallas_call(
        _in_proj_kernel,
        grid=(n_tiles,),
        in_specs=[row(D_MODEL), full(n1w), full(wret), full(wgla), full(watt),
                  full(wa2), full(ba), tab, tab],
        out_specs=[row(4 * W_RET), row(4 * W_GLA), row(W_GLA), row(3 * W_ATT), cache, cache],
        out_shape=[
            jax.ShapeDtypeStruct((m, 4 * W_RET), BF16),
            jax.ShapeDtypeStruct((m, 4 * W_GLA), BF16),
            jax.ShapeDtypeStruct((m, W_GLA), F32),
            jax.ShapeDtypeStruct((m, 3 * W_ATT), BF16),
            jax.ShapeDtypeStruct((cache_rows, W_ATT), F32),
            jax.ShapeDtypeStruct((cache_rows, W_ATT), F32),
        ],
        compiler_params=_params(1),
        name="in_proj",
    )(x2, n1w, wret, wgla, watt, wa2, ba, cos_t, sin_t)


def _retention_kernel(q_ref, k_ref, v_ref, s0_ref, dtab_ref, qdec_ref, kdec_ref,
                      gtab_ref, bd_ref, o_ref, sfin_ref, s_scr, *, lc, n_chunks):
    g = pl.program_id(1)

    @pl.when(g == 0)
    def _():
        s_scr[...] = s0_ref[0]

    dtab = dtab_ref[...]
    qdec = qdec_ref[...]
    kdec = kdec_ref[...]
    for c in range(n_chunks):
        rows = slice(c * lc, (c + 1) * lc)
        q = q_ref[rows, :]
        k = k_ref[rows, :]
        v = v_ref[rows, :]
        scores = _dot_nt(q, _block_diag_rows(k, H_RET))
        p = (scores * dtab).astype(BF16)
        intra = _dot(p, _block_diag_rows(v, H_RET))
        s_old = s_scr[...]
        inter = _dot(q, s_old.astype(BF16)) * qdec
        kd = (k.astype(F32) * kdec).astype(BF16)
        upd = _dot_tn(kd, v)
        s_scr[...] = s_old * gtab_ref[...] + upd * bd_ref[...]
        o_ref[rows, :] = intra + inter

    @pl.when(g == pl.num_programs(1) - 1)
    def _():
        sfin_ref[0] = s_scr[...]


def _retention(ret, s0, tabs, *, batch, seq_len, lc, n_chunks):
    rows = lc * n_chunks
    ng = seq_len // rows
    m = batch * seq_len
    col = lambda c: pl.BlockSpec((rows, W_RET), lambda b, g: (b * ng + g, c))
    full = lambda a: pl.BlockSpec(a.shape, lambda b, g: (0,) * a.ndim)
    state = pl.BlockSpec((1, W_RET, W_RET), lambda b, g: (b, 0, 0))
    kern = functools.partial(_retention_kernel, lc=lc, n_chunks=n_chunks)
    return pl.pallas_call(
        kern,
        grid=(batch, ng),
        in_specs=[col(0), col(1), col(2), state] + [full(t) for t in tabs],
        out_specs=[pl.BlockSpec((rows, W_RET), lambda b, g: (b * ng + g, 0)), state],
        out_shape=[jax.ShapeDtypeStruct((m, W_RET), F32),
                   jax.ShapeDtypeStruct((batch, W_RET, W_RET), F32)],
        scratch_shapes=[pltpu.VMEM((W_RET, W_RET), F32)],
        compiler_params=_params(2),
        name="retention",
    )(ret, ret, ret, s0, *tabs)


def _retention_tables(lc):
    lg = jnp.log1p(-jnp.exp2(-5.0 - jnp.arange(H_RET, dtype=F32)))
    idx = jnp.arange(lc)
    diff = idx[:, None] - idx[None, :]
    causal = diff >= 0
    dpos = jnp.where(causal, diff, 0).astype(F32)
    decay = jnp.where(causal[None], jnp.exp(dpos[None] * lg[:, None, None]), 0.0)
    dtab = jnp.transpose(decay, (1, 0, 2)).reshape(lc, H_RET * lc)
    q_decay = jnp.exp((idx + 1).astype(F32)[:, None] * lg[None, :])
    k_decay = jnp.exp((lc - 1 - idx).astype(F32)[:, None] * lg[None, :])
    qdec = jnp.repeat(q_decay, HEAD_DIM, axis=1)
    kdec = jnp.repeat(k_decay, HEAD_DIM, axis=1)
    gtab = jnp.broadcast_to(jnp.repeat(jnp.exp(lc * lg), HEAD_DIM)[:, None], (W_RET, W_RET))
    head = jnp.arange(W_RET) // HEAD_DIM
    bd = (head[:, None] == head[None, :]).astype(F32)
    return dtab, qdec, kdec, gtab, bd


def _split_bf16(x):
    hi = x.astype(BF16)
    lo = (x - hi.astype(F32)).astype(BF16)
    return jnp.concatenate([hi, lo], axis=1)


def _group_reference(b, row, half):
    n, w = b.shape
    group = 2 * half
    if group >= 8:
        b3 = b.reshape(n // group, group, w)
        return jnp.broadcast_to(b3[:, half - 1:half, :], b3.shape).reshape(n, w)
    if group == 4:
        i = row & 3
        return jnp.where(i == 0, pltpu.roll(b, n - 1, 0),
                         jnp.where(i == 1, b,
                                   jnp.where(i == 2, pltpu.roll(b, 1, 0), pltpu.roll(b, 2, 0))))
    return jnp.where((row & 1) == 1, pltpu.roll(b, 1, 0), b)


def _gla_kernel(q_ref, k_ref, v_ref, lf_ref, st0_ref, tri_ref, bd_ref,
                o_ref, stfin_ref, st_scr, sc_scr, *, lc, n_chunks):
    g = pl.program_id(1)

    @pl.when(g == 0)
    def _():
        st_scr[...] = st0_ref[0]

    n_lev = lc.bit_length() - 1
    w = W_GLA
    n = lc * n_chunks
    chunk_rows = [slice(c * lc, (c + 1) * lc) for c in range(n_chunks)]
    row = lax.broadcasted_iota(jnp.int32, (n, w), 0)
    trow = lax.broadcasted_iota(jnp.int32, (lc, H_GLA * lc), 0)
    scol = lax.broadcasted_iota(jnp.int32, (lc, H_GLA * lc), 1) & (lc - 1)
    q = q_ref[...]
    k = k_ref[...]
    v = v_ref[...]
    q32 = q.astype(F32)
    k32 = k.astype(F32)
    tri = tri_ref[...]
    lfs = _split_bf16(lf_ref[...])
    cs = jnp.concatenate([_dot(tri, lfs[r, :]) for r in chunk_rows], axis=0)
    b = cs[:, :w] + cs[:, w:]

    qe = (q32 * jnp.exp(b)).astype(BF16)
    mild = jnp.min(b) >= -GLA_SAFE_LOG_DECAY

    @pl.when(mild)
    def _():
        ke = (k32 * jnp.exp(-b)).astype(BF16)
        for r in chunk_rows:
            sc = _dot_nt(qe[r, :], _block_diag_rows(ke[r, :], H_GLA))
            sc_scr[r, :] = jnp.where(trow >= scol, sc, 0.0)

    @pl.when(jnp.logical_not(mild))
    def _():
        for r in chunk_rows:
            sc = _dot_nt(q[r, :], _block_diag_rows(k[r, :], H_GLA))
            sc_scr[r, :] = jnp.where(trow == scol, sc, 0.0)
        for lev in range(n_lev):
            half = lc >> (lev + 1)
            shift = half.bit_length()
            upper = (row & half) != 0
            e = jnp.exp(-jnp.abs(b - _group_reference(b, row, half)))
            z = (jnp.where(upper, q32, k32) * e).astype(BF16)
            valid = (((trow >> shift) == (scol >> shift))
                     & ((trow & half) != 0) & ((scol & half) == 0))
            for r in chunk_rows:
                zc = z[r, :]
                sc = _dot_nt(zc, _block_diag_rows(zc, H_GLA))
                sc_scr[r, :] = jnp.where(valid, sc, sc_scr[r, :])

    b3 = b.reshape(n_chunks, lc, w)
    b_last = b3[:, lc - 1:lc, :]
    kd = (k32 * jnp.exp(jnp.broadcast_to(b_last, b3.shape).reshape(n, w) - b)).astype(BF16)
    decay = jnp.exp(b_last)
    bd = bd_ref[...]
    for c, r in enumerate(chunk_rows):
        st_old = st_scr[...]
        inter = _dot_nt(qe[r, :], st_old.astype(BF16))
        intra = _dot(sc_scr[r, :].astype(BF16), _block_diag_rows(v[r, :], H_GLA))
        upd_t = _dot_tn(v[r, :], kd[r, :])
        st_scr[...] = st_old * decay[c] + upd_t * bd
        o_ref[r, :] = intra + inter

    @pl.when(g == pl.num_programs(1) - 1)
    def _():
        stfin_ref[0] = st_scr[...]


def _gla(gla, lf, st0, tabs, *, batch, seq_len, lc, n_chunks):
    rows = lc * n_chunks
    ng = seq_len // rows
    m = batch * seq_len
    col = lambda c: pl.BlockSpec((rows, W_GLA), lambda b, g: (b * ng + g, c))
    full = lambda a: pl.BlockSpec(a.shape, lambda b, g: (0,) * a.ndim)
    state = pl.BlockSpec((1, W_GLA, W_GLA), lambda b, g: (b, 0, 0))
    kern = functools.partial(_gla_kernel, lc=lc, n_chunks=n_chunks)
    return pl.pallas_call(
        kern,
        grid=(batch, ng),
        in_specs=[col(0), col(1), col(2), col(0), state] + [full(t) for t in tabs],
        out_specs=[pl.BlockSpec((rows, W_GLA), lambda b, g: (b * ng + g, 0)), state],
        out_shape=[jax.ShapeDtypeStruct((m, W_GLA), F32),
                   jax.ShapeDtypeStruct((batch, W_GLA, W_GLA), F32)],
        scratch_shapes=[pltpu.VMEM((W_GLA, W_GLA), F32),
                        pltpu.VMEM((rows, H_GLA * lc), F32)],
        compiler_params=_params(2),
        name="gla",
    )(gla, gla, gla, lf, st0, *tabs)


def _gla_tables(lc):
    idx = jnp.arange(lc)
    tri = (idx[:, None] >= idx[None, :]).astype(BF16)
    head = jnp.arange(W_GLA) // HEAD_DIM
    bd = (head[:, None] == head[None, :]).astype(F32)
    return tri, bd


def _band_attn_kernel(q_ref, kc_ref, vc_ref, kp_ref, vp_ref, gv_ref, o_ref,
                      kwin, vwin, bias_scr, s_scr, m_scr, *, lc, rpi, n_iter, mask_prefix):
    b_id = pl.program_id(0)
    g = pl.program_id(1)
    win = ATT_REACH + rpi
    rows = rpi * n_iter

    @pl.when((b_id == 0) & (g == 0))
    def _():
        row = lax.broadcasted_iota(jnp.int32, (rpi, BIAS_W), 0)
        rel = lax.broadcasted_iota(jnp.int32, (rpi, BIAS_W), 1) - (row // lc) * lc
        in_band = (rel >= 0) & (rel < ATT_REACH + lc)
        for h in range(H_ATT):
            x = jnp.broadcast_to(gv_ref[h:h + 1, :], (rpi, BIAS_W))
            x = pltpu.roll(x, BIAS_W - (lc - 1), 1, stride=1, stride_axis=0)
            bias_scr[h] = jnp.where(in_band, x * LOG2E, NEG_BIG)

    kwin[0:ATT_REACH, :] = kp_ref[...].astype(BF16)
    vwin[0:ATT_REACH, :] = vp_ref[...].astype(BF16)
    kwin[ATT_REACH:ATT_REACH + rows, :] = kc_ref[...]
    vwin[ATT_REACH:ATT_REACH + rows, :] = vc_ref[...]

    lane = lax.broadcasted_iota(jnp.int32, (rpi, LANES), 1)
    col = lax.broadcasted_iota(jnp.int32, (rpi, win), 1)

    def step(i, carry, *, masked):
        r0 = pl.multiple_of(i * rpi, rpi)
        first_valid = ATT_REACH - (g * rows + r0)
        for pair in range(H_ATT // 2):
            lanes = slice(pair * LANES, (pair + 1) * LANES)
            qp = q_ref[pl.ds(r0, rpi), lanes]
            kp = kwin[pl.ds(r0, win), lanes]
            for hh in range(2):
                h = 2 * pair + hh
                qh = jnp.where((lane // HEAD_DIM) == hh, qp, jnp.zeros_like(qp))
                s = _dot_nt(qh, kp) + bias_scr[h][:, :win]
                if masked:
                    s = jnp.where(col >= first_valid, s, NEG_BIG)
                s_scr[h, :, :win] = s
                m_scr[h] = jnp.broadcast_to(jnp.max(s, axis=-1, keepdims=True), (rpi, LANES))
        for pair in range(H_ATT // 2):
            lanes = slice(pair * LANES, (pair + 1) * LANES)
            vp = vwin[pl.ds(r0, win), lanes]
            outs = []
            for hh in range(2):
                h = 2 * pair + hh
                e = jnp.exp2(s_scr[h, :, :win] - m_scr[h][:, :1])
                den = jnp.sum(e, axis=-1, keepdims=True)
                outs.append(_dot(e.astype(BF16), vp) / den)
            o_ref[pl.ds(r0, rpi), lanes] = jnp.where(
                lane < HEAD_DIM, outs[0], outs[1]).astype(BF16)
        return carry

    if mask_prefix:
        @pl.when(g == 0)
        def _():
            lax.fori_loop(0, n_iter, functools.partial(step, masked=True), 0)

        @pl.when(g != 0)
        def _():
            lax.fori_loop(0, n_iter, functools.partial(step, masked=False), 0)
    else:
        lax.fori_loop(0, n_iter, functools.partial(step, masked=False), 0)


def _band_attn(att, k_prev, v_prev, gv, *, batch, seq_len, lc, rpi, n_iter, prev_from_att):
    rows = rpi * n_iter
    ng = seq_len // rows
    m = batch * seq_len
    cur = lambda c: pl.BlockSpec((rows, W_ATT), lambda b, g: (b * ng + g, c))
    if prev_from_att:
        prev = lambda c: pl.BlockSpec(
            (ATT_REACH, W_ATT), lambda b, g: (b * ng + jnp.maximum(g - 1, 0), c))
        prev_specs = [prev(1), prev(2)]
    else:
        prev_specs = [pl.BlockSpec((ATT_REACH, W_ATT), lambda b, g: (b, 0))] * 2
    kern = functools.partial(_band_attn_kernel, lc=lc, rpi=rpi, n_iter=n_iter,
                             mask_prefix=prev_from_att)
    return pl.pallas_call(
        kern,
        grid=(batch, ng),
        in_specs=[cur(0), cur(1), cur(2)] + prev_specs
                 + [pl.BlockSpec(gv.shape, lambda b, g: (0, 0))],
        out_specs=pl.BlockSpec((rows, W_ATT), lambda b, g: (b * ng + g, 0)),
        out_shape=jax.ShapeDtypeStruct((m, W_ATT), BF16),
        scratch_shapes=[pltpu.VMEM((ATT_REACH + rows, W_ATT), BF16),
                        pltpu.VMEM((ATT_REACH + rows, W_ATT), BF16),
                        pltpu.VMEM((H_ATT, rpi, BIAS_W), F32),
                        pltpu.VMEM((H_ATT, rpi, BIAS_W), F32),
                        pltpu.VMEM((H_ATT, rpi, LANES), F32)],
        compiler_params=_params(2),
        name="band_attn",
    )(att, att, att, k_prev, v_prev, gv)


def _bias_rows(rel_bias_l, lc):
    wv = jnp.arange(BIAS_W)
    dist = ATT_REACH + lc - 1 - wv
    ridx = jnp.clip(dist, -(CHUNK - 1), REL_MAX) + (CHUNK - 1)
    return rel_bias_l.astype(F32)[:, ridx]


def _out_ffn_kernel(x_ref, oa_ref, ob_ref, oc_ref, sga_ref, sgb_ref, gnw_ref, glw_ref,
                    n2w_ref, fnw_ref, ones_ref, wout_ref, wup_ref, wdown_ref, y_ref,
                    *, final_norm):
    ones = ones_ref[...]

    def head_sum(t):
        hi = t.astype(BF16)
        lo = (t - hi.astype(F32)).astype(BF16)
        return _dot(hi, ones) + _dot(lo, ones)

    inv_d = 1.0 / HEAD_DIM
    oa = oa_ref[...]
    mu = head_sum(oa) * inv_d
    da = oa - mu
    var = head_sum(da * da) * inv_d
    a = da * lax.rsqrt(var + EPS) * gnw_ref[...] * sga_ref[...].astype(F32)
    ob = ob_ref[...]
    ms = head_sum(ob * ob) * inv_d
    bb = ob * lax.rsqrt(ms + EPS) * glw_ref[...] * sgb_ref[...].astype(F32)
    cat = jnp.concatenate([a.astype(BF16), bb.astype(BF16), oc_ref[...]], axis=1)
    h1 = x_ref[...] + _dot(cat, wout_ref[...])
    hn = _rmsnorm(h1, n2w_ref[...]).astype(BF16)
    acc = jnp.zeros_like(h1)
    n_slab = D_FF // D_MODEL
    for j in range(n_slab):
        sl = slice(j * D_MODEL, (j + 1) * D_MODEL)
        u = jnp.maximum(_dot(hn, wup_ref[:, sl]), 0.0)
        acc = acc + _dot((u * u).astype(BF16), wdown_ref[sl, :])
    h2 = h1 + acc
    if final_norm:
        h2 = _rmsnorm(h2, fnw_ref[...])
    y_ref[...] = h2


def _out_ffn(x2, oa, ob, oc, ret, gla, gnw, glw, n2w, fnw, ones, wout, wup, wdown,
             *, tm, final_norm):
    m = x2.shape[0]
    full = lambda a: pl.BlockSpec(a.shape, lambda i: (0,) * a.ndim)
    row = lambda w, c=0: pl.BlockSpec((tm, w), lambda i: (i, c))
    kern = functools.partial(_out_ffn_kernel, final_norm=final_norm)
    return pl.pallas_call(
        kern,
        grid=(m // tm,),
        in_specs=[row(D_MODEL), row(W_RET), row(W_GLA), row(W_ATT), row(W_RET, 3),
                  row(W_GLA, 3), full(gnw), full(glw), full(n2w), full(fnw), full(ones),
                  full(wout), full(wup), full(wdown)],
        out_specs=row(D_MODEL),
        out_shape=jax.ShapeDtypeStruct((m, D_MODEL), F32),
        compiler_params=_params(1),
        name="out_ffn",
    )(x2, oa, ob, oc, ret, gla, gnw, glw, n2w, fnw, ones, wout, wup, wdown)


def _rotary_tables(pos, rows):
    half = HEAD_DIM // 2
    inv_freq = ROPE_BASE ** (-jnp.arange(half, dtype=F32) / half)
    ang = pos.astype(F32)[:, None] * inv_freq[None, :]
    reps = (rows // pos.shape[0], LANES // half)
    return jnp.tile(jnp.cos(ang), reps), jnp.tile(jnp.sin(ang), reps)


def _block_diag_state(s):
    bsz, h, a, b = s.shape
    eye = jnp.eye(h, dtype=s.dtype)
    return (s[:, :, :, None, :] * eye[None, :, None, :, None]).reshape(bsz, h * a, h * b)


def _diag_blocks(s, h):
    bsz, ha, hb = s.shape
    s5 = s.reshape(bsz, h, ha // h, h, hb // h)
    return jnp.stack([s5[:, i, :, i, :] for i in range(h)], axis=1)


def _layer_weights(w_in_l, w_a2_l, b_a_l):
    n_main = 4 * W_RET + 4 * W_GLA
    wret = w_in_l[:, :4 * W_RET].astype(BF16)
    low = w_in_l[:, n_main:n_main + GLA_RANK]
    wgla = jnp.concatenate(
        [w_in_l[:, 4 * W_RET:n_main], low,
         jnp.zeros((D_MODEL, LOW_PAD - GLA_RANK), w_in_l.dtype)], axis=1).astype(BF16)
    watt = w_in_l[:, n_main + GLA_RANK:].astype(BF16)
    wa2 = jnp.concatenate(
        [w_a2_l, jnp.zeros((LOW_PAD - GLA_RANK, W_GLA), w_a2_l.dtype)], axis=0).astype(BF16)
    return wret, wgla, watt, wa2, b_a_l.astype(F32)[None, :]


def _run_path(x2, weights, *, batch, seq_len, lc, n_chunks, tm, pos, s_ret0, s_gla0,
              k_prev, v_prev, final_norm):
    (n1w, n2w, fnw, wret, wgla, watt, wa2, ba, gnw, glw, gv, wout, wup, wdown, ones) = weights
    cos_t, sin_t = _rotary_tables(pos, max(seq_len, tm))
    ret, gla, lf, att, kc32, vc32 = _in_proj(
        x2, n1w, wret, wgla, watt, wa2, ba, cos_t, sin_t, tm=tm, seq_len=seq_len)
    oa, s_ret = _retention(ret, s_ret0, _retention_tables(lc),
                           batch=batch, seq_len=seq_len, lc=lc, n_chunks=n_chunks)
    ob, st_gla = _gla(gla, lf, s_gla0, _gla_tables(lc),
                      batch=batch, seq_len=seq_len, lc=lc, n_chunks=n_chunks)
    rpi = min(lc * n_chunks, ATT_ROWS_PER_STEP)
    n_iter = lc * n_chunks // rpi
    if k_prev is None:
        oc = _band_attn(att, att, att, gv, batch=batch, seq_len=seq_len, lc=lc,
                        rpi=rpi, n_iter=n_iter, prev_from_att=True)
    else:
        oc = _band_attn(att, k_prev, v_prev, gv, batch=batch, seq_len=seq_len, lc=lc,
                        rpi=rpi, n_iter=n_iter, prev_from_att=False)
    y = _out_ffn(x2, oa, ob, oc, ret, gla, gnw, glw, n2w, fnw, ones, wout, wup, wdown,
                 tm=tm, final_norm=final_norm)
    return y, s_ret, st_gla, kc32, vc32


def kernel(x_prompt, x_sample, state_ret, state_gla, cache_attn_k, cache_attn_v, norm1_w, norm2_w, final_norm_w, w_in, w_gla_a2, b_gla_a, ret_gn_w, gla_norm_w, rel_bias, w_out, w_up, w_down):
    depth = w_in.shape[0]
    bp, lp, _ = x_prompt.shape
    bs, ls, _ = x_sample.shape
    keep = min(ATT_REACH, lp)
    head = jnp.arange(W_RET) // HEAD_DIM
    ones = (head[:, None] == head[None, :]).astype(BF16)
    hp = x_prompt.reshape(bp * lp, D_MODEL)
    hs = x_sample.reshape(bs * ls, D_MODEL)
    pos_p = jnp.arange(lp)
    pos_s = PAST_LEN + jnp.arange(ls)
    zeros_state = jnp.zeros((bp, W_RET, W_RET), F32)
    outs = {k: [] for k in ("ret_p", "gla_p", "kp", "vp", "ret_s", "gla_s", "ks", "vs")}
    for l in range(depth):
        last = l == depth - 1
        wret, wgla, watt, wa2, ba = _layer_weights(w_in[l], w_gla_a2[l], b_gla_a[l])
        row = lambda v: v.astype(F32)[None, :]
        common = (row(norm1_w[l]), row(norm2_w[l]), row(final_norm_w), wret, wgla, watt, wa2, ba,
                  row(ret_gn_w[l]), row(gla_norm_w[l]))
        tail = (w_out[l].astype(BF16), w_up[l].astype(BF16), w_down[l].astype(BF16), ones)
        wts = common + (_bias_rows(rel_bias[l], CHUNK),) + tail
        hp, s_ret, st_gla, kc32, vc32 = _run_path(
            hp, wts, batch=bp, seq_len=lp, lc=CHUNK, n_chunks=ATT_REACH // CHUNK, tm=512,
            pos=pos_p, s_ret0=zeros_state, s_gla0=zeros_state, k_prev=None, v_prev=None,
            final_norm=last)
        outs["ret_p"].append(_diag_blocks(s_ret, H_RET))
        outs["gla_p"].append(jnp.swapaxes(_diag_blocks(st_gla, H_GLA), -1, -2))
        outs["kp"].append(kc32.reshape(bp, keep, H_ATT, HEAD_DIM))
        outs["vp"].append(vc32.reshape(bp, keep, H_ATT, HEAD_DIM))
        wts = common + (_bias_rows(rel_bias[l], ls),) + tail
        s_ret0 = _block_diag_state(state_ret[l].astype(F32))
        s_gla0 = _block_diag_state(jnp.swapaxes(state_gla[l].astype(F32), -1, -2))
        k_prev = cache_attn_k[l].reshape(bs * ATT_REACH, W_ATT)
        v_prev = cache_attn_v[l].reshape(bs * ATT_REACH, W_ATT)
        hs, s_ret, st_gla, kc32, vc32 = _run_path(
            hs, wts, batch=bs, seq_len=ls, lc=ls, n_chunks=1, tm=bs * ls,
            pos=pos_s, s_ret0=s_ret0, s_gla0=s_gla0, k_prev=k_prev, v_prev=v_prev,
            final_norm=last)
        outs["ret_s"].append(_diag_blocks(s_ret, H_RET))
        outs["gla_s"].append(jnp.swapaxes(_diag_blocks(st_gla, H_GLA), -1, -2))
        outs["ks"].append(kc32.reshape(bs, ls, H_ATT, HEAD_DIM))
        outs["vs"].append(vc32.reshape(bs, ls, H_ATT, HEAD_DIM))
    y_prompt = hp.reshape(bp, lp, D_MODEL)
    y_sample = hs.reshape(bs, ls, D_MODEL)
    st = lambda k: jnp.stack(outs[k])
    return (y_prompt, y_sample, st("ret_p"), st("gla_p"), st("kp"), st("vp"),
            st("ret_s"), st("gla_s"), st("ks"), st("vs"))
```

```python
import functools

import numpy as np
import jax
import jax.numpy as jnp
from jax import lax
from jax.experimental import pallas as pl
from jax.experimental.pallas import tpu as pltpu

D_MODEL = 1024
CHUNK = 64
HEAD_DIM = 64
H_RET = 4
H_GLA = 4
H_ATT = 8
W_RET = H_RET * HEAD_DIM
W_GLA = H_GLA * HEAD_DIM
W_ATT = H_ATT * HEAD_DIM
GLA_RANK = 16
GLA_TAU = 16.0
ATT_REACH = 512
REL_MAX = 256
N_REL = CHUNK + REL_MAX
D_FF = 4 * D_MODEL
IN_COLS = 4 * W_RET + 4 * W_GLA + GLA_RANK + 3 * W_ATT
ROPE_BASE = 10000.0
EPS = 1e-6
PAST_LEN = 4096

LANES = 128
LOW_PAD = LANES
BIAS_W = 640
GLA_SAFE_LOG_DECAY = 50.0
ATT_ROWS_PER_STEP = 128
LOG2E = 1.4426950408889634
NEG_BIG = -1e30
VMEM_LIMIT = 56 * 1024 * 1024

F32 = jnp.float32
BF16 = jnp.bfloat16

_NT = (((1,), (1,)), ((), ()))
_TN = (((0,), (0,)), ((), ()))


def _dot(a, b):
    return jnp.dot(a, b, preferred_element_type=F32)


def _dot_nt(a, b):
    return lax.dot_general(a, b, _NT, preferred_element_type=F32)


def _dot_tn(a, b):
    return lax.dot_general(a, b, _TN, preferred_element_type=F32)


def _rmsnorm(x, w):
    return x * lax.rsqrt(jnp.mean(x * x, axis=-1, keepdims=True) + EPS) * w


def _params(n_grid_dims):
    return pltpu.CompilerParams(
        dimension_semantics=("arbitrary",) * n_grid_dims,
        vmem_limit_bytes=VMEM_LIMIT,
    )


def _layer_spec(a, layer):
    index = (layer,) + (0,) * (a.ndim - 1)
    return pl.BlockSpec((None,) + a.shape[1:], lambda *_: index, pipeline_mode=pl.Buffered(1))


def _block_diag_rows(x, n_heads):
    lane_head = lax.broadcasted_iota(jnp.int32, x.shape, 1) // HEAD_DIM
    zero = jnp.zeros_like(x)
    return jnp.concatenate(
        [jnp.where(lane_head == h, x, zero) for h in range(n_heads)], axis=0)


def _in_proj_kernel(x_ref, n1w_ref, win_ref, wa2in_ref, ba_ref, cos_ref, sin_ref,
                    ret_ref, gla_ref, lf_ref, att_ref, kc_ref, vc_ref,
                    wret_ref, wgla_ref, watt_ref, wa2_ref, *, tiles_per_seq):
    @pl.when(pl.program_id(0) == 0)
    def _():
        n_main = 4 * W_RET + 4 * W_GLA
        wret_ref[...] = win_ref[0:4 * W_RET, :].astype(BF16)
        wgla_ref[0:4 * W_GLA, :] = win_ref[4 * W_RET:n_main, :].astype(BF16)
        wgla_ref[4 * W_GLA:4 * W_GLA + GLA_RANK, :] = (
            win_ref[n_main:n_main + GLA_RANK, :].astype(BF16))
        wgla_ref[4 * W_GLA + GLA_RANK:, :] = jnp.zeros((LOW_PAD - GLA_RANK, D_MODEL), BF16)
        watt_ref[...] = win_ref[n_main + GLA_RANK:IN_COLS, :].astype(BF16)
        wa2_ref[0:GLA_RANK, :] = wa2in_ref[...].astype(BF16)
        wa2_ref[GLA_RANK:, :] = jnp.zeros((LOW_PAD - GLA_RANK, W_GLA), BF16)

    hn = _rmsnorm(x_ref[...], n1w_ref[...]).astype(BF16)
    cos = cos_ref[...]
    sin = sin_ref[...]
    first_half = (lax.broadcasted_iota(jnp.int32, cos.shape, 1) % HEAD_DIM) < (HEAD_DIM // 2)

    def rotary(t):
        swapped = jnp.where(first_half, -pltpu.roll(t, LANES - HEAD_DIM // 2, 1),
                            pltpu.roll(t, HEAD_DIM // 2, 1))
        return t * cos + swapped * sin

    scale = HEAD_DIM ** -0.5

    acc = _dot_nt(hn, wret_ref[...])
    for j in range(W_RET // LANES):
        sl = slice(j * LANES, (j + 1) * LANES)
        ret_ref[:, sl] = rotary(acc[:, sl]).astype(BF16)
        sk = slice(W_RET + j * LANES, W_RET + (j + 1) * LANES)
        ret_ref[:, sk] = (rotary(acc[:, sk]) * scale).astype(BF16)
    ret_ref[:, 2 * W_RET:3 * W_RET] = acc[:, 2 * W_RET:3 * W_RET].astype(BF16)
    g = acc[:, 3 * W_RET:4 * W_RET]
    ret_ref[:, 3 * W_RET:4 * W_RET] = (g * jax.nn.sigmoid(g)).astype(BF16)

    acc = _dot_nt(hn, wgla_ref[...])
    gla_ref[:, 0:W_GLA] = (acc[:, 0:W_GLA] * scale).astype(BF16)
    gla_ref[:, W_GLA:3 * W_GLA] = acc[:, W_GLA:3 * W_GLA].astype(BF16)
    g = acc[:, 3 * W_GLA:4 * W_GLA]
    gla_ref[:, 3 * W_GLA:4 * W_GLA] = (g * jax.nn.sigmoid(g)).astype(BF16)
    low = acc[:, 4 * W_GLA:4 * W_GLA + LOW_PAD].astype(BF16)
    z = _dot(low, wa2_ref[...]) + ba_ref[...]
    log_sig = jnp.minimum(z, 0.0) - jnp.log(1.0 + jnp.exp(-jnp.abs(z)))
    lf_ref[...] = log_sig * (1.0 / GLA_TAU)

    acc = _dot_nt(hn, watt_ref[...])
    att_ref[:, 0:W_ATT] = (acc[:, 0:W_ATT] * (scale * LOG2E)).astype(BF16)
    att_ref[:, W_ATT:3 * W_ATT] = acc[:, W_ATT:3 * W_ATT].astype(BF16)
    if tiles_per_seq == 0:
        kc_ref[...] = acc[:, W_ATT:2 * W_ATT]
        vc_ref[...] = acc[:, 2 * W_ATT:3 * W_ATT]
    else:
        @pl.when(pl.program_id(0) % tiles_per_seq == tiles_per_seq - 1)
        def _():
            kc_ref[...] = acc[:, W_ATT:2 * W_ATT].T
            vc_ref[...] = acc[:, 2 * W_ATT:3 * W_ATT].T


def _in_proj(x2, n1w, w_in, w_a2, ba, cos_t, sin_t, *, layer, tm, seq_len):
    m = x2.shape[0]
    n_tiles = m // tm
    tab_blocks = cos_t.shape[0] // tm
    tiles_per_seq = seq_len // tm
    at_layer = functools.partial(_layer_spec, layer=layer)
    row = lambda w: pl.BlockSpec((tm, w), lambda i: (i, 0))
    tab = pl.BlockSpec((tm, LANES), lambda i: (i % tab_blocks, 0))
    if tiles_per_seq == 0:
        cache = row(W_ATT)
        cache_shape = jax.ShapeDtypeStruct((m, W_ATT), F32)
    else:
        cache = pl.BlockSpec((None, W_ATT, tm), lambda i: (i // tiles_per_seq, 0, 0))
        cache_shape = jax.ShapeDtypeStruct((m // seq_len, W_ATT, tm), F32)
    kern = functools.partial(_in_proj_kernel, tiles_per_seq=tiles_per_seq)
    return pl.pallas_call(
        kern,
        grid=(n_tiles,),
        in_specs=[row(D_MODEL), at_layer(n1w), at_layer(w_in), at_layer(w_a2), at_layer(ba),
                  tab, tab],
        out_specs=[row(4 * W_RET), row(4 * W_GLA), row(W_GLA), row(3 * W_ATT), cache, cache],
        out_shape=[
            jax.ShapeDtypeStruct((m, 4 * W_RET), BF16),
            jax.ShapeDtypeStruct((m, 4 * W_GLA), BF16),
            jax.ShapeDtypeStruct((m, W_GLA), F32),
            jax.ShapeDtypeStruct((m, 3 * W_ATT), BF16),
            cache_shape,
            cache_shape,
        ],
        scratch_shapes=[pltpu.VMEM((4 * W_RET, D_MODEL), BF16),
                        pltpu.VMEM((4 * W_GLA + LOW_PAD, D_MODEL), BF16),
                        pltpu.VMEM((3 * W_ATT, D_MODEL), BF16),
                        pltpu.VMEM((LOW_PAD, W_GLA), BF16)],
        compiler_params=_params(1),
        name="in_proj",
    )(x2, n1w, w_in, w_a2, ba, cos_t, sin_t)


def _load_block_diag(s0_ref, n_heads):
    zero = jnp.zeros((HEAD_DIM, HEAD_DIM), F32)
    return jnp.concatenate(
        [jnp.concatenate([s0_ref[i] if j == i else zero for j in range(n_heads)], axis=1)
         for i in range(n_heads)], axis=0)


def _store_diag_blocks(s, out_ref, n_heads):
    for i in range(n_heads):
        sl = slice(i * HEAD_DIM, (i + 1) * HEAD_DIM)
        out_ref[i] = s[sl, sl]


def _retention_kernel(*refs, lc, n_chunks, has_init):
    if has_init:
        (q_ref, k_ref, v_ref, s0_ref, dtab_ref, qdec_ref, kdec_ref, gtab_ref, bd_ref,
         o_ref, sfin_ref, s_scr) = refs
    else:
        (q_ref, k_ref, v_ref, dtab_ref, qdec_ref, kdec_ref, gtab_ref, bd_ref,
         o_ref, sfin_ref, s_scr) = refs
    g = pl.program_id(1)

    @pl.when(g == 0)
    def _():
        if has_init:
            s_scr[...] = _load_block_diag(s0_ref, H_RET)
        else:
            s_scr[...] = jnp.zeros_like(s_scr)

    dtab = dtab_ref[...]
    qdec = qdec_ref[...]
    kdec = kdec_ref[...]
    for c in range(n_chunks):
        rows = slice(c * lc, (c + 1) * lc)
        q = q_ref[rows, :]
        k = k_ref[rows, :]
        v = v_ref[rows, :]
        scores = _dot_nt(q, _block_diag_rows(k, H_RET))
        p = (scores * dtab).astype(BF16)
        intra = _dot(p, _block_diag_rows(v, H_RET))
        s_old = s_scr[...]
        inter = _dot(q, s_old.astype(BF16)) * qdec
        kd = (k.astype(F32) * kdec).astype(BF16)
        upd = _dot_tn(kd, v)
        s_scr[...] = s_old * gtab_ref[...] + upd * bd_ref[...]
        o_ref[rows, :] = intra + inter

    @pl.when(g == pl.num_programs(1) - 1)
    def _():
        _store_diag_blocks(s_scr[...], sfin_ref, H_RET)


def _state_specs(states, layer, n_heads):
    blk = (n_heads, HEAD_DIM, HEAD_DIM)
    out_spec = pl.BlockSpec((None,) + blk, lambda b, g: (b, 0, 0, 0))
    if states is None:
        return [], [], out_spec
    in_spec = pl.BlockSpec((None, None) + blk, lambda b, g: (layer, b, 0, 0, 0))
    return [in_spec], [states], out_spec


def _retention(ret, states, layer, tabs, *, batch, seq_len, lc, n_chunks):
    rows = lc * n_chunks
    ng = seq_len // rows
    m = batch * seq_len
    col = lambda c: pl.BlockSpec((rows, W_RET), lambda b, g: (b * ng + g, c))
    full = lambda a: pl.BlockSpec(a.shape, lambda b, g: (0,) * a.ndim)
    s_in_spec, s_in, s_out_spec = _state_specs(states, layer, H_RET)
    kern = functools.partial(_retention_kernel, lc=lc, n_chunks=n_chunks,
                             has_init=states is not None)
    return pl.pallas_call(
        kern,
        grid=(batch, ng),
        in_specs=[col(0), col(1), col(2)] + s_in_spec + [full(t) for t in tabs],
        out_specs=[pl.BlockSpec((rows, W_RET), lambda b, g: (b * ng + g, 0)), s_out_spec],
        out_shape=[jax.ShapeDtypeStruct((m, W_RET), F32),
                   jax.ShapeDtypeStruct((batch, H_RET, HEAD_DIM, HEAD_DIM), F32)],
        scratch_shapes=[pltpu.VMEM((W_RET, W_RET), F32)],
        compiler_params=_params(2),
        name="retention",
    )(ret, ret, ret, *s_in, *tabs)


def _retention_tables(lc):
    lg = np.log1p(-np.exp2(-5.0 - np.arange(H_RET, dtype=np.float64)))
    idx = np.arange(lc)
    diff = idx[:, None] - idx[None, :]
    causal = diff >= 0
    dpos = np.where(causal, diff, 0).astype(np.float64)
    decay = np.where(causal[None], np.exp(dpos[None] * lg[:, None, None]), 0.0)
    dtab = np.transpose(decay, (1, 0, 2)).reshape(lc, H_RET * lc)
    q_decay = np.exp((idx + 1).astype(np.float64)[:, None] * lg[None, :])
    k_decay = np.exp((lc - 1 - idx).astype(np.float64)[:, None] * lg[None, :])
    qdec = np.repeat(q_decay, HEAD_DIM, axis=1)
    kdec = np.repeat(k_decay, HEAD_DIM, axis=1)
    gtab = np.broadcast_to(np.repeat(np.exp(lc * lg), HEAD_DIM)[:, None], (W_RET, W_RET))
    head = np.arange(W_RET) // HEAD_DIM
    bd = head[:, None] == head[None, :]
    return tuple(jnp.asarray(t, F32) for t in (dtab, qdec, kdec, gtab, bd))


def _split_bf16(x):
    hi = x.astype(BF16)
    lo = (x - hi.astype(F32)).astype(BF16)
    return jnp.concatenate([hi, lo], axis=1)


def _group_reference(b, row, half):
    n, w = b.shape
    group = 2 * half
    if group >= 8:
        b3 = b.reshape(n // group, group, w)
        return jnp.broadcast_to(b3[:, half - 1:half, :], b3.shape).reshape(n, w)
    if group == 4:
        i = row & 3
        return jnp.where(i == 0, pltpu.roll(b, n - 1, 0),
                         jnp.where(i == 1, b,
                                   jnp.where(i == 2, pltpu.roll(b, 1, 0), pltpu.roll(b, 2, 0))))
    return jnp.where((row & 1) == 1, pltpu.roll(b, 1, 0), b)


def _gla_kernel(*refs, lc, n_chunks, has_init):
    if has_init:
        (q_ref, k_ref, v_ref, lf_ref, s0_ref, tri_ref, bd_ref, onehot_ref,
         o_ref, sfin_ref, s_scr, sc_scr) = refs
    else:
        (q_ref, k_ref, v_ref, lf_ref, tri_ref, bd_ref, onehot_ref,
         o_ref, sfin_ref, s_scr, sc_scr) = refs
    g = pl.program_id(1)

    @pl.when(g == 0)
    def _():
        if has_init:
            s_scr[...] = _load_block_diag(s0_ref, H_GLA)
        else:
            s_scr[...] = jnp.zeros_like(s_scr)

    n_lev = lc.bit_length() - 1
    w = W_GLA
    n = lc * n_chunks
    chunk_rows = [slice(c * lc, (c + 1) * lc) for c in range(n_chunks)]
    row = lax.broadcasted_iota(jnp.int32, (n, w), 0)
    trow = lax.broadcasted_iota(jnp.int32, (lc, H_GLA * lc), 0)
    scol = lax.broadcasted_iota(jnp.int32, (lc, H_GLA * lc), 1) & (lc - 1)
    q = q_ref[...]
    k = k_ref[...]
    v = v_ref[...]
    q32 = q.astype(F32)
    k32 = k.astype(F32)
    tri = tri_ref[...]
    lfs = _split_bf16(lf_ref[...])
    cs = jnp.concatenate([_dot(tri, lfs[r, :]) for r in chunk_rows], axis=0)
    b = cs[:, :w] + cs[:, w:]

    qe = (q32 * jnp.exp(b)).astype(BF16)
    mild = jnp.min(b) >= -GLA_SAFE_LOG_DECAY

    @pl.when(mild)
    def _():
        ke = (k32 * jnp.exp(-b)).astype(BF16)
        for r in chunk_rows:
            sc = _dot_nt(qe[r, :], _block_diag_rows(ke[r, :], H_GLA))
            sc_scr[r, :] = jnp.where(trow >= scol, sc, 0.0)

    @pl.when(jnp.logical_not(mild))
    def _():
        for r in chunk_rows:
            sc = _dot_nt(q[r, :], _block_diag_rows(k[r, :], H_GLA))
            sc_scr[r, :] = jnp.where(trow == scol, sc, 0.0)
        for lev in range(n_lev):
            half = lc >> (lev + 1)
            shift = half.bit_length()
            upper = (row & half) != 0
            e = jnp.exp(-jnp.abs(b - _group_reference(b, row, half)))
            z = (jnp.where(upper, q32, k32) * e).astype(BF16)
            valid = (((trow >> shift) == (scol >> shift))
                     & ((trow & half) != 0) & ((scol & half) == 0))
            for r in chunk_rows:
                zc = z[r, :]
                sc = _dot_nt(zc, _block_diag_rows(zc, H_GLA))
                sc_scr[r, :] = jnp.where(valid, sc, sc_scr[r, :])

    b3 = b.reshape(n_chunks, lc, w)
    b_last = b3[:, lc - 1:lc, :]
    kd = (k32 * jnp.exp(jnp.broadcast_to(b_last, b3.shape).reshape(n, w) - b)).astype(BF16)
    col_sum = _dot_tn(lfs, onehot_ref[...])
    decay_col = jnp.exp(col_sum[:w, :] + col_sum[w:, :])
    bd = bd_ref[...]
    for c, r in enumerate(chunk_rows):
        s_old = s_scr[...]
        inter = _dot(qe[r, :], s_old.astype(BF16))
        intra = _dot(sc_scr[r, :].astype(BF16), _block_diag_rows(v[r, :], H_GLA))
        upd = _dot_tn(kd[r, :], v[r, :])
        s_scr[...] = s_old * decay_col[:, c:c + 1] + upd * bd
        o_ref[r, :] = intra + inter

    @pl.when(g == pl.num_programs(1) - 1)
    def _():
        _store_diag_blocks(s_scr[...], sfin_ref, H_GLA)


def _gla(gla, lf, states, layer, tabs, *, batch, seq_len, lc, n_chunks):
    rows = lc * n_chunks
    ng = seq_len // rows
    m = batch * seq_len
    col = lambda c: pl.BlockSpec((rows, W_GLA), lambda b, g: (b * ng + g, c))
    full = lambda a: pl.BlockSpec(a.shape, lambda b, g: (0,) * a.ndim)
    s_in_spec, s_in, s_out_spec = _state_specs(states, layer, H_GLA)
    kern = functools.partial(_gla_kernel, lc=lc, n_chunks=n_chunks, has_init=states is not None)
    return pl.pallas_call(
        kern,
        grid=(batch, ng),
        in_specs=[col(0), col(1), col(2), col(0)] + s_in_spec + [full(t) for t in tabs],
        out_specs=[pl.BlockSpec((rows, W_GLA), lambda b, g: (b * ng + g, 0)), s_out_spec],
        out_shape=[jax.ShapeDtypeStruct((m, W_GLA), F32),
                   jax.ShapeDtypeStruct((batch, H_GLA, HEAD_DIM, HEAD_DIM), F32)],
        scratch_shapes=[pltpu.VMEM((W_GLA, W_GLA), F32),
                        pltpu.VMEM((rows, H_GLA * lc), F32)],
        compiler_params=_params(2),
        name="gla",
    )(gla, gla, gla, lf, *s_in, *tabs)


def _gla_tables(lc, n_chunks):
    idx = np.arange(lc)
    tri = idx[:, None] >= idx[None, :]
    head = np.arange(W_GLA) // HEAD_DIM
    bd = head[:, None] == head[None, :]
    chunk_of_row = np.arange(lc * n_chunks) // lc
    onehot = chunk_of_row[:, None] == np.arange(LANES)[None, :]
    return jnp.asarray(tri, BF16), jnp.asarray(bd, F32), jnp.asarray(onehot, BF16)


def _band_attn_kernel(q_ref, kc_ref, vc_ref, kp_ref, vp_ref, gv_ref, o_ref,
                      kwin, vwin, bias_scr, s_scr, m_scr, *, lc, rpi, n_iter):
    b_id = pl.program_id(0)
    g = pl.program_id(1)
    win = ATT_REACH + rpi
    rows = rpi * n_iter

    @pl.when((b_id == 0) & (g == 0))
    def _():
        row = lax.broadcasted_iota(jnp.int32, (rpi, BIAS_W), 0)
        rel = lax.broadcasted_iota(jnp.int32, (rpi, BIAS_W), 1) - (row // lc) * lc
        in_band = (rel >= 0) & (rel < ATT_REACH + lc)
        for h in range(H_ATT):
            x = jnp.broadcast_to(gv_ref[h:h + 1, :], (rpi, BIAS_W))
            x = pltpu.roll(x, BIAS_W - (lc - 1), 1, stride=1, stride_axis=0)
            bias_scr[h] = jnp.where(in_band, x * LOG2E, NEG_BIG)

    kwin[0:ATT_REACH, :] = kp_ref[...]
    vwin[0:ATT_REACH, :] = vp_ref[...]
    kwin[ATT_REACH:ATT_REACH + rows, :] = kc_ref[...]
    vwin[ATT_REACH:ATT_REACH + rows, :] = vc_ref[...]

    lane = lax.broadcasted_iota(jnp.int32, (rpi, LANES), 1)
    col = lax.broadcasted_iota(jnp.int32, (rpi, win), 1)

    def step(i, carry, *, masked):
        r0 = pl.multiple_of(i * rpi, rpi)
        first_valid = ATT_REACH - (g * rows + r0)
        for pair in range(H_ATT // 2):
            lanes = slice(pair * LANES, (pair + 1) * LANES)
            qp = q_ref[pl.ds(r0, rpi), lanes]
            kp = kwin[pl.ds(r0, win), lanes]
            for hh in range(2):
                h = 2 * pair + hh
                qh = jnp.where((lane // HEAD_DIM) == hh, qp, jnp.zeros_like(qp))
                s = _dot_nt(qh, kp) + bias_scr[h][:, :win]
                if masked:
                    s = jnp.where(col >= first_valid, s, NEG_BIG)
                s_scr[h, :, :win] = s
                m_scr[h] = jnp.broadcast_to(jnp.max(s, axis=-1, keepdims=True), (rpi, LANES))
        for pair in range(H_ATT // 2):
            lanes = slice(pair * LANES, (pair + 1) * LANES)
            vp = vwin[pl.ds(r0, win), lanes]
            outs = []
            for hh in range(2):
                h = 2 * pair + hh
                e = jnp.exp2(s_scr[h, :, :win] - m_scr[h][:, :1])
                den = jnp.sum(e, axis=-1, keepdims=True)
                outs.append(_dot(e.astype(BF16), vp) / den)
            o_ref[pl.ds(r0, rpi), lanes] = jnp.where(
                lane < HEAD_DIM, outs[0], outs[1]).astype(BF16)
        return carry

    @pl.when(g == 0)
    def _():
        lax.fori_loop(0, n_iter, functools.partial(step, masked=True), 0)

    @pl.when(g != 0)
    def _():
        lax.fori_loop(0, n_iter, functools.partial(step, masked=False), 0)


def _band_attn(att, gv, layer, *, batch, seq_len, lc, rpi, n_iter):
    rows = rpi * n_iter
    ng = seq_len // rows
    m = batch * seq_len
    cur = lambda c: pl.BlockSpec((rows, W_ATT), lambda b, g: (b * ng + g, c))
    prev = lambda c: pl.BlockSpec(
        (ATT_REACH, W_ATT), lambda b, g: (b * ng + jnp.maximum(g - 1, 0), c))
    kern = functools.partial(_band_attn_kernel, lc=lc, rpi=rpi, n_iter=n_iter)
    return pl.pallas_call(
        kern,
        grid=(batch, ng),
        in_specs=[cur(0), cur(1), cur(2), prev(1), prev(2),
                  _layer_spec(gv, layer)],
        out_specs=pl.BlockSpec((rows, W_ATT), lambda b, g: (b * ng + g, 0)),
        out_shape=jax.ShapeDtypeStruct((m, W_ATT), BF16),
        scratch_shapes=[pltpu.VMEM((ATT_REACH + rows, W_ATT), BF16),
                        pltpu.VMEM((ATT_REACH + rows, W_ATT), BF16),
                        pltpu.VMEM((H_ATT, rpi, BIAS_W), F32),
                        pltpu.VMEM((H_ATT, rpi, BIAS_W), F32),
                        pltpu.VMEM((H_ATT, rpi, LANES), F32)],
        compiler_params=_params(2),
        name="band_attn",
    )(att, att, att, att, att, gv)


def _sample_attn_kernel(q_ref, kn_ref, vn_ref, kt_ref, vt_ref, gv_ref, o_ref, *, ls):
    win = ATT_REACH + ls
    lane = lax.broadcasted_iota(jnp.int32, (ls, LANES), 1)
    for pair in range(H_ATT // 2):
        lanes = slice(pair * LANES, (pair + 1) * LANES)
        qp = q_ref[:, lanes]
        knp = kn_ref[:, lanes]
        vnp = vn_ref[:, lanes]
        kt = kt_ref[2 * pair:2 * pair + 2].reshape(2 * HEAD_DIM, ATT_REACH).astype(BF16)
        vt = vt_ref[2 * pair:2 * pair + 2].reshape(2 * HEAD_DIM, ATT_REACH).astype(BF16)
        outs = []
        for hh in range(2):
            h = 2 * pair + hh
            x = jnp.broadcast_to(gv_ref[h:h + 1, :], (ls, BIAS_W))
            bias = pltpu.roll(x, BIAS_W - (ls - 1), 1, stride=1, stride_axis=0) * LOG2E
            qh = jnp.where((lane // HEAD_DIM) == hh, qp, jnp.zeros_like(qp))
            s_old = _dot(qh, kt) + bias[:, :ATT_REACH]
            s_new = _dot_nt(qh, knp) + bias[:, ATT_REACH:win]
            mx = jnp.maximum(jnp.max(s_old, axis=-1, keepdims=True),
                             jnp.max(s_new, axis=-1, keepdims=True))
            e_old = jnp.exp2(s_old - mx)
            e_new = jnp.exp2(s_new - mx)
            den = (jnp.sum(e_old, axis=-1, keepdims=True)
                   + jnp.sum(e_new, axis=-1, keepdims=True))
            pv = _dot_nt(e_old.astype(BF16), vt) + _dot(e_new.astype(BF16), vnp)
            outs.append(pv / den)
        o_ref[:, lanes] = jnp.where(lane < HEAD_DIM, outs[0], outs[1]).astype(BF16)


def _sample_attn(att, cache_kt, cache_vt, layer, gv, *, batch, ls):
    new = lambda c: pl.BlockSpec((ls, W_ATT), lambda b: (b, c))
    old = pl.BlockSpec((None, None, H_ATT, HEAD_DIM, ATT_REACH), lambda b: (layer, b, 0, 0, 0))
    kern = functools.partial(_sample_attn_kernel, ls=ls)
    return pl.pallas_call(
        kern,
        grid=(batch,),
        in_specs=[new(0), new(1), new(2), old, old, _layer_spec(gv, layer)],
        out_specs=pl.BlockSpec((ls, W_ATT), lambda b: (b, 0)),
        out_shape=jax.ShapeDtypeStruct((batch * ls, W_ATT), BF16),
        compiler_params=_params(1),
        name="sample_attn",
    )(att, att, att, cache_kt, cache_vt, gv)


def _bias_rows(rel_bias_l, lc):
    wv = np.arange(BIAS_W)
    dist = ATT_REACH + lc - 1 - wv
    ridx = np.clip(dist, -(CHUNK - 1), REL_MAX) + (CHUNK - 1)
    return rel_bias_l[..., ridx]


def _out_ffn_kernel(x_ref, oa_ref, ob_ref, oc_ref, sga_ref, sgb_ref, gnw_ref, glw_ref,
                    n2w_ref, fnw_ref, ones_ref, wout_ref, wup_ref, wdown_ref, y_ref,
                    *, final_norm):
    ones = ones_ref[...]

    def head_sum(t):
        hi = t.astype(BF16)
        lo = (t - hi.astype(F32)).astype(BF16)
        return _dot(hi, ones) + _dot(lo, ones)

    inv_d = 1.0 / HEAD_DIM
    oa = oa_ref[...]
    mu = head_sum(oa) * inv_d
    da = oa - mu
    var = head_sum(da * da) * inv_d
    a = da * lax.rsqrt(var + EPS) * gnw_ref[...] * sga_ref[...].astype(F32)
    ob = ob_ref[...]
    ms = head_sum(ob * ob) * inv_d
    bb = ob * lax.rsqrt(ms + EPS) * glw_ref[...] * sgb_ref[...].astype(F32)
    cat = jnp.concatenate([a.astype(BF16), bb.astype(BF16), oc_ref[...]], axis=1)
    h1 = x_ref[...] + _dot(cat, wout_ref[...])
    hn = _rmsnorm(h1, n2w_ref[...]).astype(BF16)
    acc = jnp.zeros_like(h1)
    n_slab = D_FF // D_MODEL
    for j in range(n_slab):
        sl = slice(j * D_MODEL, (j + 1) * D_MODEL)
        u = jnp.maximum(_dot(hn, wup_ref[:, sl]), 0.0)
        acc = acc + _dot((u * u).astype(BF16), wdown_ref[sl, :])
    h2 = h1 + acc
    if final_norm:
        h2 = _rmsnorm(h2, fnw_ref[...])
    y_ref[...] = h2


def _out_ffn(x2, oa, ob, oc, ret, gla, gnw, glw, n2w, fnw, ones, wout, wup, wdown,
             *, layer, tm, final_norm):
    m = x2.shape[0]
    full = lambda a: pl.BlockSpec(a.shape, lambda i: (0,) * a.ndim)
    at_layer = functools.partial(_layer_spec, layer=layer)
    row = lambda w, c=0: pl.BlockSpec((tm, w), lambda i: (i, c))
    kern = functools.partial(_out_ffn_kernel, final_norm=final_norm)
    return pl.pallas_call(
        kern,
        grid=(m // tm,),
        in_specs=[row(D_MODEL), row(W_RET), row(W_GLA), row(W_ATT), row(W_RET, 3),
                  row(W_GLA, 3), at_layer(gnw), at_layer(glw), at_layer(n2w), full(fnw),
                  full(ones), at_layer(wout), at_layer(wup), at_layer(wdown)],
        out_specs=row(D_MODEL),
        out_shape=jax.ShapeDtypeStruct((m, D_MODEL), F32),
        compiler_params=_params(1),
        name="out_ffn",
    )(x2, oa, ob, oc, ret, gla, gnw, glw, n2w, fnw, ones, wout, wup, wdown)


def _rotary_tables(pos, rows):
    half = HEAD_DIM // 2
    inv_freq = ROPE_BASE ** (-np.arange(half, dtype=np.float64) / half)
    ang = np.asarray(pos, np.float64)[:, None] * inv_freq[None, :]
    reps = (rows // len(pos), LANES // half)
    return (jnp.asarray(np.tile(np.cos(ang), reps), F32),
            jnp.asarray(np.tile(np.sin(ang), reps), F32))


def _run_path(x2, p, *, batch, seq_len, lc, n_chunks, tm, pos, layer, states, caches,
              final_norm):
    cos_t, sin_t = _rotary_tables(pos, max(seq_len, tm))
    ret, gla, lf, att, kc32, vc32 = _in_proj(
        x2, p["n1w"], p["w_in"], p["w_a2"], p["ba"], cos_t, sin_t,
        layer=layer, tm=tm, seq_len=seq_len)
    s_ret0, s_gla0 = states if states is not None else (None, None)
    oa, s_ret = _retention(ret, s_ret0, layer, _retention_tables(lc),
                           batch=batch, seq_len=seq_len, lc=lc, n_chunks=n_chunks)
    ob, s_gla = _gla(gla, lf, s_gla0, layer, _gla_tables(lc, n_chunks),
                     batch=batch, seq_len=seq_len, lc=lc, n_chunks=n_chunks)
    gv = _bias_rows(p["rel_bias"], lc)
    if caches is None:
        rpi = min(lc * n_chunks, ATT_ROWS_PER_STEP)
        oc = _band_attn(att, gv, layer, batch=batch, seq_len=seq_len, lc=lc, rpi=rpi,
                        n_iter=lc * n_chunks // rpi)
    else:
        oc = _sample_attn(att, caches[0], caches[1], layer, gv, batch=batch, ls=seq_len)
    y = _out_ffn(x2, oa, ob, oc, ret, gla, p["gnw"], p["glw"], p["n2w"], p["fnw"], p["ones"],
                 p["wout"], p["wup"], p["wdown"], layer=layer, tm=tm, final_norm=final_norm)
    return y, s_ret, s_gla, kc32, vc32


def kernel(x_prompt, x_sample, state_ret, state_gla, cache_attn_k, cache_attn_v, norm1_w, norm2_w, final_norm_w, w_in, w_gla_a2, b_gla_a, ret_gn_w, gla_norm_w, rel_bias, w_out, w_up, w_down):
    depth = w_in.shape[0]
    bp, lp, _ = x_prompt.shape
    bs, ls, _ = x_sample.shape
    keep = min(ATT_REACH, lp)
    head = np.arange(W_RET) // HEAD_DIM
    ones = jnp.asarray(head[:, None] == head[None, :], BF16)
    hp = x_prompt.reshape(bp * lp, D_MODEL)
    hs = x_sample.reshape(bs * ls, D_MODEL)
    pos_p = np.arange(lp)
    pos_s = PAST_LEN + np.arange(ls)
    tm_p = keep
    to_head_major = lambda c: jnp.transpose(c, (0, 1, 3, 4, 2))
    caches = (to_head_major(cache_attn_k.astype(F32)), to_head_major(cache_attn_v.astype(F32)))
    from_head_major = lambda t: jnp.transpose(
        t.reshape(bp, H_ATT, HEAD_DIM, keep), (0, 3, 1, 2))
    states = (state_ret.astype(F32), state_gla.astype(F32))
    outs = {k: [] for k in ("ret_p", "gla_p", "kp", "vp", "ret_s", "gla_s", "ks", "vs")}
    rows3 = lambda v: v.astype(F32).reshape(depth, 1, v.shape[-1])
    params = dict(
        n1w=rows3(norm1_w), n2w=rows3(norm2_w), fnw=final_norm_w.astype(F32)[None, :],
        gnw=rows3(ret_gn_w), glw=rows3(gla_norm_w), ba=rows3(b_gla_a),
        w_in=jnp.swapaxes(w_in.astype(F32), 1, 2), w_a2=w_gla_a2.astype(F32),
        rel_bias=rel_bias.astype(F32),
        wout=w_out.astype(BF16), wup=w_up.astype(BF16), wdown=w_down.astype(BF16), ones=ones)
    for l in range(depth):
        last = l == depth - 1
        hp, s_ret, s_gla, kt32, vt32 = _run_path(
            hp, params, batch=bp, seq_len=lp, lc=CHUNK, n_chunks=ATT_REACH // CHUNK, tm=tm_p,
            pos=pos_p, layer=l, states=None, caches=None, final_norm=last)
        outs["ret_p"].append(s_ret)
        outs["gla_p"].append(s_gla)
        outs["kp"].append(from_head_major(kt32))
        outs["vp"].append(from_head_major(vt32))
        hs, s_ret, s_gla, kc32, vc32 = _run_path(
            hs, params, batch=bs, seq_len=ls, lc=ls, n_chunks=1, tm=bs * ls,
            pos=pos_s, layer=l, states=states, caches=caches, final_norm=last)
        outs["ret_s"].append(s_ret)
        outs["gla_s"].append(s_gla)
        outs["ks"].append(kc32.reshape(bs, ls, H_ATT, HEAD_DIM))
        outs["vs"].append(vc32.reshape(bs, ls, H_ATT, HEAD_DIM))
    y_prompt = hp.reshape(bp, lp, D_MODEL)
    y_sample = hs.reshape(bs, ls, D_MODEL)
    st = lambda k: jnp.stack(outs[k])
    return (y_prompt, y_sample, st("ret_p"), st("gla_p"), st("kp"), st("vp"),
            st("ret_s"), st("gla_s"), st("ks"), st("vs"))
```

```python
import functools

import numpy as np
import jax
import jax.numpy as jnp
from jax import lax
from jax.experimental import pallas as pl
from jax.experimental.pallas import tpu as pltpu

D_MODEL = 1024
CHUNK = 64
HEAD_DIM = 64
H_RET = 4
H_GLA = 4
H_ATT = 8
W_RET = H_RET * HEAD_DIM
W_GLA = H_GLA * HEAD_DIM
W_ATT = H_ATT * HEAD_DIM
GLA_RANK = 16
GLA_TAU = 16.0
ATT_REACH = 512
REL_MAX = 256
N_REL = CHUNK + REL_MAX
D_FF = 4 * D_MODEL
IN_COLS = 4 * W_RET + 4 * W_GLA + GLA_RANK + 3 * W_ATT
ROPE_BASE = 10000.0
EPS = 1e-6
PAST_LEN = 4096

LANES = 128
LOW_PAD = LANES
BIAS_W = 640
GLA_SAFE_LOG_DECAY = 50.0
ATT_ROWS_PER_STEP = 128
LOG2E = 1.4426950408889634
NEG_BIG = -1e30
VMEM_LIMIT = 56 * 1024 * 1024

F32 = jnp.float32
BF16 = jnp.bfloat16

_NT = (((1,), (1,)), ((), ()))
_TN = (((0,), (0,)), ((), ()))


def _dot(a, b):
    return jnp.dot(a, b, preferred_element_type=F32)


def _dot_nt(a, b):
    return lax.dot_general(a, b, _NT, preferred_element_type=F32)


def _dot_tn(a, b):
    return lax.dot_general(a, b, _TN, preferred_element_type=F32)


def _rmsnorm(x, w):
    return x * lax.rsqrt(jnp.mean(x * x, axis=-1, keepdims=True) + EPS) * w


def _params(n_grid_dims):
    return pltpu.CompilerParams(
        dimension_semantics=("arbitrary",) * n_grid_dims,
        vmem_limit_bytes=VMEM_LIMIT,
    )


def _layer_spec(a, layer):
    index = (layer,) + (0,) * (a.ndim - 1)
    return pl.BlockSpec((None,) + a.shape[1:], lambda *_: index, pipeline_mode=pl.Buffered(1))


def _block_diag_rows(x, n_heads):
    lane_head = lax.broadcasted_iota(jnp.int32, x.shape, 1) // HEAD_DIM
    zero = jnp.zeros_like(x)
    return jnp.concatenate(
        [jnp.where(lane_head == h, x, zero) for h in range(n_heads)], axis=0)


def _in_proj_kernel(x_ref, n1w_ref, win_ref, wa2in_ref, ba_ref, cos_ref, sin_ref,
                    ret_ref, gla_ref, lf_ref, att_ref, kc_ref, vc_ref,
                    wret_ref, wgla_ref, watt_ref, wa2_ref, *, tiles_per_seq):
    @pl.when(pl.program_id(0) == 0)
    def _():
        n_main = 4 * W_RET + 4 * W_GLA
        wret_ref[...] = win_ref[0:4 * W_RET, :].astype(BF16)
        wgla_ref[0:4 * W_GLA, :] = win_ref[4 * W_RET:n_main, :].astype(BF16)
        wgla_ref[4 * W_GLA:4 * W_GLA + GLA_RANK, :] = (
            win_ref[n_main:n_main + GLA_RANK, :].astype(BF16))
        wgla_ref[4 * W_GLA + GLA_RANK:, :] = jnp.zeros((LOW_PAD - GLA_RANK, D_MODEL), BF16)
        watt_ref[...] = win_ref[n_main + GLA_RANK:IN_COLS, :].astype(BF16)
        wa2_ref[0:GLA_RANK, :] = wa2in_ref[...].astype(BF16)
        wa2_ref[GLA_RANK:, :] = jnp.zeros((LOW_PAD - GLA_RANK, W_GLA), BF16)

    hn = _rmsnorm(x_ref[...], n1w_ref[...]).astype(BF16)
    cos = cos_ref[...]
    sin = sin_ref[...]
    first_half = (lax.broadcasted_iota(jnp.int32, cos.shape, 1) % HEAD_DIM) < (HEAD_DIM // 2)

    def rotary(t):
        swapped = jnp.where(first_half, -pltpu.roll(t, LANES - HEAD_DIM // 2, 1),
                            pltpu.roll(t, HEAD_DIM // 2, 1))
        return t * cos + swapped * sin

    scale = HEAD_DIM ** -0.5

    acc = _dot_nt(hn, wret_ref[...])
    for j in range(W_RET // LANES):
        sl = slice(j * LANES, (j + 1) * LANES)
        ret_ref[:, sl] = rotary(acc[:, sl]).astype(BF16)
        sk = slice(W_RET + j * LANES, W_RET + (j + 1) * LANES)
        ret_ref[:, sk] = (rotary(acc[:, sk]) * scale).astype(BF16)
    ret_ref[:, 2 * W_RET:3 * W_RET] = acc[:, 2 * W_RET:3 * W_RET].astype(BF16)
    g = acc[:, 3 * W_RET:4 * W_RET]
    ret_ref[:, 3 * W_RET:4 * W_RET] = (g * jax.nn.sigmoid(g)).astype(BF16)

    acc = _dot_nt(hn, wgla_ref[...])
    gla_ref[:, 0:W_GLA] = (acc[:, 0:W_GLA] * scale).astype(BF16)
    gla_ref[:, W_GLA:3 * W_GLA] = acc[:, W_GLA:3 * W_GLA].astype(BF16)
    g = acc[:, 3 * W_GLA:4 * W_GLA]
    gla_ref[:, 3 * W_GLA:4 * W_GLA] = (g * jax.nn.sigmoid(g)).astype(BF16)
    low = acc[:, 4 * W_GLA:4 * W_GLA + LOW_PAD].astype(BF16)
    z = _dot(low, wa2_ref[...]) + ba_ref[...]
    log_sig = jnp.minimum(z, 0.0) - jnp.log(1.0 + jnp.exp(-jnp.abs(z)))
    lf_ref[...] = log_sig * (1.0 / GLA_TAU)

    acc = _dot_nt(hn, watt_ref[...])
    att_ref[:, 0:W_ATT] = (acc[:, 0:W_ATT] * (scale * LOG2E)).astype(BF16)
    att_ref[:, W_ATT:3 * W_ATT] = acc[:, W_ATT:3 * W_ATT].astype(BF16)
    if tiles_per_seq == 0:
        kc_ref[...] = acc[:, W_ATT:2 * W_ATT]
        vc_ref[...] = acc[:, 2 * W_ATT:3 * W_ATT]
    else:
        @pl.when(pl.program_id(0) % tiles_per_seq == tiles_per_seq - 1)
        def _():
            kc_ref[...] = acc[:, W_ATT:2 * W_ATT].T
            vc_ref[...] = acc[:, 2 * W_ATT:3 * W_ATT].T


def _in_proj(x2, n1w, w_in, w_a2, ba, cos_t, sin_t, *, layer, tm, seq_len):
    m = x2.shape[0]
    n_tiles = m // tm
    tab_blocks = cos_t.shape[0] // tm
    tiles_per_seq = seq_len // tm
    at_layer = functools.partial(_layer_spec, layer=layer)
    row = lambda w: pl.BlockSpec((tm, w), lambda i: (i, 0))
    tab = pl.BlockSpec((tm, LANES), lambda i: (i % tab_blocks, 0))
    if tiles_per_seq == 0:
        cache = row(W_ATT)
        cache_shape = jax.ShapeDtypeStruct((m, W_ATT), F32)
    else:
        cache = pl.BlockSpec((None, W_ATT, tm), lambda i: (i // tiles_per_seq, 0, 0))
        cache_shape = jax.ShapeDtypeStruct((m // seq_len, W_ATT, tm), F32)
    kern = functools.partial(_in_proj_kernel, tiles_per_seq=tiles_per_seq)
    return pl.pallas_call(
        kern,
        grid=(n_tiles,),
        in_specs=[row(D_MODEL), at_layer(n1w), at_layer(w_in), at_layer(w_a2), at_layer(ba),
                  tab, tab],
        out_specs=[row(4 * W_RET), row(4 * W_GLA), row(W_GLA), row(3 * W_ATT), cache, cache],
        out_shape=[
            jax.ShapeDtypeStruct((m, 4 * W_RET), BF16),
            jax.ShapeDtypeStruct((m, 4 * W_GLA), BF16),
            jax.ShapeDtypeStruct((m, W_GLA), F32),
            jax.ShapeDtypeStruct((m, 3 * W_ATT), BF16),
            cache_shape,
            cache_shape,
        ],
        scratch_shapes=[pltpu.VMEM((4 * W_RET, D_MODEL), BF16),
                        pltpu.VMEM((4 * W_GLA + LOW_PAD, D_MODEL), BF16),
                        pltpu.VMEM((3 * W_ATT, D_MODEL), BF16),
                        pltpu.VMEM((LOW_PAD, W_GLA), BF16)],
        compiler_params=_params(1),
        name="in_proj",
    )(x2, n1w, w_in, w_a2, ba, cos_t, sin_t)


def _load_block_diag(s0_ref, n_heads):
    zero = jnp.zeros((HEAD_DIM, HEAD_DIM), F32)
    return jnp.concatenate(
        [jnp.concatenate([s0_ref[i] if j == i else zero for j in range(n_heads)], axis=1)
         for i in range(n_heads)], axis=0)


def _store_diag_blocks(s, out_ref, n_heads):
    for i in range(n_heads):
        sl = slice(i * HEAD_DIM, (i + 1) * HEAD_DIM)
        out_ref[i] = s[sl, sl]


def _retention_kernel(*refs, lc, n_chunks, has_init):
    if has_init:
        (q_ref, k_ref, v_ref, s0_ref, dtab_ref, qdec_ref, kdec_ref, gtab_ref, bd_ref,
         o_ref, sfin_ref, s_scr) = refs
    else:
        (q_ref, k_ref, v_ref, dtab_ref, qdec_ref, kdec_ref, gtab_ref, bd_ref,
         o_ref, sfin_ref, s_scr) = refs
    g = pl.program_id(1)

    @pl.when(g == 0)
    def _():
        if has_init:
            s_scr[...] = _load_block_diag(s0_ref, H_RET)
        else:
            s_scr[...] = jnp.zeros_like(s_scr)

    chunk_rows = [slice(c * lc, (c + 1) * lc) for c in range(n_chunks)]
    dtab = dtab_ref[...]
    q = q_ref[...]
    k = k_ref[...]
    v = v_ref[...]
    kd = (k.astype(F32) * kdec_ref[...]).astype(BF16)
    gtab = gtab_ref[...]
    bd = bd_ref[...]
    states = [s_scr[...]]
    for r in chunk_rows:
        upd = _dot_tn(kd[r, :], v[r, :])
        states.append(states[-1] * gtab + upd * bd)
    s_scr[...] = states[-1]
    for c, r in enumerate(chunk_rows):
        scores = _dot_nt(q[r, :], _block_diag_rows(k[r, :], H_RET))
        p = (scores * dtab).astype(BF16)
        intra = _dot(p, _block_diag_rows(v[r, :], H_RET))
        inter = _dot(q[r, :], states[c].astype(BF16))
        o_ref[r, :] = intra + inter * qdec_ref[r, :]

    @pl.when(g == pl.num_programs(1) - 1)
    def _():
        _store_diag_blocks(s_scr[...], sfin_ref, H_RET)


def _state_specs(states, layer, n_heads):
    blk = (n_heads, HEAD_DIM, HEAD_DIM)
    out_spec = pl.BlockSpec((None,) + blk, lambda b, g: (b, 0, 0, 0))
    if states is None:
        return [], [], out_spec
    in_spec = pl.BlockSpec((None, None) + blk, lambda b, g: (layer, b, 0, 0, 0))
    return [in_spec], [states], out_spec


def _retention(ret, states, layer, tabs, *, batch, seq_len, lc, n_chunks):
    rows = lc * n_chunks
    ng = seq_len // rows
    m = batch * seq_len
    col = lambda c: pl.BlockSpec((rows, W_RET), lambda b, g: (b * ng + g, c))
    full = lambda a: pl.BlockSpec(a.shape, lambda b, g: (0,) * a.ndim)
    s_in_spec, s_in, s_out_spec = _state_specs(states, layer, H_RET)
    kern = functools.partial(_retention_kernel, lc=lc, n_chunks=n_chunks,
                             has_init=states is not None)
    return pl.pallas_call(
        kern,
        grid=(batch, ng),
        in_specs=[col(0), col(1), col(2)] + s_in_spec + [full(t) for t in tabs],
        out_specs=[pl.BlockSpec((rows, W_RET), lambda b, g: (b * ng + g, 0)), s_out_spec],
        out_shape=[jax.ShapeDtypeStruct((m, W_RET), F32),
                   jax.ShapeDtypeStruct((batch, H_RET, HEAD_DIM, HEAD_DIM), F32)],
        scratch_shapes=[pltpu.VMEM((W_RET, W_RET), F32)],
        compiler_params=_params(2),
        name="retention",
    )(ret, ret, ret, *s_in, *tabs)


def _retention_tables(lc, n_chunks):
    lg = np.log1p(-np.exp2(-5.0 - np.arange(H_RET, dtype=np.float64)))
    idx = np.arange(lc)
    diff = idx[:, None] - idx[None, :]
    causal = diff >= 0
    dpos = np.where(causal, diff, 0).astype(np.float64)
    decay = np.where(causal[None], np.exp(dpos[None] * lg[:, None, None]), 0.0)
    dtab = np.transpose(decay, (1, 0, 2)).reshape(lc, H_RET * lc)
    q_decay = np.exp((idx + 1).astype(np.float64)[:, None] * lg[None, :])
    k_decay = np.exp((lc - 1 - idx).astype(np.float64)[:, None] * lg[None, :])
    qdec = np.tile(np.repeat(q_decay, HEAD_DIM, axis=1), (n_chunks, 1))
    kdec = np.tile(np.repeat(k_decay, HEAD_DIM, axis=1), (n_chunks, 1))
    gtab = np.broadcast_to(np.repeat(np.exp(lc * lg), HEAD_DIM)[:, None], (W_RET, W_RET))
    head = np.arange(W_RET) // HEAD_DIM
    bd = head[:, None] == head[None, :]
    return tuple(jnp.asarray(t, F32) for t in (dtab, qdec, kdec, gtab, bd))


def _split_bf16(x):
    hi = x.astype(BF16)
    lo = (x - hi.astype(F32)).astype(BF16)
    return jnp.concatenate([hi, lo], axis=1)


def _group_reference(b, row, half):
    n, w = b.shape
    group = 2 * half
    if group >= 8:
        b3 = b.reshape(n // group, group, w)
        return jnp.broadcast_to(b3[:, half - 1:half, :], b3.shape).reshape(n, w)
    if group == 4:
        i = row & 3
        return jnp.where(i == 0, pltpu.roll(b, n - 1, 0),
                         jnp.where(i == 1, b,
                                   jnp.where(i == 2, pltpu.roll(b, 1, 0), pltpu.roll(b, 2, 0))))
    return jnp.where((row & 1) == 1, pltpu.roll(b, 1, 0), b)


def _gla_kernel(*refs, lc, n_chunks, has_init):
    if has_init:
        (q_ref, k_ref, v_ref, lf_ref, s0_ref, tri_ref, bd_ref, onehot_ref,
         o_ref, sfin_ref, s_scr, sc_scr) = refs
    else:
        (q_ref, k_ref, v_ref, lf_ref, tri_ref, bd_ref, onehot_ref,
         o_ref, sfin_ref, s_scr, sc_scr) = refs
    g = pl.program_id(1)

    @pl.when(g == 0)
    def _():
        if has_init:
            s_scr[...] = _load_block_diag(s0_ref, H_GLA)
        else:
            s_scr[...] = jnp.zeros_like(s_scr)

    n_lev = lc.bit_length() - 1
    w = W_GLA
    n = lc * n_chunks
    chunk_rows = [slice(c * lc, (c + 1) * lc) for c in range(n_chunks)]
    row = lax.broadcasted_iota(jnp.int32, (n, w), 0)
    trow = lax.broadcasted_iota(jnp.int32, (lc, H_GLA * lc), 0)
    scol = lax.broadcasted_iota(jnp.int32, (lc, H_GLA * lc), 1) & (lc - 1)
    q = q_ref[...]
    k = k_ref[...]
    v = v_ref[...]
    q32 = q.astype(F32)
    k32 = k.astype(F32)
    tri = tri_ref[...]
    lfs = _split_bf16(lf_ref[...])
    cs = jnp.concatenate([_dot(tri, lfs[r, :]) for r in chunk_rows], axis=0)
    b = cs[:, :w] + cs[:, w:]

    qe = (q32 * jnp.exp(b)).astype(BF16)
    mild = jnp.min(b) >= -GLA_SAFE_LOG_DECAY

    @pl.when(mild)
    def _():
        ke = (k32 * jnp.exp(-b)).astype(BF16)
        for r in chunk_rows:
            sc = _dot_nt(qe[r, :], _block_diag_rows(ke[r, :], H_GLA))
            sc_scr[r, :] = jnp.where(trow >= scol, sc, 0.0)

    @pl.when(jnp.logical_not(mild))
    def _():
        for r in chunk_rows:
            sc = _dot_nt(q[r, :], _block_diag_rows(k[r, :], H_GLA))
            sc_scr[r, :] = jnp.where(trow == scol, sc, 0.0)
        for lev in range(n_lev):
            half = lc >> (lev + 1)
            shift = half.bit_length()
            upper = (row & half) != 0
            e = jnp.exp(-jnp.abs(b - _group_reference(b, row, half)))
            z = (jnp.where(upper, q32, k32) * e).astype(BF16)
            valid = (((trow >> shift) == (scol >> shift))
                     & ((trow & half) != 0) & ((scol & half) == 0))
            for r in chunk_rows:
                zc = z[r, :]
                sc = _dot_nt(zc, _block_diag_rows(zc, H_GLA))
                sc_scr[r, :] = jnp.where(valid, sc, sc_scr[r, :])

    b3 = b.reshape(n_chunks, lc, w)
    b_last = b3[:, lc - 1:lc, :]
    kd = (k32 * jnp.exp(jnp.broadcast_to(b_last, b3.shape).reshape(n, w) - b)).astype(BF16)
    col_sum = _dot_tn(lfs, onehot_ref[...])
    decay_col = jnp.exp(col_sum[:w, :] + col_sum[w:, :])
    bd = bd_ref[...]
    states = [s_scr[...]]
    for c, r in enumerate(chunk_rows):
        upd = _dot_tn(kd[r, :], v[r, :])
        states.append(states[-1] * decay_col[:, c:c + 1] + upd * bd)
    s_scr[...] = states[-1]
    for c, r in enumerate(chunk_rows):
        inter = _dot(qe[r, :], states[c].astype(BF16))
        intra = _dot(sc_scr[r, :].astype(BF16), _block_diag_rows(v[r, :], H_GLA))
        o_ref[r, :] = intra + inter

    @pl.when(g == pl.num_programs(1) - 1)
    def _():
        _store_diag_blocks(s_scr[...], sfin_ref, H_GLA)


def _gla(gla, lf, states, layer, tabs, *, batch, seq_len, lc, n_chunks):
    rows = lc * n_chunks
    ng = seq_len // rows
    m = batch * seq_len
    col = lambda c: pl.BlockSpec((rows, W_GLA), lambda b, g: (b * ng + g, c))
    full = lambda a: pl.BlockSpec(a.shape, lambda b, g: (0,) * a.ndim)
    s_in_spec, s_in, s_out_spec = _state_specs(states, layer, H_GLA)
    kern = functools.partial(_gla_kernel, lc=lc, n_chunks=n_chunks, has_init=states is not None)
    return pl.pallas_call(
        kern,
        grid=(batch, ng),
        in_specs=[col(0), col(1), col(2), col(0)] + s_in_spec + [full(t) for t in tabs],
        out_specs=[pl.BlockSpec((rows, W_GLA), lambda b, g: (b * ng + g, 0)), s_out_spec],
        out_shape=[jax.ShapeDtypeStruct((m, W_GLA), F32),
                   jax.ShapeDtypeStruct((batch, H_GLA, HEAD_DIM, HEAD_DIM), F32)],
        scratch_shapes=[pltpu.VMEM((W_GLA, W_GLA), F32),
                        pltpu.VMEM((rows, H_GLA * lc), F32)],
        compiler_params=_params(2),
        name="gla",
    )(gla, gla, gla, lf, *s_in, *tabs)


def _gla_tables(lc, n_chunks):
    idx = np.arange(lc)
    tri = idx[:, None] >= idx[None, :]
    head = np.arange(W_GLA) // HEAD_DIM
    bd = head[:, None] == head[None, :]
    chunk_of_row = np.arange(lc * n_chunks) // lc
    onehot = chunk_of_row[:, None] == np.arange(LANES)[None, :]
    return jnp.asarray(tri, BF16), jnp.asarray(bd, F32), jnp.asarray(onehot, BF16)


def _band_attn_kernel(q_ref, kc_ref, vc_ref, gv_ref, o_ref,
                      kwin_t, vwin, bias_scr, s_scr, m_scr, *, lc, rpi, n_iter):
    b_id = pl.program_id(0)
    g = pl.program_id(1)
    win = ATT_REACH + rpi
    rows = rpi * n_iter

    @pl.when((b_id == 0) & (g == 0))
    def _():
        row = lax.broadcasted_iota(jnp.int32, (rpi, BIAS_W), 0)
        rel = lax.broadcasted_iota(jnp.int32, (rpi, BIAS_W), 1) - (row // lc) * lc
        in_band = (rel >= 0) & (rel < ATT_REACH + lc)
        for h in range(H_ATT):
            x = jnp.broadcast_to(gv_ref[h:h + 1, :], (rpi, BIAS_W))
            x = pltpu.roll(x, BIAS_W - (lc - 1), 1, stride=1, stride_axis=0)
            bias_scr[h // 2, (h % 2) * rpi:(h % 2 + 1) * rpi, :] = jnp.where(
                in_band, x * LOG2E, NEG_BIG)

    @pl.when(g == 0)
    def _():
        kwin_t[:, 0:ATT_REACH] = jnp.zeros((W_ATT, ATT_REACH), BF16)
        vwin[0:ATT_REACH, :] = jnp.zeros((ATT_REACH, W_ATT), BF16)

    @pl.when(g != 0)
    def _():
        kwin_t[:, 0:ATT_REACH] = kwin_t[:, ATT_REACH:ATT_REACH + rows]
        vwin[0:ATT_REACH, :] = vwin[ATT_REACH:ATT_REACH + rows, :]

    kwin_t[:, ATT_REACH:ATT_REACH + rows] = kc_ref[...].astype(F32).T.astype(BF16)
    vwin[ATT_REACH:ATT_REACH + rows, :] = vc_ref[...]

    lane = lax.broadcasted_iota(jnp.int32, (rpi, LANES), 1)
    lane2 = lax.broadcasted_iota(jnp.int32, (2 * rpi, LANES), 1)
    head2 = lax.broadcasted_iota(jnp.int32, (2 * rpi, LANES), 0) // rpi
    col = lax.broadcasted_iota(jnp.int32, (2 * rpi, win), 1)

    def step(i, carry, *, masked):
        r0 = pl.multiple_of(i * rpi, rpi)
        first_valid = ATT_REACH - (g * rows + r0)
        for pair in range(H_ATT // 2):
            lanes = slice(pair * LANES, (pair + 1) * LANES)
            qp = q_ref[pl.ds(r0, rpi), lanes]
            q2 = jnp.concatenate([qp, qp], axis=0)
            q2 = jnp.where((lane2 // HEAD_DIM) == head2, q2, jnp.zeros_like(q2))
            s = _dot(q2, kwin_t[lanes, pl.ds(r0, win)]) + bias_scr[pair][:, :win]
            if masked:
                s = jnp.where(col >= first_valid, s, NEG_BIG)
            s_scr[pair, :, :win] = s
            m_scr[pair] = jnp.broadcast_to(jnp.max(s, axis=-1, keepdims=True), (2 * rpi, LANES))
        for pair in range(H_ATT // 2):
            lanes = slice(pair * LANES, (pair + 1) * LANES)
            e = jnp.exp2(s_scr[pair, :, :win] - m_scr[pair][:, :1])
            den = jnp.sum(e, axis=-1, keepdims=True)
            pv = _dot(e.astype(BF16), vwin[pl.ds(r0, win), lanes]) / den
            o_ref[pl.ds(r0, rpi), lanes] = jnp.where(
                lane < HEAD_DIM, pv[:rpi, :], pv[rpi:, :]).astype(BF16)
        return carry

    @pl.when(g == 0)
    def _():
        lax.fori_loop(0, n_iter, functools.partial(step, masked=True), 0)

    @pl.when(g != 0)
    def _():
        lax.fori_loop(0, n_iter, functools.partial(step, masked=False), 0)


def _band_attn(att, gv, layer, *, batch, seq_len, lc, rpi, n_iter):
    rows = rpi * n_iter
    assert rows == ATT_REACH, "the carried window is exactly one row group"
    ng = seq_len // rows
    m = batch * seq_len
    cur = lambda c: pl.BlockSpec((rows, W_ATT), lambda b, g: (b * ng + g, c))
    kern = functools.partial(_band_attn_kernel, lc=lc, rpi=rpi, n_iter=n_iter)
    return pl.pallas_call(
        kern,
        grid=(batch, ng),
        in_specs=[cur(0), cur(1), cur(2), _layer_spec(gv, layer)],
        out_specs=pl.BlockSpec((rows, W_ATT), lambda b, g: (b * ng + g, 0)),
        out_shape=jax.ShapeDtypeStruct((m, W_ATT), BF16),
        scratch_shapes=[pltpu.VMEM((W_ATT, ATT_REACH + rows), BF16),
                        pltpu.VMEM((ATT_REACH + rows, W_ATT), BF16),
                        pltpu.VMEM((H_ATT // 2, 2 * rpi, BIAS_W), F32),
                        pltpu.VMEM((H_ATT // 2, 2 * rpi, BIAS_W), F32),
                        pltpu.VMEM((H_ATT // 2, 2 * rpi, LANES), F32)],
        compiler_params=_params(2),
        name="band_attn",
    )(att, att, att, gv)


def _sample_attn_kernel(q_ref, kn_ref, vn_ref, kt_ref, vt_ref, gv_ref, o_ref, *, ls):
    win = ATT_REACH + ls
    lane = lax.broadcasted_iota(jnp.int32, (ls, LANES), 1)
    for pair in range(H_ATT // 2):
        lanes = slice(pair * LANES, (pair + 1) * LANES)
        qp = q_ref[:, lanes]
        knp = kn_ref[:, lanes]
        vnp = vn_ref[:, lanes]
        kt = kt_ref[2 * pair:2 * pair + 2].reshape(2 * HEAD_DIM, ATT_REACH).astype(BF16)
        vt = vt_ref[2 * pair:2 * pair + 2].reshape(2 * HEAD_DIM, ATT_REACH).astype(BF16)
        outs = []
        for hh in range(2):
            h = 2 * pair + hh
            x = jnp.broadcast_to(gv_ref[h:h + 1, :], (ls, BIAS_W))
            bias = pltpu.roll(x, BIAS_W - (ls - 1), 1, stride=1, stride_axis=0) * LOG2E
            qh = jnp.where((lane // HEAD_DIM) == hh, qp, jnp.zeros_like(qp))
            s_old = _dot(qh, kt) + bias[:, :ATT_REACH]
            s_new = _dot_nt(qh, knp) + bias[:, ATT_REACH:win]
            mx = jnp.maximum(jnp.max(s_old, axis=-1, keepdims=True),
                             jnp.max(s_new, axis=-1, keepdims=True))
            e_old = jnp.exp2(s_old - mx)
            e_new = jnp.exp2(s_new - mx)
            den = (jnp.sum(e_old, axis=-1, keepdims=True)
                   + jnp.sum(e_new, axis=-1, keepdims=True))
            pv = _dot_nt(e_old.astype(BF16), vt) + _dot(e_new.astype(BF16), vnp)
            outs.append(pv / den)
        o_ref[:, lanes] = jnp.where(lane < HEAD_DIM, outs[0], outs[1]).astype(BF16)


def _sample_attn(att, cache_kt, cache_vt, layer, gv, *, batch, ls):
    new = lambda c: pl.BlockSpec((ls, W_ATT), lambda b: (b, c))
    old = pl.BlockSpec((None, None, H_ATT, HEAD_DIM, ATT_REACH), lambda b: (layer, b, 0, 0, 0))
    kern = functools.partial(_sample_attn_kernel, ls=ls)
    return pl.pallas_call(
        kern,
        grid=(batch,),
        in_specs=[new(0), new(1), new(2), old, old, _layer_spec(gv, layer)],
        out_specs=pl.BlockSpec((ls, W_ATT), lambda b: (b, 0)),
        out_shape=jax.ShapeDtypeStruct((batch * ls, W_ATT), BF16),
        compiler_params=_params(1),
        name="sample_attn",
    )(att, att, att, cache_kt, cache_vt, gv)


def _bias_rows(rel_bias_l, lc):
    wv = np.arange(BIAS_W)
    dist = ATT_REACH + lc - 1 - wv
    ridx = np.clip(dist, -(CHUNK - 1), REL_MAX) + (CHUNK - 1)
    return rel_bias_l[..., ridx]


def _out_ffn_kernel(x_ref, oa_ref, ob_ref, oc_ref, sga_ref, sgb_ref, gnw_ref, glw_ref,
                    n2w_ref, fnw_ref, ones_ref, wout_ref, wup_ref, wdown_ref, y_ref,
                    *, final_norm):
    ones = ones_ref[...]

    def head_sum(t):
        hi = t.astype(BF16)
        lo = (t - hi.astype(F32)).astype(BF16)
        return _dot(hi, ones) + _dot(lo, ones)

    inv_d = 1.0 / HEAD_DIM
    oa = oa_ref[...]
    mu = head_sum(oa) * inv_d
    da = oa - mu
    var = head_sum(da * da) * inv_d
    a = da * lax.rsqrt(var + EPS) * gnw_ref[...] * sga_ref[...].astype(F32)
    ob = ob_ref[...]
    ms = head_sum(ob * ob) * inv_d
    bb = ob * lax.rsqrt(ms + EPS) * glw_ref[...] * sgb_ref[...].astype(F32)
    cat = jnp.concatenate([a.astype(BF16), bb.astype(BF16), oc_ref[...]], axis=1)
    h1 = x_ref[...] + _dot(cat, wout_ref[...])
    hn = _rmsnorm(h1, n2w_ref[...]).astype(BF16)
    acc = jnp.zeros_like(h1)
    n_slab = D_FF // D_MODEL
    for j in range(n_slab):
        sl = slice(j * D_MODEL, (j + 1) * D_MODEL)
        u = jnp.maximum(_dot(hn, wup_ref[:, sl]), 0.0)
        acc = acc + _dot((u * u).astype(BF16), wdown_ref[sl, :])
    h2 = h1 + acc
    if final_norm:
        h2 = _rmsnorm(h2, fnw_ref[...])
    y_ref[...] = h2


def _out_ffn(x2, oa, ob, oc, ret, gla, gnw, glw, n2w, fnw, ones, wout, wup, wdown,
             *, layer, tm, final_norm):
    m = x2.shape[0]
    full = lambda a: pl.BlockSpec(a.shape, lambda i: (0,) * a.ndim)
    at_layer = functools.partial(_layer_spec, layer=layer)
    row = lambda w, c=0: pl.BlockSpec((tm, w), lambda i: (i, c))
    kern = functools.partial(_out_ffn_kernel, final_norm=final_norm)
    return pl.pallas_call(
        kern,
        grid=(m // tm,),
        in_specs=[row(D_MODEL), row(W_RET), row(W_GLA), row(W_ATT), row(W_RET, 3),
                  row(W_GLA, 3), at_layer(gnw), at_layer(glw), at_layer(n2w), full(fnw),
                  full(ones), at_layer(wout), at_layer(wup), at_layer(wdown)],
        out_specs=row(D_MODEL),
        out_shape=jax.ShapeDtypeStruct((m, D_MODEL), F32),
        compiler_params=_params(1),
        name="out_ffn",
    )(x2, oa, ob, oc, ret, gla, gnw, glw, n2w, fnw, ones, wout, wup, wdown)


def _rotary_tables(pos, rows):
    half = HEAD_DIM // 2
    inv_freq = ROPE_BASE ** (-np.arange(half, dtype=np.float64) / half)
    ang = np.asarray(pos, np.float64)[:, None] * inv_freq[None, :]
    reps = (rows // len(pos), LANES // half)
    return (jnp.asarray(np.tile(np.cos(ang), reps), F32),
            jnp.asarray(np.tile(np.sin(ang), reps), F32))


def _run_path(x2, p, *, batch, seq_len, lc, n_chunks, tm, pos, layer, states, caches,
              final_norm):
    cos_t, sin_t = _rotary_tables(pos, max(seq_len, tm))
    ret, gla, lf, att, kc32, vc32 = _in_proj(
        x2, p["n1w"], p["w_in"], p["w_a2"], p["ba"], cos_t, sin_t,
        layer=layer, tm=tm, seq_len=seq_len)
    s_ret0, s_gla0 = states if states is not None else (None, None)
    oa, s_ret = _retention(ret, s_ret0, layer, _retention_tables(lc, n_chunks),
                           batch=batch, seq_len=seq_len, lc=lc, n_chunks=n_chunks)
    ob, s_gla = _gla(gla, lf, s_gla0, layer, _gla_tables(lc, n_chunks),
                     batch=batch, seq_len=seq_len, lc=lc, n_chunks=n_chunks)
    gv = _bias_rows(p["rel_bias"], lc)
    if caches is None:
        rpi = min(lc * n_chunks, ATT_ROWS_PER_STEP)
        oc = _band_attn(att, gv, layer, batch=batch, seq_len=seq_len, lc=lc, rpi=rpi,
                        n_iter=lc * n_chunks // rpi)
    else:
        oc = _sample_attn(att, caches[0], caches[1], layer, gv, batch=batch, ls=seq_len)
    y = _out_ffn(x2, oa, ob, oc, ret, gla, p["gnw"], p["glw"], p["n2w"], p["fnw"], p["ones"],
                 p["wout"], p["wup"], p["wdown"], layer=layer, tm=tm, final_norm=final_norm)
    return y, s_ret, s_gla, kc32, vc32


def kernel(x_prompt, x_sample, state_ret, state_gla, cache_attn_k, cache_attn_v, norm1_w, norm2_w, final_norm_w, w_in, w_gla_a2, b_gla_a, ret_gn_w, gla_norm_w, rel_bias, w_out, w_up, w_down):
    depth = w_in.shape[0]
    bp, lp, _ = x_prompt.shape
    bs, ls, _ = x_sample.shape
    keep = min(ATT_REACH, lp)
    head = np.arange(W_RET) // HEAD_DIM
    ones = jnp.asarray(head[:, None] == head[None, :], BF16)
    hp = x_prompt.reshape(bp * lp, D_MODEL)
    hs = x_sample.reshape(bs * ls, D_MODEL)
    pos_p = np.arange(lp)
    pos_s = PAST_LEN + np.arange(ls)
    tm_p = keep
    to_head_major = lambda c: jnp.transpose(c, (0, 1, 3, 4, 2))
    caches = (to_head_major(cache_attn_k.astype(F32)), to_head_major(cache_attn_v.astype(F32)))
    from_head_major = lambda t: jnp.transpose(
        t.reshape(bp, H_ATT, HEAD_DIM, keep), (0, 3, 1, 2))
    states = (state_ret.astype(F32), state_gla.astype(F32))
    outs = {k: [] for k in ("ret_p", "gla_p", "kp", "vp", "ret_s", "gla_s", "ks", "vs")}
    rows3 = lambda v: v.astype(F32).reshape(depth, 1, v.shape[-1])
    params = dict(
        n1w=rows3(norm1_w), n2w=rows3(norm2_w), fnw=final_norm_w.astype(F32)[None, :],
        gnw=rows3(ret_gn_w), glw=rows3(gla_norm_w), ba=rows3(b_gla_a),
        w_in=jnp.swapaxes(w_in.astype(F32), 1, 2), w_a2=w_gla_a2.astype(F32),
        rel_bias=rel_bias.astype(F32),
        wout=w_out.astype(BF16), wup=w_up.astype(BF16), wdown=w_down.astype(BF16), ones=ones)
    for l in range(depth):
        last = l == depth - 1
        hp, s_ret, s_gla, kt32, vt32 = _run_path(
            hp, params, batch=bp, seq_len=lp, lc=CHUNK, n_chunks=ATT_REACH // CHUNK, tm=tm_p,
            pos=pos_p, layer=l, states=None, caches=None, final_norm=last)
        outs["ret_p"].append(s_ret)
        outs["gla_p"].append(s_gla)
        outs["kp"].append(from_head_major(kt32))
        outs["vp"].append(from_head_major(vt32))
        hs, s_ret, s_gla, kc32, vc32 = _run_path(
            hs, params, batch=bs, seq_len=ls, lc=ls, n_chunks=1, tm=bs * ls,
            pos=pos_s, layer=l, states=states, caches=caches, final_norm=last)
        outs["ret_s"].append(s_ret)
        outs["gla_s"].append(s_gla)
        outs["ks"].append(kc32.reshape(bs, ls, H_ATT, HEAD_DIM))
        outs["vs"].append(vc32.reshape(bs, ls, H_ATT, HEAD_DIM))
    y_prompt = hp.reshape(bp, lp, D_MODEL)
    y_sample = hs.reshape(bs, ls, D_MODEL)
    st = lambda k: jnp.stack(outs[k])
    return (y_prompt, y_sample, st("ret_p"), st("gla_p"), st("kp"), st("vp"),
            st("ret_s"), st("gla_s"), st("ks"), st("vs"))
```

```python
import functools

import numpy as np
import jax
import jax.numpy as jnp
from jax import lax
from jax.experimental import pallas as pl
from jax.experimental.pallas import tpu as pltpu

D_MODEL = 1024
CHUNK = 64
HEAD_DIM = 64
H_RET = 4
H_GLA = 4
H_ATT = 8
W_RET = H_RET * HEAD_DIM
W_GLA = H_GLA * HEAD_DIM
W_ATT = H_ATT * HEAD_DIM
GLA_RANK = 16
GLA_TAU = 16.0
ATT_REACH = 512
REL_MAX = 256
N_REL = CHUNK + REL_MAX
D_FF = 4 * D_MODEL
IN_COLS = 4 * W_RET + 4 * W_GLA + GLA_RANK + 3 * W_ATT
ROPE_BASE = 10000.0
EPS = 1e-6
PAST_LEN = 4096

LANES = 128
LOW_PAD = LANES
BIAS_W = 640
GLA_SAFE_LOG_DECAY = 50.0
ATT_SAFE_LOG2 = 80.0
NORM_ROUNDING_MARGIN = 1.05
ATT_ROWS_PER_STEP = 128
LOG2E = 1.4426950408889634
NEG_BIG = -1e30
VMEM_LIMIT = 56 * 1024 * 1024

F32 = jnp.float32
BF16 = jnp.bfloat16

_NT = (((1,), (1,)), ((), ()))
_TN = (((0,), (0,)), ((), ()))


def _dot(a, b):
    return jnp.dot(a, b, preferred_element_type=F32)


def _dot_nt(a, b):
    return lax.dot_general(a, b, _NT, preferred_element_type=F32)


def _dot_tn(a, b):
    return lax.dot_general(a, b, _TN, preferred_element_type=F32)


def _rmsnorm(x, w):
    return x * lax.rsqrt(jnp.mean(x * x, axis=-1, keepdims=True) + EPS) * w


def _params(n_grid_dims):
    return pltpu.CompilerParams(
        dimension_semantics=("arbitrary",) * n_grid_dims,
        vmem_limit_bytes=VMEM_LIMIT,
    )


def _layer_spec(a, layer):
    index = (layer,) + (0,) * (a.ndim - 1)
    return pl.BlockSpec((None,) + a.shape[1:], lambda *_: index, pipeline_mode=pl.Buffered(1))


def _block_diag_rows(x, n_heads):
    lane_head = lax.broadcasted_iota(jnp.int32, x.shape, 1) // HEAD_DIM
    zero = jnp.zeros_like(x)
    return jnp.concatenate(
        [jnp.where(lane_head == h, x, zero) for h in range(n_heads)], axis=0)


def _in_proj_kernel(x_ref, n1w_ref, win_ref, wa2in_ref, ba_ref, cos_ref, sin_ref,
                    ret_ref, gla_ref, lf_ref, att_ref, kc_ref, vc_ref,
                    wret_ref, wgla_ref, watt_ref, wa2_ref, *, tiles_per_seq):
    @pl.when(pl.program_id(0) == 0)
    def _():
        n_main = 4 * W_RET + 4 * W_GLA
        wret_ref[...] = win_ref[0:4 * W_RET, :].astype(BF16)
        wgla_ref[0:4 * W_GLA, :] = win_ref[4 * W_RET:n_main, :].astype(BF16)
        wgla_ref[4 * W_GLA:4 * W_GLA + GLA_RANK, :] = (
            win_ref[n_main:n_main + GLA_RANK, :].astype(BF16))
        wgla_ref[4 * W_GLA + GLA_RANK:, :] = jnp.zeros((LOW_PAD - GLA_RANK, D_MODEL), BF16)
        watt_ref[...] = win_ref[n_main + GLA_RANK:IN_COLS, :].astype(BF16)
        wa2_ref[0:GLA_RANK, :] = wa2in_ref[...].astype(BF16)
        wa2_ref[GLA_RANK:, :] = jnp.zeros((LOW_PAD - GLA_RANK, W_GLA), BF16)

    hn = _rmsnorm(x_ref[...], n1w_ref[...]).astype(BF16)
    cos = cos_ref[...]
    sin = sin_ref[...]
    first_half = (lax.broadcasted_iota(jnp.int32, cos.shape, 1) % HEAD_DIM) < (HEAD_DIM // 2)

    def rotary(t):
        swapped = jnp.where(first_half, -pltpu.roll(t, LANES - HEAD_DIM // 2, 1),
                            pltpu.roll(t, HEAD_DIM // 2, 1))
        return t * cos + swapped * sin

    scale = HEAD_DIM ** -0.5

    acc = _dot_nt(hn, wret_ref[...])
    for j in range(W_RET // LANES):
        sl = slice(j * LANES, (j + 1) * LANES)
        ret_ref[:, sl] = rotary(acc[:, sl]).astype(BF16)
        sk = slice(W_RET + j * LANES, W_RET + (j + 1) * LANES)
        ret_ref[:, sk] = (rotary(acc[:, sk]) * scale).astype(BF16)
    ret_ref[:, 2 * W_RET:3 * W_RET] = acc[:, 2 * W_RET:3 * W_RET].astype(BF16)
    g = acc[:, 3 * W_RET:4 * W_RET]
    ret_ref[:, 3 * W_RET:4 * W_RET] = (g * jax.nn.sigmoid(g)).astype(BF16)

    acc = _dot_nt(hn, wgla_ref[...])
    gla_ref[:, 0:W_GLA] = (acc[:, 0:W_GLA] * scale).astype(BF16)
    gla_ref[:, W_GLA:3 * W_GLA] = acc[:, W_GLA:3 * W_GLA].astype(BF16)
    g = acc[:, 3 * W_GLA:4 * W_GLA]
    gla_ref[:, 3 * W_GLA:4 * W_GLA] = (g * jax.nn.sigmoid(g)).astype(BF16)
    low = acc[:, 4 * W_GLA:4 * W_GLA + LOW_PAD].astype(BF16)
    z = _dot(low, wa2_ref[...]) + ba_ref[...]
    log_sig = jnp.minimum(z, 0.0) - jnp.log(1.0 + jnp.exp(-jnp.abs(z)))
    lf_ref[...] = log_sig * (1.0 / GLA_TAU)

    acc = _dot_nt(hn, watt_ref[...])
    att_ref[:, 0:W_ATT] = (acc[:, 0:W_ATT] * (scale * LOG2E)).astype(BF16)
    att_ref[:, W_ATT:3 * W_ATT] = acc[:, W_ATT:3 * W_ATT].astype(BF16)
    if tiles_per_seq == 0:
        kc_ref[...] = acc[:, W_ATT:2 * W_ATT]
        vc_ref[...] = acc[:, 2 * W_ATT:3 * W_ATT]
    else:
        @pl.when(pl.program_id(0) % tiles_per_seq == tiles_per_seq - 1)
        def _():
            kc_ref[...] = acc[:, W_ATT:2 * W_ATT].T
            vc_ref[...] = acc[:, 2 * W_ATT:3 * W_ATT].T


def _in_proj(x2, n1w, w_in, w_a2, ba, cos_t, sin_t, *, layer, tm, seq_len):
    m = x2.shape[0]
    n_tiles = m // tm
    tab_blocks = cos_t.shape[0] // tm
    tiles_per_seq = seq_len // tm
    at_layer = functools.partial(_layer_spec, layer=layer)
    row = lambda w: pl.BlockSpec((tm, w), lambda i: (i, 0))
    tab = pl.BlockSpec((tm, LANES), lambda i: (i % tab_blocks, 0))
    if tiles_per_seq == 0:
        cache = row(W_ATT)
        cache_shape = jax.ShapeDtypeStruct((m, W_ATT), F32)
    else:
        cache = pl.BlockSpec((None, W_ATT, tm), lambda i: (i // tiles_per_seq, 0, 0))
        cache_shape = jax.ShapeDtypeStruct((m // seq_len, W_ATT, tm), F32)
    kern = functools.partial(_in_proj_kernel, tiles_per_seq=tiles_per_seq)
    return pl.pallas_call(
        kern,
        grid=(n_tiles,),
        in_specs=[row(D_MODEL), at_layer(n1w), at_layer(w_in), at_layer(w_a2), at_layer(ba),
                  tab, tab],
        out_specs=[row(4 * W_RET), row(4 * W_GLA), row(W_GLA), row(3 * W_ATT), cache, cache],
        out_shape=[
            jax.ShapeDtypeStruct((m, 4 * W_RET), BF16),
            jax.ShapeDtypeStruct((m, 4 * W_GLA), BF16),
            jax.ShapeDtypeStruct((m, W_GLA), F32),
            jax.ShapeDtypeStruct((m, 3 * W_ATT), BF16),
            cache_shape,
            cache_shape,
        ],
        scratch_shapes=[pltpu.VMEM((4 * W_RET, D_MODEL), BF16),
                        pltpu.VMEM((4 * W_GLA + LOW_PAD, D_MODEL), BF16),
                        pltpu.VMEM((3 * W_ATT, D_MODEL), BF16),
                        pltpu.VMEM((LOW_PAD, W_GLA), BF16)],
        compiler_params=_params(1),
        name="in_proj",
    )(x2, n1w, w_in, w_a2, ba, cos_t, sin_t)


def _load_block_diag(s0_ref, n_heads):
    zero = jnp.zeros((HEAD_DIM, HEAD_DIM), F32)
    return jnp.concatenate(
        [jnp.concatenate([s0_ref[i] if j == i else zero for j in range(n_heads)], axis=1)
         for i in range(n_heads)], axis=0)


def _store_diag_blocks(s, out_ref, n_heads):
    for i in range(n_heads):
        sl = slice(i * HEAD_DIM, (i + 1) * HEAD_DIM)
        out_ref[i] = s[sl, sl]


def _retention_kernel(*refs, lc, n_chunks, has_init):
    if has_init:
        (q_ref, k_ref, v_ref, s0_ref, dtab_ref, qdec_ref, kdec_ref, gtab_ref, bd_ref,
         o_ref, sfin_ref, s_scr) = refs
    else:
        (q_ref, k_ref, v_ref, dtab_ref, qdec_ref, kdec_ref, gtab_ref, bd_ref,
         o_ref, sfin_ref, s_scr) = refs
    g = pl.program_id(1)

    @pl.when(g == 0)
    def _():
        if has_init:
            s_scr[...] = _load_block_diag(s0_ref, H_RET)
        else:
            s_scr[...] = jnp.zeros_like(s_scr)

    chunk_rows = [slice(c * lc, (c + 1) * lc) for c in range(n_chunks)]
    dtab = dtab_ref[...]
    q = q_ref[...]
    k = k_ref[...]
    v = v_ref[...]
    kd = (k.astype(F32) * kdec_ref[...]).astype(BF16)
    gtab = gtab_ref[...]
    bd = bd_ref[...]
    states = [s_scr[...]]
    for r in chunk_rows:
        upd = _dot_tn(kd[r, :], v[r, :])
        states.append(states[-1] * gtab + upd * bd)
    s_scr[...] = states[-1]
    for c, r in enumerate(chunk_rows):
        scores = _dot_nt(q[r, :], _block_diag_rows(k[r, :], H_RET))
        p = (scores * dtab).astype(BF16)
        intra = _dot(p, _block_diag_rows(v[r, :], H_RET))
        inter = _dot(q[r, :], states[c].astype(BF16))
        o_ref[r, :] = intra + inter * qdec_ref[r, :]

    @pl.when(g == pl.num_programs(1) - 1)
    def _():
        _store_diag_blocks(s_scr[...], sfin_ref, H_RET)


def _state_specs(states, layer, n_heads):
    blk = (n_heads, HEAD_DIM, HEAD_DIM)
    out_spec = pl.BlockSpec((None,) + blk, lambda b, g: (b, 0, 0, 0))
    if states is None:
        return [], [], out_spec
    in_spec = pl.BlockSpec((None, None) + blk, lambda b, g: (layer, b, 0, 0, 0))
    return [in_spec], [states], out_spec


def _retention(ret, states, layer, tabs, *, batch, seq_len, lc, n_chunks):
    rows = lc * n_chunks
    ng = seq_len // rows
    m = batch * seq_len
    col = lambda c: pl.BlockSpec((rows, W_RET), lambda b, g: (b * ng + g, c))
    full = lambda a: pl.BlockSpec(a.shape, lambda b, g: (0,) * a.ndim)
    s_in_spec, s_in, s_out_spec = _state_specs(states, layer, H_RET)
    kern = functools.partial(_retention_kernel, lc=lc, n_chunks=n_chunks,
                             has_init=states is not None)
    return pl.pallas_call(
        kern,
        grid=(batch, ng),
        in_specs=[col(0), col(1), col(2)] + s_in_spec + [full(t) for t in tabs],
        out_specs=[pl.BlockSpec((rows, W_RET), lambda b, g: (b * ng + g, 0)), s_out_spec],
        out_shape=[jax.ShapeDtypeStruct((m, W_RET), F32),
                   jax.ShapeDtypeStruct((batch, H_RET, HEAD_DIM, HEAD_DIM), F32)],
        scratch_shapes=[pltpu.VMEM((W_RET, W_RET), F32)],
        compiler_params=_params(2),
        name="retention",
    )(ret, ret, ret, *s_in, *tabs)


def _retention_tables(lc, n_chunks):
    lg = np.log1p(-np.exp2(-5.0 - np.arange(H_RET, dtype=np.float64)))
    idx = np.arange(lc)
    diff = idx[:, None] - idx[None, :]
    causal = diff >= 0
    dpos = np.where(causal, diff, 0).astype(np.float64)
    decay = np.where(causal[None], np.exp(dpos[None] * lg[:, None, None]), 0.0)
    dtab = np.transpose(decay, (1, 0, 2)).reshape(lc, H_RET * lc)
    q_decay = np.exp((idx + 1).astype(np.float64)[:, None] * lg[None, :])
    k_decay = np.exp((lc - 1 - idx).astype(np.float64)[:, None] * lg[None, :])
    qdec = np.tile(np.repeat(q_decay, HEAD_DIM, axis=1), (n_chunks, 1))
    kdec = np.tile(np.repeat(k_decay, HEAD_DIM, axis=1), (n_chunks, 1))
    gtab = np.broadcast_to(np.repeat(np.exp(lc * lg), HEAD_DIM)[:, None], (W_RET, W_RET))
    head = np.arange(W_RET) // HEAD_DIM
    bd = head[:, None] == head[None, :]
    return tuple(jnp.asarray(t, F32) for t in (dtab, qdec, kdec, gtab, bd))


def _split_bf16(x):
    hi = x.astype(BF16)
    lo = (x - hi.astype(F32)).astype(BF16)
    return jnp.concatenate([hi, lo], axis=1)


def _group_reference(b, row, half):
    n, w = b.shape
    group = 2 * half
    if group >= 8:
        b3 = b.reshape(n // group, group, w)
        return jnp.broadcast_to(b3[:, half - 1:half, :], b3.shape).reshape(n, w)
    if group == 4:
        i = row & 3
        return jnp.where(i == 0, pltpu.roll(b, n - 1, 0),
                         jnp.where(i == 1, b,
                                   jnp.where(i == 2, pltpu.roll(b, 1, 0), pltpu.roll(b, 2, 0))))
    return jnp.where((row & 1) == 1, pltpu.roll(b, 1, 0), b)


def _gla_kernel(*refs, lc, n_chunks, has_init):
    if has_init:
        (q_ref, k_ref, v_ref, lf_ref, s0_ref, tri_ref, bd_ref, onehot_ref,
         o_ref, sfin_ref, s_scr, sc_scr) = refs
    else:
        (q_ref, k_ref, v_ref, lf_ref, tri_ref, bd_ref, onehot_ref,
         o_ref, sfin_ref, s_scr, sc_scr) = refs
    g = pl.program_id(1)

    @pl.when(g == 0)
    def _():
        if has_init:
            s_scr[...] = _load_block_diag(s0_ref, H_GLA)
        else:
            s_scr[...] = jnp.zeros_like(s_scr)

    n_lev = lc.bit_length() - 1
    w = W_GLA
    n = lc * n_chunks
    chunk_rows = [slice(c * lc, (c + 1) * lc) for c in range(n_chunks)]
    row = lax.broadcasted_iota(jnp.int32, (n, w), 0)
    trow = lax.broadcasted_iota(jnp.int32, (lc, H_GLA * lc), 0)
    scol = lax.broadcasted_iota(jnp.int32, (lc, H_GLA * lc), 1) & (lc - 1)
    q = q_ref[...]
    k = k_ref[...]
    v = v_ref[...]
    q32 = q.astype(F32)
    k32 = k.astype(F32)
    tri = tri_ref[...]
    lfs = _split_bf16(lf_ref[...])
    cs = jnp.concatenate([_dot(tri, lfs[r, :]) for r in chunk_rows], axis=0)
    b = cs[:, :w] + cs[:, w:]

    qe = (q32 * jnp.exp(b)).astype(BF16)
    mild = jnp.min(b) >= -GLA_SAFE_LOG_DECAY

    @pl.when(mild)
    def _():
        ke = (k32 * jnp.exp(-b)).astype(BF16)
        for r in chunk_rows:
            sc = _dot_nt(qe[r, :], _block_diag_rows(ke[r, :], H_GLA))
            sc_scr[r, :] = jnp.where(trow >= scol, sc, 0.0)

    @pl.when(jnp.logical_not(mild))
    def _():
        for r in chunk_rows:
            sc = _dot_nt(q[r, :], _block_diag_rows(k[r, :], H_GLA))
            sc_scr[r, :] = jnp.where(trow == scol, sc, 0.0)
        for lev in range(n_lev):
            half = lc >> (lev + 1)
            shift = half.bit_length()
            upper = (row & half) != 0
            e = jnp.exp(-jnp.abs(b - _group_reference(b, row, half)))
            z = (jnp.where(upper, q32, k32) * e).astype(BF16)
            valid = (((trow >> shift) == (scol >> shift))
                     & ((trow & half) != 0) & ((scol & half) == 0))
            for r in chunk_rows:
                zc = z[r, :]
                sc = _dot_nt(zc, _block_diag_rows(zc, H_GLA))
                sc_scr[r, :] = jnp.where(valid, sc, sc_scr[r, :])

    b3 = b.reshape(n_chunks, lc, w)
    b_last = b3[:, lc - 1:lc, :]
    kd = (k32 * jnp.exp(jnp.broadcast_to(b_last, b3.shape).reshape(n, w) - b)).astype(BF16)
    col_sum = _dot_tn(lfs, onehot_ref[...])
    decay_col = jnp.exp(col_sum[:w, :] + col_sum[w:, :])
    bd = bd_ref[...]
    states = [s_scr[...]]
    for c, r in enumerate(chunk_rows):
        upd = _dot_tn(kd[r, :], v[r, :])
        states.append(states[-1] * decay_col[:, c:c + 1] + upd * bd)
    s_scr[...] = states[-1]
    for c, r in enumerate(chunk_rows):
        inter = _dot(qe[r, :], states[c].astype(BF16))
        intra = _dot(sc_scr[r, :].astype(BF16), _block_diag_rows(v[r, :], H_GLA))
        o_ref[r, :] = intra + inter

    @pl.when(g == pl.num_programs(1) - 1)
    def _():
        _store_diag_blocks(s_scr[...], sfin_ref, H_GLA)


def _gla(gla, lf, states, layer, tabs, *, batch, seq_len, lc, n_chunks):
    rows = lc * n_chunks
    ng = seq_len // rows
    m = batch * seq_len
    col = lambda c: pl.BlockSpec((rows, W_GLA), lambda b, g: (b * ng + g, c))
    full = lambda a: pl.BlockSpec(a.shape, lambda b, g: (0,) * a.ndim)
    s_in_spec, s_in, s_out_spec = _state_specs(states, layer, H_GLA)
    kern = functools.partial(_gla_kernel, lc=lc, n_chunks=n_chunks, has_init=states is not None)
    return pl.pallas_call(
        kern,
        grid=(batch, ng),
        in_specs=[col(0), col(1), col(2), col(0)] + s_in_spec + [full(t) for t in tabs],
        out_specs=[pl.BlockSpec((rows, W_GLA), lambda b, g: (b * ng + g, 0)), s_out_spec],
        out_shape=[jax.ShapeDtypeStruct((m, W_GLA), F32),
                   jax.ShapeDtypeStruct((batch, H_GLA, HEAD_DIM, HEAD_DIM), F32)],
        scratch_shapes=[pltpu.VMEM((W_GLA, W_GLA), F32),
                        pltpu.VMEM((rows, H_GLA * lc), F32)],
        compiler_params=_params(2),
        name="gla",
    )(gla, gla, gla, lf, *s_in, *tabs)


def _gla_tables(lc, n_chunks):
    idx = np.arange(lc)
    tri = idx[:, None] >= idx[None, :]
    head = np.arange(W_GLA) // HEAD_DIM
    bd = head[:, None] == head[None, :]
    chunk_of_row = np.arange(lc * n_chunks) // lc
    onehot = chunk_of_row[:, None] == np.arange(LANES)[None, :]
    return jnp.asarray(tri, BF16), jnp.asarray(bd, F32), jnp.asarray(onehot, BF16)


def _band_attn_kernel(q_ref, kc_ref, vc_ref, gv_ref, hones_ref, o_ref,
                      kwin_t, vwin, bias_scr, s_scr, m_scr, ksq_prev, *, lc, rpi, n_iter):
    b_id = pl.program_id(0)
    g = pl.program_id(1)
    win = ATT_REACH + rpi
    rows = rpi * n_iter

    @pl.when((b_id == 0) & (g == 0))
    def _():
        ksq_prev[0] = 0.0
        row = lax.broadcasted_iota(jnp.int32, (rpi, BIAS_W), 0)
        rel = lax.broadcasted_iota(jnp.int32, (rpi, BIAS_W), 1) - (row // lc) * lc
        in_band = (rel >= 0) & (rel < ATT_REACH + lc)
        for h in range(H_ATT):
            x = jnp.broadcast_to(gv_ref[h:h + 1, :], (rpi, BIAS_W))
            x = pltpu.roll(x, BIAS_W - (lc - 1), 1, stride=1, stride_axis=0)
            bias_scr[h // 2, (h % 2) * rpi:(h % 2 + 1) * rpi, :] = jnp.where(
                in_band, x * LOG2E, NEG_BIG)

    @pl.when(g == 0)
    def _():
        kwin_t[:, 0:ATT_REACH] = jnp.zeros((W_ATT, ATT_REACH), BF16)
        vwin[0:ATT_REACH, :] = jnp.zeros((ATT_REACH, W_ATT), BF16)

    @pl.when(g != 0)
    def _():
        kwin_t[:, 0:ATT_REACH] = kwin_t[:, ATT_REACH:ATT_REACH + rows]
        vwin[0:ATT_REACH, :] = vwin[ATT_REACH:ATT_REACH + rows, :]

    k_cur = kc_ref[...].astype(F32)
    kwin_t[:, ATT_REACH:ATT_REACH + rows] = k_cur.T.astype(BF16)
    vwin[ATT_REACH:ATT_REACH + rows, :] = vc_ref[...]

    head_ones = hones_ref[...]
    q_cur = q_ref[...].astype(F32)
    q_sq = jnp.max(_dot((q_cur * q_cur).astype(BF16), head_ones))
    k_sq_cur = jnp.max(_dot((k_cur * k_cur).astype(BF16), head_ones))
    k_sq = jnp.maximum(k_sq_cur, jnp.where(g == 0, 0.0, ksq_prev[0]))
    ksq_prev[0] = k_sq_cur
    room = ATT_SAFE_LOG2 - jnp.max(jnp.abs(gv_ref[...])) * LOG2E
    bounded = (room > 0.0) & (q_sq * k_sq * NORM_ROUNDING_MARGIN <= room * room)

    lane = lax.broadcasted_iota(jnp.int32, (rpi, LANES), 1)
    lane2 = lax.broadcasted_iota(jnp.int32, (2 * rpi, LANES), 1)
    head2 = lax.broadcasted_iota(jnp.int32, (2 * rpi, LANES), 0) // rpi
    col = lax.broadcasted_iota(jnp.int32, (2 * rpi, win), 1)

    def step(i, carry, *, masked, small):
        r0 = pl.multiple_of(i * rpi, rpi)
        first_valid = ATT_REACH - (g * rows + r0)

        def scores(pair):
            lanes = slice(pair * LANES, (pair + 1) * LANES)
            qp = q_ref[pl.ds(r0, rpi), lanes]
            q2 = jnp.concatenate([qp, qp], axis=0)
            q2 = jnp.where((lane2 // HEAD_DIM) == head2, q2, jnp.zeros_like(q2))
            s = _dot(q2, kwin_t[lanes, pl.ds(r0, win)]) + bias_scr[pair][:, :win]
            if masked:
                s = jnp.where(col >= first_valid, s, NEG_BIG)
            return s

        def finish(pair, e):
            lanes = slice(pair * LANES, (pair + 1) * LANES)
            den = jnp.sum(e, axis=-1, keepdims=True)
            pv = _dot(e.astype(BF16), vwin[pl.ds(r0, win), lanes]) / den
            o_ref[pl.ds(r0, rpi), lanes] = jnp.where(
                lane < HEAD_DIM, pv[:rpi, :], pv[rpi:, :]).astype(BF16)

        if small:
            for pair in range(H_ATT // 2):
                finish(pair, jnp.exp2(scores(pair)))
        else:
            for pair in range(H_ATT // 2):
                s = scores(pair)
                s_scr[pair, :, :win] = s
                m_scr[pair] = jnp.broadcast_to(
                    jnp.max(s, axis=-1, keepdims=True), (2 * rpi, LANES))
            for pair in range(H_ATT // 2):
                finish(pair, jnp.exp2(s_scr[pair, :, :win] - m_scr[pair][:, :1]))
        return carry

    def run(masked):
        @pl.when(bounded)
        def _():
            lax.fori_loop(0, n_iter, functools.partial(step, masked=masked, small=True), 0)

        @pl.when(jnp.logical_not(bounded))
        def _():
            lax.fori_loop(0, n_iter, functools.partial(step, masked=masked, small=False), 0)

    @pl.when(g == 0)
    def _():
        run(True)

    @pl.when(g != 0)
    def _():
        run(False)


def _band_attn(att, gv, layer, *, batch, seq_len, lc, rpi, n_iter):
    rows = rpi * n_iter
    assert rows == ATT_REACH, "the carried window is exactly one row group"
    ng = seq_len // rows
    m = batch * seq_len
    cur = lambda c: pl.BlockSpec((rows, W_ATT), lambda b, g: (b * ng + g, c))
    kern = functools.partial(_band_attn_kernel, lc=lc, rpi=rpi, n_iter=n_iter)
    head_ones = _head_ones()
    return pl.pallas_call(
        kern,
        grid=(batch, ng),
        in_specs=[cur(0), cur(1), cur(2), _layer_spec(gv, layer),
                  pl.BlockSpec(head_ones.shape, lambda b, g: (0, 0))],
        out_specs=pl.BlockSpec((rows, W_ATT), lambda b, g: (b * ng + g, 0)),
        out_shape=jax.ShapeDtypeStruct((m, W_ATT), BF16),
        scratch_shapes=[pltpu.VMEM((W_ATT, ATT_REACH + rows), BF16),
                        pltpu.VMEM((ATT_REACH + rows, W_ATT), BF16),
                        pltpu.VMEM((H_ATT // 2, 2 * rpi, BIAS_W), F32),
                        pltpu.VMEM((H_ATT // 2, 2 * rpi, BIAS_W), F32),
                        pltpu.VMEM((H_ATT // 2, 2 * rpi, LANES), F32),
                        pltpu.SMEM((1,), F32)],
        compiler_params=_params(2),
        name="band_attn",
    )(att, att, att, gv, head_ones)


def _sample_attn_kernel(q_ref, kn_ref, vn_ref, kt_ref, vt_ref, gv_ref, hones_ref, o_ref, *, ls):
    win = ATT_REACH + ls
    lane = lax.broadcasted_iota(jnp.int32, (ls, LANES), 1)
    lane2 = lax.broadcasted_iota(jnp.int32, (2 * ls, LANES), 1)
    head2 = lax.broadcasted_iota(jnp.int32, (2 * ls, LANES), 0) // ls

    head_ones = hones_ref[...]
    q32 = q_ref[...].astype(F32)
    kn32 = kn_ref[...].astype(F32)
    kc32 = kt_ref[...]
    q_sq = jnp.max(_dot((q32 * q32).astype(BF16), head_ones))
    k_sq = jnp.maximum(jnp.max(_dot((kn32 * kn32).astype(BF16), head_ones)),
                       jnp.max(jnp.sum(kc32 * kc32, axis=1)))
    room = ATT_SAFE_LOG2 - jnp.max(jnp.abs(gv_ref[...])) * LOG2E
    bounded = (room > 0.0) & (q_sq * k_sq * NORM_ROUNDING_MARGIN <= room * room)

    def head_pair(pair, small):
        lanes = slice(pair * LANES, (pair + 1) * LANES)
        qp = q_ref[:, lanes]
        q2 = jnp.concatenate([qp, qp], axis=0)
        q2 = jnp.where((lane2 // HEAD_DIM) == head2, q2, jnp.zeros_like(q2))
        kt = kt_ref[2 * pair:2 * pair + 2].reshape(2 * HEAD_DIM, ATT_REACH).astype(BF16)
        vt = vt_ref[2 * pair:2 * pair + 2].reshape(2 * HEAD_DIM, ATT_REACH).astype(BF16)
        bias = jnp.concatenate(
            [pltpu.roll(jnp.broadcast_to(gv_ref[h:h + 1, :], (ls, BIAS_W)),
                        BIAS_W - (ls - 1), 1, stride=1, stride_axis=0)
             for h in (2 * pair, 2 * pair + 1)], axis=0) * LOG2E
        s_old = _dot(q2, kt) + bias[:, :ATT_REACH]
        s_new = _dot_nt(q2, kn_ref[:, lanes]) + bias[:, ATT_REACH:win]
        if not small:
            mx = jnp.maximum(jnp.max(s_old, axis=-1, keepdims=True),
                             jnp.max(s_new, axis=-1, keepdims=True))
            s_old = s_old - mx
            s_new = s_new - mx
        e_old = jnp.exp2(s_old)
        e_new = jnp.exp2(s_new)
        den = jnp.sum(e_old, axis=-1, keepdims=True) + jnp.sum(e_new, axis=-1, keepdims=True)
        pv = (_dot_nt(e_old.astype(BF16), vt) + _dot(e_new.astype(BF16), vn_ref[:, lanes])) / den
        o_ref[:, lanes] = jnp.where(lane < HEAD_DIM, pv[:ls, :], pv[ls:, :]).astype(BF16)

    @pl.when(bounded)
    def _():
        for pair in range(H_ATT // 2):
            head_pair(pair, True)

    @pl.when(jnp.logical_not(bounded))
    def _():
        for pair in range(H_ATT // 2):
            head_pair(pair, False)


def _head_ones():
    lane_head = np.arange(W_ATT) // HEAD_DIM
    return jnp.asarray(lane_head[:, None] == np.arange(LANES)[None, :], BF16)


def _sample_attn(att, cache_kt, cache_vt, layer, gv, *, batch, ls):
    new = lambda c: pl.BlockSpec((ls, W_ATT), lambda b: (b, c))
    old = pl.BlockSpec((None, None, H_ATT, HEAD_DIM, ATT_REACH), lambda b: (layer, b, 0, 0, 0))
    head_ones = _head_ones()
    kern = functools.partial(_sample_attn_kernel, ls=ls)
    return pl.pallas_call(
        kern,
        grid=(batch,),
        in_specs=[new(0), new(1), new(2), old, old, _layer_spec(gv, layer),
                  pl.BlockSpec(head_ones.shape, lambda b: (0, 0))],
        out_specs=pl.BlockSpec((ls, W_ATT), lambda b: (b, 0)),
        out_shape=jax.ShapeDtypeStruct((batch * ls, W_ATT), BF16),
        compiler_params=_params(1),
        name="sample_attn",
    )(att, att, att, cache_kt, cache_vt, gv, head_ones)


def _bias_rows(rel_bias_l, lc):
    wv = np.arange(BIAS_W)
    dist = ATT_REACH + lc - 1 - wv
    ridx = np.clip(dist, -(CHUNK - 1), REL_MAX) + (CHUNK - 1)
    return rel_bias_l[..., ridx]


def _out_ffn_kernel(x_ref, oa_ref, ob_ref, oc_ref, sga_ref, sgb_ref, gnw_ref, glw_ref,
                    n2w_ref, fnw_ref, ones_ref, wout_ref, wup_ref, wdown_ref, y_ref,
                    *, final_norm):
    ones = ones_ref[...]

    def head_sum(t):
        return _dot(t.astype(BF16), ones)

    inv_d = 1.0 / HEAD_DIM
    oa = oa_ref[...]
    mu = head_sum(oa) * inv_d
    da = oa - mu
    var = head_sum(da * da) * inv_d
    a = da * lax.rsqrt(var + EPS) * gnw_ref[...] * sga_ref[...].astype(F32)
    ob = ob_ref[...]
    ms = head_sum(ob * ob) * inv_d
    bb = ob * lax.rsqrt(ms + EPS) * glw_ref[...] * sgb_ref[...].astype(F32)
    cat = jnp.concatenate([a.astype(BF16), bb.astype(BF16), oc_ref[...]], axis=1)
    h1 = x_ref[...] + _dot(cat, wout_ref[...])
    hn = _rmsnorm(h1, n2w_ref[...]).astype(BF16)
    acc = jnp.zeros_like(h1)
    n_slab = D_FF // D_MODEL
    for j in range(n_slab):
        sl = slice(j * D_MODEL, (j + 1) * D_MODEL)
        u = jnp.maximum(_dot(hn, wup_ref[:, sl]), 0.0)
        acc = acc + _dot((u * u).astype(BF16), wdown_ref[sl, :])
    h2 = h1 + acc
    if final_norm:
        h2 = _rmsnorm(h2, fnw_ref[...])
    y_ref[...] = h2


def _out_ffn(x2, oa, ob, oc, ret, gla, gnw, glw, n2w, fnw, ones, wout, wup, wdown,
             *, layer, tm, final_norm):
    m = x2.shape[0]
    full = lambda a: pl.BlockSpec(a.shape, lambda i: (0,) * a.ndim)
    at_layer = functools.partial(_layer_spec, layer=layer)
    row = lambda w, c=0: pl.BlockSpec((tm, w), lambda i: (i, c))
    kern = functools.partial(_out_ffn_kernel, final_norm=final_norm)
    return pl.pallas_call(
        kern,
        grid=(m // tm,),
        in_specs=[row(D_MODEL), row(W_RET), row(W_GLA), row(W_ATT), row(W_RET, 3),
                  row(W_GLA, 3), at_layer(gnw), at_layer(glw), at_layer(n2w), full(fnw),
                  full(ones), at_layer(wout), at_layer(wup), at_layer(wdown)],
        out_specs=row(D_MODEL),
        out_shape=jax.ShapeDtypeStruct((m, D_MODEL), F32),
        compiler_params=_params(1),
        name="out_ffn",
    )(x2, oa, ob, oc, ret, gla, gnw, glw, n2w, fnw, ones, wout, wup, wdown)


def _rotary_tables(pos, rows):
    half = HEAD_DIM // 2
    inv_freq = ROPE_BASE ** (-np.arange(half, dtype=np.float64) / half)
    ang = np.asarray(pos, np.float64)[:, None] * inv_freq[None, :]
    reps = (rows // len(pos), LANES // half)
    return (jnp.asarray(np.tile(np.cos(ang), reps), F32),
            jnp.asarray(np.tile(np.sin(ang), reps), F32))


def _run_path(x2, p, *, batch, seq_len, lc, n_chunks, tm, pos, layer, states, caches,
              final_norm):
    cos_t, sin_t = _rotary_tables(pos, max(seq_len, tm))
    ret, gla, lf, att, kc32, vc32 = _in_proj(
        x2, p["n1w"], p["w_in"], p["w_a2"], p["ba"], cos_t, sin_t,
        layer=layer, tm=tm, seq_len=seq_len)
    s_ret0, s_gla0 = states if states is not None else (None, None)
    oa, s_ret = _retention(ret, s_ret0, layer, _retention_tables(lc, n_chunks),
                           batch=batch, seq_len=seq_len, lc=lc, n_chunks=n_chunks)
    ob, s_gla = _gla(gla, lf, s_gla0, layer, _gla_tables(lc, n_chunks),
                     batch=batch, seq_len=seq_len, lc=lc, n_chunks=n_chunks)
    gv = _bias_rows(p["rel_bias"], lc)
    if caches is None:
        rpi = min(lc * n_chunks, ATT_ROWS_PER_STEP)
        oc = _band_attn(att, gv, layer, batch=batch, seq_len=seq_len, lc=lc, rpi=rpi,
                        n_iter=lc * n_chunks // rpi)
    else:
        oc = _sample_attn(att, caches[0], caches[1], layer, gv, batch=batch, ls=seq_len)
    y = _out_ffn(x2, oa, ob, oc, ret, gla, p["gnw"], p["glw"], p["n2w"], p["fnw"], p["ones"],
                 p["wout"], p["wup"], p["wdown"], layer=layer, tm=tm, final_norm=final_norm)
    return y, s_ret, s_gla, kc32, vc32


def kernel(x_prompt, x_sample, state_ret, state_gla, cache_attn_k, cache_attn_v, norm1_w, norm2_w, final_norm_w, w_in, w_gla_a2, b_gla_a, ret_gn_w, gla_norm_w, rel_bias, w_out, w_up, w_down):
    depth = w_in.shape[0]
    bp, lp, _ = x_prompt.shape
    bs, ls, _ = x_sample.shape
    keep = min(ATT_REACH, lp)
    head = np.arange(W_RET) // HEAD_DIM
    ones = jnp.asarray(head[:, None] == head[None, :], BF16)
    hp = x_prompt.reshape(bp * lp, D_MODEL)
    hs = x_sample.reshape(bs * ls, D_MODEL)
    pos_p = np.arange(lp)
    pos_s = PAST_LEN + np.arange(ls)
    tm_p = keep
    to_head_major = lambda c: jnp.transpose(c, (0, 1, 3, 4, 2))
    caches = (to_head_major(cache_attn_k.astype(F32)), to_head_major(cache_attn_v.astype(F32)))
    from_head_major = lambda t: jnp.transpose(
        t.reshape(bp, H_ATT, HEAD_DIM, keep), (0, 3, 1, 2))
    states = (state_ret.astype(F32), state_gla.astype(F32))
    outs = {k: [] for k in ("ret_p", "gla_p", "kp", "vp", "ret_s", "gla_s", "ks", "vs")}
    rows3 = lambda v: v.astype(F32).reshape(depth, 1, v.shape[-1])
    params = dict(
        n1w=rows3(norm1_w), n2w=rows3(norm2_w), fnw=final_norm_w.astype(F32)[None, :],
        gnw=rows3(ret_gn_w), glw=rows3(gla_norm_w), ba=rows3(b_gla_a),
        w_in=jnp.swapaxes(w_in.astype(F32), 1, 2), w_a2=w_gla_a2.astype(F32),
        rel_bias=rel_bias.astype(F32),
        wout=w_out.astype(BF16), wup=w_up.astype(BF16), wdown=w_down.astype(BF16), ones=ones)
    for l in range(depth):
        last = l == depth - 1
        hp, s_ret, s_gla, kt32, vt32 = _run_path(
            hp, params, batch=bp, seq_len=lp, lc=CHUNK, n_chunks=ATT_REACH // CHUNK, tm=tm_p,
            pos=pos_p, layer=l, states=None, caches=None, final_norm=last)
        outs["ret_p"].append(s_ret)
        outs["gla_p"].append(s_gla)
        outs["kp"].append(from_head_major(kt32))
        outs["vp"].append(from_head_major(vt32))
        hs, s_ret, s_gla, kc32, vc32 = _run_path(
            hs, params, batch=bs, seq_len=ls, lc=ls, n_chunks=1, tm=bs * ls,
            pos=pos_s, layer=l, states=states, caches=caches, final_norm=last)
        outs["ret_s"].append(s_ret)
        outs["gla_s"].append(s_gla)
        outs["ks"].append(kc32.reshape(bs, ls, H_ATT, HEAD_DIM))
        outs["vs"].append(vc32.reshape(bs, ls, H_ATT, HEAD_DIM))
    y_prompt = hp.reshape(bp, lp, D_MODEL)
    y_sample = hs.reshape(bs, ls, D_MODEL)
    st = lambda k: jnp.stack(outs[k])
    return (y_prompt, y_sample, st("ret_p"), st("gla_p"), st("kp"), st("vp"),
            st("ret_s"), st("gla_s"), st("ks"), st("vs"))
```

```python
import functools

import numpy as np
import jax
import jax.numpy as jnp
from jax import lax
from jax.experimental import pallas as pl
from jax.experimental.pallas import tpu as pltpu

D_MODEL = 1024
CHUNK = 64
HEAD_DIM = 64
H_RET = 4
H_GLA = 4
H_ATT = 8
W_RET = H_RET * HEAD_DIM
W_GLA = H_GLA * HEAD_DIM
W_ATT = H_ATT * HEAD_DIM
GLA_RANK = 16
GLA_TAU = 16.0
ATT_REACH = 512
REL_MAX = 256
N_REL = CHUNK + REL_MAX
D_FF = 4 * D_MODEL
IN_COLS = 4 * W_RET + 4 * W_GLA + GLA_RANK + 3 * W_ATT
ROPE_BASE = 10000.0
EPS = 1e-6
PAST_LEN = 4096

LANES = 128
LOW_PAD = LANES
BIAS_W = 640
GLA_SAFE_LOG_DECAY = 50.0
ATT_SAFE_LOG2 = 80.0
NORM_ROUNDING_MARGIN = 1.05
ATT_ROWS_PER_STEP = 128
LOG2E = 1.4426950408889634
NEG_BIG = -1e30
VMEM_LIMIT = 56 * 1024 * 1024

F32 = jnp.float32
BF16 = jnp.bfloat16

_NT = (((1,), (1,)), ((), ()))
_TN = (((0,), (0,)), ((), ()))


def _dot(a, b):
    return jnp.dot(a, b, preferred_element_type=F32)


def _dot_nt(a, b):
    return lax.dot_general(a, b, _NT, preferred_element_type=F32)


def _dot_tn(a, b):
    return lax.dot_general(a, b, _TN, preferred_element_type=F32)


def _rmsnorm(x, w):
    return x * lax.rsqrt(jnp.mean(x * x, axis=-1, keepdims=True) + EPS) * w


def _params(n_grid_dims):
    return pltpu.CompilerParams(
        dimension_semantics=("arbitrary",) * n_grid_dims,
        vmem_limit_bytes=VMEM_LIMIT,
    )


def _layer_spec(a, layer):
    index = (layer,) + (0,) * (a.ndim - 1)
    return pl.BlockSpec((None,) + a.shape[1:], lambda *_: index, pipeline_mode=pl.Buffered(1))


def _block_diag_rows(x, n_heads):
    lane_head = lax.broadcasted_iota(jnp.int32, x.shape, 1) // HEAD_DIM
    zero = jnp.zeros_like(x)
    return jnp.concatenate(
        [jnp.where(lane_head == h, x, zero) for h in range(n_heads)], axis=0)


def _in_proj_kernel(*refs, tiles_per_seq, n_prev):
    (x_ref, n1w_ref, win_ref, wa2in_ref, ba_ref, cos_ref, sin_ref) = refs[:7]
    kprev_ref, vprev_ref = refs[7:9] if n_prev else (None, None)
    (ret_ref, gla_ref, lf_ref, att_ref, kc_ref, vc_ref,
     wret_ref, wgla_ref, watt_ref, wa2_ref) = refs[7 + (2 if n_prev else 0):]

    @pl.when(pl.program_id(0) == 0)
    def _():
        n_main = 4 * W_RET + 4 * W_GLA
        wret_ref[...] = win_ref[0:4 * W_RET, :].astype(BF16)
        wgla_ref[0:4 * W_GLA, :] = win_ref[4 * W_RET:n_main, :].astype(BF16)
        wgla_ref[4 * W_GLA:4 * W_GLA + GLA_RANK, :] = (
            win_ref[n_main:n_main + GLA_RANK, :].astype(BF16))
        wgla_ref[4 * W_GLA + GLA_RANK:, :] = jnp.zeros((LOW_PAD - GLA_RANK, D_MODEL), BF16)
        watt_ref[...] = win_ref[n_main + GLA_RANK:IN_COLS, :].astype(BF16)
        wa2_ref[0:GLA_RANK, :] = wa2in_ref[...].astype(BF16)
        wa2_ref[GLA_RANK:, :] = jnp.zeros((LOW_PAD - GLA_RANK, W_GLA), BF16)

    hn = _rmsnorm(x_ref[...], n1w_ref[...]).astype(BF16)
    cos = cos_ref[...]
    sin = sin_ref[...]
    first_half = (lax.broadcasted_iota(jnp.int32, cos.shape, 1) % HEAD_DIM) < (HEAD_DIM // 2)

    def rotary(t):
        swapped = jnp.where(first_half, -pltpu.roll(t, LANES - HEAD_DIM // 2, 1),
                            pltpu.roll(t, HEAD_DIM // 2, 1))
        return t * cos + swapped * sin

    scale = HEAD_DIM ** -0.5

    acc = _dot_nt(hn, wret_ref[...])
    for j in range(W_RET // LANES):
        sl = slice(j * LANES, (j + 1) * LANES)
        ret_ref[:, sl] = rotary(acc[:, sl]).astype(BF16)
        sk = slice(W_RET + j * LANES, W_RET + (j + 1) * LANES)
        ret_ref[:, sk] = (rotary(acc[:, sk]) * scale).astype(BF16)
    ret_ref[:, 2 * W_RET:3 * W_RET] = acc[:, 2 * W_RET:3 * W_RET].astype(BF16)
    g = acc[:, 3 * W_RET:4 * W_RET]
    ret_ref[:, 3 * W_RET:4 * W_RET] = (g * jax.nn.sigmoid(g)).astype(BF16)

    acc = _dot_nt(hn, wgla_ref[...])
    gla_ref[:, 0:W_GLA] = (acc[:, 0:W_GLA] * scale).astype(BF16)
    gla_ref[:, W_GLA:3 * W_GLA] = acc[:, W_GLA:3 * W_GLA].astype(BF16)
    g = acc[:, 3 * W_GLA:4 * W_GLA]
    gla_ref[:, 3 * W_GLA:4 * W_GLA] = (g * jax.nn.sigmoid(g)).astype(BF16)
    low = acc[:, 4 * W_GLA:4 * W_GLA + LOW_PAD].astype(BF16)
    z = _dot(low, wa2_ref[...]) + ba_ref[...]
    log_sig = jnp.minimum(z, 0.0) - jnp.log(1.0 + jnp.exp(-jnp.abs(z)))
    lf_ref[...] = log_sig * (1.0 / GLA_TAU)

    acc = _dot_nt(hn, watt_ref[...])
    att_ref[:, 0:W_ATT] = (acc[:, 0:W_ATT] * (scale * LOG2E)).astype(BF16)
    att_ref[:, W_ATT:3 * W_ATT] = acc[:, W_ATT:3 * W_ATT].astype(BF16)
    if tiles_per_seq == 0:
        kc_ref[...] = acc[:, W_ATT:2 * W_ATT]
        vc_ref[...] = acc[:, 2 * W_ATT:3 * W_ATT]
    else:
        @pl.when(pl.program_id(0) % tiles_per_seq == tiles_per_seq - 1)
        def _():
            if n_prev:
                kc_ref[0:n_prev] = kprev_ref[...]
                vc_ref[0:n_prev] = vprev_ref[...]
            kc_ref[n_prev] = acc[:, W_ATT:2 * W_ATT].T
            vc_ref[n_prev] = acc[:, 2 * W_ATT:3 * W_ATT].T


def _in_proj(x2, n1w, w_in, w_a2, ba, cos_t, sin_t, prev_kv, *, layer, tm, seq_len):
    m = x2.shape[0]
    n_tiles = m // tm
    tab_blocks = cos_t.shape[0] // tm
    tiles_per_seq = seq_len // tm
    at_layer = functools.partial(_layer_spec, layer=layer)
    row = lambda w: pl.BlockSpec((tm, w), lambda i: (i, 0))
    tab = pl.BlockSpec((tm, LANES), lambda i: (i % tab_blocks, 0))
    n_prev = 0 if prev_kv is None else prev_kv[0].shape[0]
    prev_specs = []
    if tiles_per_seq == 0:
        cache = row(W_ATT)
        cache_shape = jax.ShapeDtypeStruct((m, W_ATT), F32)
    else:
        stacked = lambda n: pl.BlockSpec((n, None, W_ATT, tm),
                                         lambda i: (0, i // tiles_per_seq, 0, 0))
        cache = stacked(n_prev + 1)
        cache_shape = jax.ShapeDtypeStruct((n_prev + 1, m // seq_len, W_ATT, tm), F32)
        prev_specs = [stacked(n_prev)] * 2 if n_prev else []
    kern = functools.partial(_in_proj_kernel, tiles_per_seq=tiles_per_seq, n_prev=n_prev)
    return pl.pallas_call(
        kern,
        grid=(n_tiles,),
        in_specs=[row(D_MODEL), at_layer(n1w), at_layer(w_in), at_layer(w_a2), at_layer(ba),
                  tab, tab] + prev_specs,
        out_specs=[row(4 * W_RET), row(4 * W_GLA), row(W_GLA), row(3 * W_ATT), cache, cache],
        out_shape=[
            jax.ShapeDtypeStruct((m, 4 * W_RET), BF16),
            jax.ShapeDtypeStruct((m, 4 * W_GLA), BF16),
            jax.ShapeDtypeStruct((m, W_GLA), F32),
            jax.ShapeDtypeStruct((m, 3 * W_ATT), BF16),
            cache_shape,
            cache_shape,
        ],
        scratch_shapes=[pltpu.VMEM((4 * W_RET, D_MODEL), BF16),
                        pltpu.VMEM((4 * W_GLA + LOW_PAD, D_MODEL), BF16),
                        pltpu.VMEM((3 * W_ATT, D_MODEL), BF16),
                        pltpu.VMEM((LOW_PAD, W_GLA), BF16)],
        compiler_params=_params(1),
        name="in_proj",
    )(x2, n1w, w_in, w_a2, ba, cos_t, sin_t, *(prev_kv if n_prev else ()))


def _load_block_diag(s0_ref, n_heads):
    zero = jnp.zeros((HEAD_DIM, HEAD_DIM), F32)
    return jnp.concatenate(
        [jnp.concatenate([s0_ref[i] if j == i else zero for j in range(n_heads)], axis=1)
         for i in range(n_heads)], axis=0)


def _store_diag_blocks(s, out_ref, n_heads):
    for i in range(n_heads):
        sl = slice(i * HEAD_DIM, (i + 1) * HEAD_DIM)
        out_ref[i] = s[sl, sl]


def _retention_kernel(*refs, lc, n_chunks, has_init):
    if has_init:
        (q_ref, k_ref, v_ref, s0_ref, dtab_ref, qdec_ref, kdec_ref, gtab_ref, bd_ref,
         o_ref, sfin_ref, s_scr) = refs
    else:
        (q_ref, k_ref, v_ref, dtab_ref, qdec_ref, kdec_ref, gtab_ref, bd_ref,
         o_ref, sfin_ref, s_scr) = refs
    g = pl.program_id(1)

    @pl.when(g == 0)
    def _():
        if has_init:
            s_scr[...] = _load_block_diag(s0_ref, H_RET)
        else:
            s_scr[...] = jnp.zeros_like(s_scr)

    chunk_rows = [slice(c * lc, (c + 1) * lc) for c in range(n_chunks)]
    dtab = dtab_ref[...]
    q = q_ref[...]
    k = k_ref[...]
    v = v_ref[...]
    kd = (k.astype(F32) * kdec_ref[...]).astype(BF16)
    gtab = gtab_ref[...]
    bd = bd_ref[...]
    states = [s_scr[...]]
    for r in chunk_rows:
        upd = _dot_tn(kd[r, :], v[r, :])
        states.append(states[-1] * gtab + upd * bd)
    s_scr[...] = states[-1]
    for c, r in enumerate(chunk_rows):
        scores = _dot_nt(q[r, :], _block_diag_rows(k[r, :], H_RET))
        p = (scores * dtab).astype(BF16)
        intra = _dot(p, _block_diag_rows(v[r, :], H_RET))
        inter = _dot(q[r, :], states[c].astype(BF16))
        o_ref[r, :] = intra + inter * qdec_ref[r, :]

    @pl.when(g == pl.num_programs(1) - 1)
    def _():
        _store_diag_blocks(s_scr[...], sfin_ref, H_RET)


def _state_specs(states, layer, n_heads):
    blk = (n_heads, HEAD_DIM, HEAD_DIM)
    out_spec = pl.BlockSpec((None,) + blk, lambda b, g: (b, 0, 0, 0))
    if states is None:
        return [], [], out_spec
    in_spec = pl.BlockSpec((None, None) + blk, lambda b, g: (layer, b, 0, 0, 0))
    return [in_spec], [states], out_spec


def _retention(ret, states, layer, tabs, *, batch, seq_len, lc, n_chunks):
    rows = lc * n_chunks
    ng = seq_len // rows
    m = batch * seq_len
    col = lambda c: pl.BlockSpec((rows, W_RET), lambda b, g: (b * ng + g, c))
    full = lambda a: pl.BlockSpec(a.shape, lambda b, g: (0,) * a.ndim)
    s_in_spec, s_in, s_out_spec = _state_specs(states, layer, H_RET)
    kern = functools.partial(_retention_kernel, lc=lc, n_chunks=n_chunks,
                             has_init=states is not None)
    return pl.pallas_call(
        kern,
        grid=(batch, ng),
        in_specs=[col(0), col(1), col(2)] + s_in_spec + [full(t) for t in tabs],
        out_specs=[pl.BlockSpec((rows, W_RET), lambda b, g: (b * ng + g, 0)), s_out_spec],
        out_shape=[jax.ShapeDtypeStruct((m, W_RET), F32),
                   jax.ShapeDtypeStruct((batch, H_RET, HEAD_DIM, HEAD_DIM), F32)],
        scratch_shapes=[pltpu.VMEM((W_RET, W_RET), F32)],
        compiler_params=_params(2),
        name="retention",
    )(ret, ret, ret, *s_in, *tabs)


def _retention_tables(lc, n_chunks):
    lg = np.log1p(-np.exp2(-5.0 - np.arange(H_RET, dtype=np.float64)))
    idx = np.arange(lc)
    diff = idx[:, None] - idx[None, :]
    causal = diff >= 0
    dpos = np.where(causal, diff, 0).astype(np.float64)
    decay = np.where(causal[None], np.exp(dpos[None] * lg[:, None, None]), 0.0)
    dtab = np.transpose(decay, (1, 0, 2)).reshape(lc, H_RET * lc)
    q_decay = np.exp((idx + 1).astype(np.float64)[:, None] * lg[None, :])
    k_decay = np.exp((lc - 1 - idx).astype(np.float64)[:, None] * lg[None, :])
    qdec = np.tile(np.repeat(q_decay, HEAD_DIM, axis=1), (n_chunks, 1))
    kdec = np.tile(np.repeat(k_decay, HEAD_DIM, axis=1), (n_chunks, 1))
    gtab = np.broadcast_to(np.repeat(np.exp(lc * lg), HEAD_DIM)[:, None], (W_RET, W_RET))
    head = np.arange(W_RET) // HEAD_DIM
    bd = head[:, None] == head[None, :]
    return tuple(jnp.asarray(t, F32) for t in (dtab, qdec, kdec, gtab, bd))


def _split_bf16(x):
    hi = x.astype(BF16)
    lo = (x - hi.astype(F32)).astype(BF16)
    return jnp.concatenate([hi, lo], axis=1)


def _group_reference(b, row, half):
    n, w = b.shape
    group = 2 * half
    if group >= 8:
        b3 = b.reshape(n // group, group, w)
        return jnp.broadcast_to(b3[:, half - 1:half, :], b3.shape).reshape(n, w)
    if group == 4:
        i = row & 3
        return jnp.where(i == 0, pltpu.roll(b, n - 1, 0),
                         jnp.where(i == 1, b,
                                   jnp.where(i == 2, pltpu.roll(b, 1, 0), pltpu.roll(b, 2, 0))))
    return jnp.where((row & 1) == 1, pltpu.roll(b, 1, 0), b)


def _gla_kernel(*refs, lc, n_chunks, has_init):
    if has_init:
        (q_ref, k_ref, v_ref, lf_ref, s0_ref, tri_ref, bd_ref, onehot_ref,
         o_ref, sfin_ref, s_scr, sc_scr) = refs
    else:
        (q_ref, k_ref, v_ref, lf_ref, tri_ref, bd_ref, onehot_ref,
         o_ref, sfin_ref, s_scr, sc_scr) = refs
    g = pl.program_id(1)

    @pl.when(g == 0)
    def _():
        if has_init:
            s_scr[...] = _load_block_diag(s0_ref, H_GLA)
        else:
            s_scr[...] = jnp.zeros_like(s_scr)

    n_lev = lc.bit_length() - 1
    w = W_GLA
    n = lc * n_chunks
    chunk_rows = [slice(c * lc, (c + 1) * lc) for c in range(n_chunks)]
    row = lax.broadcasted_iota(jnp.int32, (n, w), 0)
    trow = lax.broadcasted_iota(jnp.int32, (lc, H_GLA * lc), 0)
    scol = lax.broadcasted_iota(jnp.int32, (lc, H_GLA * lc), 1) & (lc - 1)
    q = q_ref[...]
    k = k_ref[...]
    v = v_ref[...]
    q32 = q.astype(F32)
    k32 = k.astype(F32)
    tri = tri_ref[...]
    lfs = _split_bf16(lf_ref[...])
    cs = jnp.concatenate([_dot(tri, lfs[r, :]) for r in chunk_rows], axis=0)
    b = cs[:, :w] + cs[:, w:]

    qe = (q32 * jnp.exp(b)).astype(BF16)
    mild = jnp.min(b) >= -GLA_SAFE_LOG_DECAY

    @pl.when(mild)
    def _():
        ke = (k32 * jnp.exp(-b)).astype(BF16)
        for r in chunk_rows:
            sc = _dot_nt(qe[r, :], _block_diag_rows(ke[r, :], H_GLA))
            sc_scr[r, :] = jnp.where(trow >= scol, sc, 0.0)

    @pl.when(jnp.logical_not(mild))
    def _():
        for r in chunk_rows:
            sc = _dot_nt(q[r, :], _block_diag_rows(k[r, :], H_GLA))
            sc_scr[r, :] = jnp.where(trow == scol, sc, 0.0)
        for lev in range(n_lev):
            half = lc >> (lev + 1)
            shift = half.bit_length()
            upper = (row & half) != 0
            e = jnp.exp(-jnp.abs(b - _group_reference(b, row, half)))
            z = (jnp.where(upper, q32, k32) * e).astype(BF16)
            valid = (((trow >> shift) == (scol >> shift))
                     & ((trow & half) != 0) & ((scol & half) == 0))
            for r in chunk_rows:
                zc = z[r, :]
                sc = _dot_nt(zc, _block_diag_rows(zc, H_GLA))
                sc_scr[r, :] = jnp.where(valid, sc, sc_scr[r, :])

    b3 = b.reshape(n_chunks, lc, w)
    b_last = b3[:, lc - 1:lc, :]
    kd = (k32 * jnp.exp(jnp.broadcast_to(b_last, b3.shape).reshape(n, w) - b)).astype(BF16)
    col_sum = _dot_tn(lfs, onehot_ref[...])
    decay_col = jnp.exp(col_sum[:w, :] + col_sum[w:, :])
    bd = bd_ref[...]
    states = [s_scr[...]]
    for c, r in enumerate(chunk_rows):
        upd = _dot_tn(kd[r, :], v[r, :])
        states.append(states[-1] * decay_col[:, c:c + 1] + upd * bd)
    s_scr[...] = states[-1]
    for c, r in enumerate(chunk_rows):
        inter = _dot(qe[r, :], states[c].astype(BF16))
        intra = _dot(sc_scr[r, :].astype(BF16), _block_diag_rows(v[r, :], H_GLA))
        o_ref[r, :] = intra + inter

    @pl.when(g == pl.num_programs(1) - 1)
    def _():
        _store_diag_blocks(s_scr[...], sfin_ref, H_GLA)


def _gla(gla, lf, states, layer, tabs, *, batch, seq_len, lc, n_chunks):
    rows = lc * n_chunks
    ng = seq_len // rows
    m = batch * seq_len
    col = lambda c: pl.BlockSpec((rows, W_GLA), lambda b, g: (b * ng + g, c))
    full = lambda a: pl.BlockSpec(a.shape, lambda b, g: (0,) * a.ndim)
    s_in_spec, s_in, s_out_spec = _state_specs(states, layer, H_GLA)
    kern = functools.partial(_gla_kernel, lc=lc, n_chunks=n_chunks, has_init=states is not None)
    return pl.pallas_call(
        kern,
        grid=(batch, ng),
        in_specs=[col(0), col(1), col(2), col(0)] + s_in_spec + [full(t) for t in tabs],
        out_specs=[pl.BlockSpec((rows, W_GLA), lambda b, g: (b * ng + g, 0)), s_out_spec],
        out_shape=[jax.ShapeDtypeStruct((m, W_GLA), F32),
                   jax.ShapeDtypeStruct((batch, H_GLA, HEAD_DIM, HEAD_DIM), F32)],
        scratch_shapes=[pltpu.VMEM((W_GLA, W_GLA), F32),
                        pltpu.VMEM((rows, H_GLA * lc), F32)],
        compiler_params=_params(2),
        name="gla",
    )(gla, gla, gla, lf, *s_in, *tabs)


def _gla_tables(lc, n_chunks):
    idx = np.arange(lc)
    tri = idx[:, None] >= idx[None, :]
    head = np.arange(W_GLA) // HEAD_DIM
    bd = head[:, None] == head[None, :]
    chunk_of_row = np.arange(lc * n_chunks) // lc
    onehot = chunk_of_row[:, None] == np.arange(LANES)[None, :]
    return jnp.asarray(tri, BF16), jnp.asarray(bd, F32), jnp.asarray(onehot, BF16)


def _band_attn_kernel(q_ref, kc_ref, vc_ref, gv_ref, hones_ref, o_ref,
                      kwin_t, vwin, bias_scr, s_scr, m_scr, ksq_prev, *, lc, rpi, n_iter):
    b_id = pl.program_id(0)
    g = pl.program_id(1)
    win = ATT_REACH + rpi
    rows = rpi * n_iter

    @pl.when((b_id == 0) & (g == 0))
    def _():
        kwin_t[:, 0:rows] = jnp.zeros((W_ATT, rows), BF16)
        vwin[0:rows, :] = jnp.zeros((rows, W_ATT), BF16)
        ksq_prev[0] = 0.0
        row = lax.broadcasted_iota(jnp.int32, (rpi, BIAS_W), 0)
        rel = lax.broadcasted_iota(jnp.int32, (rpi, BIAS_W), 1) - (row // lc) * lc
        in_band = (rel >= 0) & (rel < ATT_REACH + lc)
        for h in range(H_ATT):
            x = jnp.broadcast_to(gv_ref[h:h + 1, :], (rpi, BIAS_W))
            x = pltpu.roll(x, BIAS_W - (lc - 1), 1, stride=1, stride_axis=0)
            bias_scr[h // 2, (h % 2) * rpi:(h % 2 + 1) * rpi, :] = jnp.where(
                in_band, x * LOG2E, NEG_BIG)

    base = pl.multiple_of(g * rows, rows)
    k_cur = kc_ref[...].astype(F32)
    kwin_t[:, pl.ds(base + rows, rows)] = k_cur.T.astype(BF16)
    vwin[pl.ds(base + rows, rows), :] = vc_ref[...]

    head_ones = hones_ref[...]
    q_cur = q_ref[...].astype(F32)
    q_sq = jnp.max(_dot((q_cur * q_cur).astype(BF16), head_ones))
    k_sq_cur = jnp.max(_dot((k_cur * k_cur).astype(BF16), head_ones))
    k_sq = jnp.maximum(k_sq_cur, jnp.where(g == 0, 0.0, ksq_prev[0]))
    ksq_prev[0] = k_sq_cur
    room = ATT_SAFE_LOG2 - jnp.max(jnp.abs(gv_ref[...])) * LOG2E
    bounded = (room > 0.0) & (q_sq * k_sq * NORM_ROUNDING_MARGIN <= room * room)

    lane = lax.broadcasted_iota(jnp.int32, (rpi, LANES), 1)
    lane2 = lax.broadcasted_iota(jnp.int32, (2 * rpi, LANES), 1)
    head2 = lax.broadcasted_iota(jnp.int32, (2 * rpi, LANES), 0) // rpi
    col = lax.broadcasted_iota(jnp.int32, (2 * rpi, win), 1)

    def step(i, carry, *, masked, small):
        r0 = pl.multiple_of(i * rpi, rpi)
        first_valid = ATT_REACH - (g * rows + r0)

        def scores(pair):
            lanes = slice(pair * LANES, (pair + 1) * LANES)
            qp = q_ref[pl.ds(r0, rpi), lanes]
            q2 = jnp.concatenate([qp, qp], axis=0)
            q2 = jnp.where((lane2 // HEAD_DIM) == head2, q2, jnp.zeros_like(q2))
            s = _dot(q2, kwin_t[lanes, pl.ds(base + r0, win)]) + bias_scr[pair][:, :win]
            if masked:
                s = jnp.where(col >= first_valid, s, NEG_BIG)
            return s

        def finish(pair, e):
            lanes = slice(pair * LANES, (pair + 1) * LANES)
            den = jnp.sum(e, axis=-1, keepdims=True)
            pv = _dot(e.astype(BF16), vwin[pl.ds(base + r0, win), lanes]) / den
            o_ref[pl.ds(r0, rpi), lanes] = jnp.where(
                lane < HEAD_DIM, pv[:rpi, :], pv[rpi:, :]).astype(BF16)

        if small:
            for pair in range(H_ATT // 2):
                finish(pair, jnp.exp2(scores(pair)))
        else:
            for pair in range(H_ATT // 2):
                s = scores(pair)
                s_scr[pair, :, :win] = s
                m_scr[pair] = jnp.broadcast_to(
                    jnp.max(s, axis=-1, keepdims=True), (2 * rpi, LANES))
            for pair in range(H_ATT // 2):
                finish(pair, jnp.exp2(s_scr[pair, :, :win] - m_scr[pair][:, :1]))
        return carry

    def run(masked):
        @pl.when(bounded)
        def _():
            lax.fori_loop(0, n_iter, functools.partial(step, masked=masked, small=True), 0)

        @pl.when(jnp.logical_not(bounded))
        def _():
            lax.fori_loop(0, n_iter, functools.partial(step, masked=masked, small=False), 0)

    @pl.when(g == 0)
    def _():
        run(True)

    @pl.when(g != 0)
    def _():
        run(False)


def _band_attn(att, gv, layer, *, batch, seq_len, lc, rpi, n_iter):
    rows = rpi * n_iter
    assert rows == ATT_REACH, "the carried window is exactly one row group"
    ng = seq_len // rows
    m = batch * seq_len
    cur = lambda c: pl.BlockSpec((rows, W_ATT), lambda b, g: (b * ng + g, c))
    kern = functools.partial(_band_attn_kernel, lc=lc, rpi=rpi, n_iter=n_iter)
    head_ones = _head_ones()
    return pl.pallas_call(
        kern,
        grid=(batch, ng),
        in_specs=[cur(0), cur(1), cur(2), _layer_spec(gv, layer),
                  pl.BlockSpec(head_ones.shape, lambda b, g: (0, 0))],
        out_specs=pl.BlockSpec((rows, W_ATT), lambda b, g: (b * ng + g, 0)),
        out_shape=jax.ShapeDtypeStruct((m, W_ATT), BF16),
        scratch_shapes=[pltpu.VMEM((W_ATT, (ng + 1) * rows), BF16),
                        pltpu.VMEM(((ng + 1) * rows, W_ATT), BF16),
                        pltpu.VMEM((H_ATT // 2, 2 * rpi, BIAS_W), F32),
                        pltpu.VMEM((H_ATT // 2, 2 * rpi, BIAS_W), F32),
                        pltpu.VMEM((H_ATT // 2, 2 * rpi, LANES), F32),
                        pltpu.SMEM((1,), F32)],
        compiler_params=_params(2),
        name="band_attn",
    )(att, att, att, gv, head_ones)


def _sample_attn_kernel(q_ref, kn_ref, vn_ref, kt_ref, vt_ref, gv_ref, hones_ref, o_ref, *, ls):
    win = ATT_REACH + ls
    lane = lax.broadcasted_iota(jnp.int32, (ls, LANES), 1)
    lane2 = lax.broadcasted_iota(jnp.int32, (2 * ls, LANES), 1)
    head2 = lax.broadcasted_iota(jnp.int32, (2 * ls, LANES), 0) // ls

    head_ones = hones_ref[...]
    q32 = q_ref[...].astype(F32)
    kn32 = kn_ref[...].astype(F32)
    kc32 = kt_ref[...]
    q_sq = jnp.max(_dot((q32 * q32).astype(BF16), head_ones))
    k_sq = jnp.maximum(jnp.max(_dot((kn32 * kn32).astype(BF16), head_ones)),
                       jnp.max(jnp.sum(kc32 * kc32, axis=1)))
    room = ATT_SAFE_LOG2 - jnp.max(jnp.abs(gv_ref[...])) * LOG2E
    bounded = (room > 0.0) & (q_sq * k_sq * NORM_ROUNDING_MARGIN <= room * room)

    def head_pair(pair, small):
        lanes = slice(pair * LANES, (pair + 1) * LANES)
        qp = q_ref[:, lanes]
        q2 = jnp.concatenate([qp, qp], axis=0)
        q2 = jnp.where((lane2 // HEAD_DIM) == head2, q2, jnp.zeros_like(q2))
        kt = kt_ref[2 * pair:2 * pair + 2].reshape(2 * HEAD_DIM, ATT_REACH).astype(BF16)
        vt = vt_ref[2 * pair:2 * pair + 2].reshape(2 * HEAD_DIM, ATT_REACH).astype(BF16)
        bias = jnp.concatenate(
            [pltpu.roll(jnp.broadcast_to(gv_ref[h:h + 1, :], (ls, BIAS_W)),
                        BIAS_W - (ls - 1), 1, stride=1, stride_axis=0)
             for h in (2 * pair, 2 * pair + 1)], axis=0) * LOG2E
        s_old = _dot(q2, kt) + bias[:, :ATT_REACH]
        s_new = _dot_nt(q2, kn_ref[:, lanes]) + bias[:, ATT_REACH:win]
        if not small:
            mx = jnp.maximum(jnp.max(s_old, axis=-1, keepdims=True),
                             jnp.max(s_new, axis=-1, keepdims=True))
            s_old = s_old - mx
            s_new = s_new - mx
        e_old = jnp.exp2(s_old)
        e_new = jnp.exp2(s_new)
        den = jnp.sum(e_old, axis=-1, keepdims=True) + jnp.sum(e_new, axis=-1, keepdims=True)
        pv = (_dot_nt(e_old.astype(BF16), vt) + _dot(e_new.astype(BF16), vn_ref[:, lanes])) / den
        o_ref[:, lanes] = jnp.where(lane < HEAD_DIM, pv[:ls, :], pv[ls:, :]).astype(BF16)

    @pl.when(bounded)
    def _():
        for pair in range(H_ATT // 2):
            head_pair(pair, True)

    @pl.when(jnp.logical_not(bounded))
    def _():
        for pair in range(H_ATT // 2):
            head_pair(pair, False)


def _head_ones():
    lane_head = np.arange(W_ATT) // HEAD_DIM
    return jnp.asarray(lane_head[:, None] == np.arange(LANES)[None, :], BF16)


def _sample_attn(att, cache_kt, cache_vt, layer, gv, *, batch, ls):
    new = lambda c: pl.BlockSpec((ls, W_ATT), lambda b: (b, c))
    old = pl.BlockSpec((None, None, H_ATT, HEAD_DIM, ATT_REACH), lambda b: (layer, b, 0, 0, 0))
    head_ones = _head_ones()
    kern = functools.partial(_sample_attn_kernel, ls=ls)
    return pl.pallas_call(
        kern,
        grid=(batch,),
        in_specs=[new(0), new(1), new(2), old, old, _layer_spec(gv, layer),
                  pl.BlockSpec(head_ones.shape, lambda b: (0, 0))],
        out_specs=pl.BlockSpec((ls, W_ATT), lambda b: (b, 0)),
        out_shape=jax.ShapeDtypeStruct((batch * ls, W_ATT), BF16),
        compiler_params=_params(1),
        name="sample_attn",
    )(att, att, att, cache_kt, cache_vt, gv, head_ones)


def _bias_rows(rel_bias_l, lc):
    wv = np.arange(BIAS_W)
    dist = ATT_REACH + lc - 1 - wv
    ridx = np.clip(dist, -(CHUNK - 1), REL_MAX) + (CHUNK - 1)
    return rel_bias_l[..., ridx]


def _out_ffn_kernel(x_ref, oa_ref, ob_ref, oc_ref, sga_ref, sgb_ref, gnw_ref, glw_ref,
                    n2w_ref, fnw_ref, ones_ref, wout_ref, wup_ref, wdown_ref, y_ref,
                    *, final_norm):
    ones = ones_ref[...]

    def head_sum(t):
        return _dot(t.astype(BF16), ones)

    inv_d = 1.0 / HEAD_DIM
    oa = oa_ref[...]
    mu = head_sum(oa) * inv_d
    da = oa - mu
    var = head_sum(da * da) * inv_d
    a = da * lax.rsqrt(var + EPS) * gnw_ref[...] * sga_ref[...].astype(F32)
    ob = ob_ref[...]
    ms = head_sum(ob * ob) * inv_d
    bb = ob * lax.rsqrt(ms + EPS) * glw_ref[...] * sgb_ref[...].astype(F32)
    cat = jnp.concatenate([a.astype(BF16), bb.astype(BF16), oc_ref[...]], axis=1)
    h1 = x_ref[...] + _dot(cat, wout_ref[...])
    hn = _rmsnorm(h1, n2w_ref[...]).astype(BF16)
    acc = jnp.zeros_like(h1)
    n_slab = D_FF // D_MODEL
    for j in range(n_slab):
        sl = slice(j * D_MODEL, (j + 1) * D_MODEL)
        u = jnp.maximum(_dot(hn, wup_ref[:, sl]), 0.0)
        acc = acc + _dot((u * u).astype(BF16), wdown_ref[sl, :])
    h2 = h1 + acc
    if final_norm:
        h2 = _rmsnorm(h2, fnw_ref[...])
    y_ref[...] = h2


def _out_ffn(x2, oa, ob, oc, ret, gla, gnw, glw, n2w, fnw, ones, wout, wup, wdown,
             *, layer, tm, final_norm):
    m = x2.shape[0]
    full = lambda a: pl.BlockSpec(a.shape, lambda i: (0,) * a.ndim)
    at_layer = functools.partial(_layer_spec, layer=layer)
    row = lambda w, c=0: pl.BlockSpec((tm, w), lambda i: (i, c))
    kern = functools.partial(_out_ffn_kernel, final_norm=final_norm)
    return pl.pallas_call(
        kern,
        grid=(m // tm,),
        in_specs=[row(D_MODEL), row(W_RET), row(W_GLA), row(W_ATT), row(W_RET, 3),
                  row(W_GLA, 3), at_layer(gnw), at_layer(glw), at_layer(n2w), full(fnw),
                  full(ones), at_layer(wout), at_layer(wup), at_layer(wdown)],
        out_specs=row(D_MODEL),
        out_shape=jax.ShapeDtypeStruct((m, D_MODEL), F32),
        compiler_params=_params(1),
        name="out_ffn",
    )(x2, oa, ob, oc, ret, gla, gnw, glw, n2w, fnw, ones, wout, wup, wdown)


def _rotary_tables(pos, rows):
    half = HEAD_DIM // 2
    inv_freq = ROPE_BASE ** (-np.arange(half, dtype=np.float64) / half)
    ang = np.asarray(pos, np.float64)[:, None] * inv_freq[None, :]
    reps = (rows // len(pos), LANES // half)
    return (jnp.asarray(np.tile(np.cos(ang), reps), F32),
            jnp.asarray(np.tile(np.sin(ang), reps), F32))


def _run_path(x2, p, *, batch, seq_len, lc, n_chunks, tm, pos, layer, states, caches,
              final_norm, prev_kv=None):
    cos_t, sin_t = _rotary_tables(pos, max(seq_len, tm))
    ret, gla, lf, att, kc32, vc32 = _in_proj(
        x2, p["n1w"], p["w_in"], p["w_a2"], p["ba"], cos_t, sin_t, prev_kv,
        layer=layer, tm=tm, seq_len=seq_len)
    s_ret0, s_gla0 = states if states is not None else (None, None)
    oa, s_ret = _retention(ret, s_ret0, layer, _retention_tables(lc, n_chunks),
                           batch=batch, seq_len=seq_len, lc=lc, n_chunks=n_chunks)
    ob, s_gla = _gla(gla, lf, s_gla0, layer, _gla_tables(lc, n_chunks),
                     batch=batch, seq_len=seq_len, lc=lc, n_chunks=n_chunks)
    gv = _bias_rows(p["rel_bias"], lc)
    if caches is None:
        rpi = min(lc * n_chunks, ATT_ROWS_PER_STEP)
        oc = _band_attn(att, gv, layer, batch=batch, seq_len=seq_len, lc=lc, rpi=rpi,
                        n_iter=lc * n_chunks // rpi)
    else:
        oc = _sample_attn(att, caches[0], caches[1], layer, gv, batch=batch, ls=seq_len)
    y = _out_ffn(x2, oa, ob, oc, ret, gla, p["gnw"], p["glw"], p["n2w"], p["fnw"], p["ones"],
                 p["wout"], p["wup"], p["wdown"], layer=layer, tm=tm, final_norm=final_norm)
    return y, s_ret, s_gla, kc32, vc32


def kernel(x_prompt, x_sample, state_ret, state_gla, cache_attn_k, cache_attn_v, norm1_w, norm2_w, final_norm_w, w_in, w_gla_a2, b_gla_a, ret_gn_w, gla_norm_w, rel_bias, w_out, w_up, w_down):
    depth = w_in.shape[0]
    bp, lp, _ = x_prompt.shape
    bs, ls, _ = x_sample.shape
    keep = min(ATT_REACH, lp)
    head = np.arange(W_RET) // HEAD_DIM
    ones = jnp.asarray(head[:, None] == head[None, :], BF16)
    hp = x_prompt.reshape(bp * lp, D_MODEL)
    hs = x_sample.reshape(bs * ls, D_MODEL)
    pos_p = np.arange(lp)
    pos_s = PAST_LEN + np.arange(ls)
    tm_p = keep
    to_head_major = lambda c: jnp.transpose(c, (0, 1, 3, 4, 2))
    caches = (to_head_major(cache_attn_k.astype(F32)), to_head_major(cache_attn_v.astype(F32)))
    from_head_major = lambda t: jnp.transpose(
        t.reshape(depth, bp, H_ATT, HEAD_DIM, keep), (0, 1, 4, 2, 3))
    states = (state_ret.astype(F32), state_gla.astype(F32))
    outs = {k: [] for k in ("ret_p", "gla_p", "ret_s", "gla_s", "ks", "vs")}
    prompt_kv = None
    rows3 = lambda v: v.astype(F32).reshape(depth, 1, v.shape[-1])
    params = dict(
        n1w=rows3(norm1_w), n2w=rows3(norm2_w), fnw=final_norm_w.astype(F32)[None, :],
        gnw=rows3(ret_gn_w), glw=rows3(gla_norm_w), ba=rows3(b_gla_a),
        w_in=jnp.swapaxes(w_in.astype(F32), 1, 2), w_a2=w_gla_a2.astype(F32),
        rel_bias=rel_bias.astype(F32),
        wout=w_out.astype(BF16), wup=w_up.astype(BF16), wdown=w_down.astype(BF16), ones=ones)
    for l in range(depth):
        last = l == depth - 1
        hp, s_ret, s_gla, kt32, vt32 = _run_path(
            hp, params, batch=bp, seq_len=lp, lc=CHUNK, n_chunks=ATT_REACH // CHUNK, tm=tm_p,
            pos=pos_p, layer=l, states=None, caches=None, final_norm=last, prev_kv=prompt_kv)
        prompt_kv = (kt32, vt32)
        outs["ret_p"].append(s_ret)
        outs["gla_p"].append(s_gla)
        hs, s_ret, s_gla, kc32, vc32 = _run_path(
            hs, params, batch=bs, seq_len=ls, lc=ls, n_chunks=1, tm=bs * ls,
            pos=pos_s, layer=l, states=states, caches=caches, final_norm=last)
        outs["ret_s"].append(s_ret)
        outs["gla_s"].append(s_gla)
        outs["ks"].append(kc32.reshape(bs, ls, H_ATT, HEAD_DIM))
        outs["vs"].append(vc32.reshape(bs, ls, H_ATT, HEAD_DIM))
    y_prompt = hp.reshape(bp, lp, D_MODEL)
    y_sample = hs.reshape(bs, ls, D_MODEL)
    st = lambda k: jnp.stack(outs[k])
    return (y_prompt, y_sample, st("ret_p"), st("gla_p"),
            from_head_major(prompt_kv[0]), from_head_major(prompt_kv[1]),
            st("ret_s"), st("gla_s"), st("ks"), st("vs"))
```

```python
import functools

import numpy as np
import jax
import jax.numpy as jnp
from jax import lax
from jax.experimental import pallas as pl
from jax.experimental.pallas import tpu as pltpu

D_MODEL = 1024
CHUNK = 64
HEAD_DIM = 64
H_RET = 4
H_GLA = 4
H_ATT = 8
W_RET = H_RET * HEAD_DIM
W_GLA = H_GLA * HEAD_DIM
W_ATT = H_ATT * HEAD_DIM
GLA_RANK = 16
GLA_TAU = 16.0
ATT_REACH = 512
REL_MAX = 256
N_REL = CHUNK + REL_MAX
D_FF = 4 * D_MODEL
IN_COLS = 4 * W_RET + 4 * W_GLA + GLA_RANK + 3 * W_ATT
ROPE_BASE = 10000.0
EPS = 1e-6
PAST_LEN = 4096

LANES = 128
LOW_PAD = LANES
BIAS_W = 640
GLA_SAFE_LOG_DECAY = 50.0
ATT_SAFE_LOG2 = 80.0
NORM_ROUNDING_MARGIN = 1.05
ATT_ROWS_PER_STEP = 128
LOG2E = 1.4426950408889634
NEG_BIG = -1e30
VMEM_LIMIT = 56 * 1024 * 1024

F32 = jnp.float32
BF16 = jnp.bfloat16

_NT = (((1,), (1,)), ((), ()))
_TN = (((0,), (0,)), ((), ()))


def _dot(a, b):
    return jnp.dot(a, b, preferred_element_type=F32)


def _dot_nt(a, b):
    return lax.dot_general(a, b, _NT, preferred_element_type=F32)


def _dot_tn(a, b):
    return lax.dot_general(a, b, _TN, preferred_element_type=F32)


def _rmsnorm(x, w):
    return x * lax.rsqrt(jnp.mean(x * x, axis=-1, keepdims=True) + EPS) * w


def _params(n_grid_dims):
    return pltpu.CompilerParams(
        dimension_semantics=("arbitrary",) * n_grid_dims,
        vmem_limit_bytes=VMEM_LIMIT,
    )


def _layer_spec(a, layer):
    index = (layer,) + (0,) * (a.ndim - 1)
    return pl.BlockSpec((None,) + a.shape[1:], lambda *_: index, pipeline_mode=pl.Buffered(1))


def _block_diag_rows(x, n_heads):
    lane_head = lax.broadcasted_iota(jnp.int32, x.shape, 1) // HEAD_DIM
    zero = jnp.zeros_like(x)
    return jnp.concatenate(
        [jnp.where(lane_head == h, x, zero) for h in range(n_heads)], axis=0)


def _project_rows(x_ref, cos_ref, sin_ref, n1w_ref, ba_ref, weights,
                  ret_ref, gla_ref, lf_ref, att_ref):
    wret_ref, wgla_ref, watt_ref, wa2_ref = weights
    hn = _rmsnorm(x_ref[...], n1w_ref[...]).astype(BF16)
    cos = cos_ref[...]
    sin = sin_ref[...]
    first_half = (lax.broadcasted_iota(jnp.int32, cos.shape, 1) % HEAD_DIM) < (HEAD_DIM // 2)

    def rotary(t):
        swapped = jnp.where(first_half, -pltpu.roll(t, LANES - HEAD_DIM // 2, 1),
                            pltpu.roll(t, HEAD_DIM // 2, 1))
        return t * cos + swapped * sin

    scale = HEAD_DIM ** -0.5

    acc = _dot_nt(hn, wret_ref[...])
    for j in range(W_RET // LANES):
        sl = slice(j * LANES, (j + 1) * LANES)
        ret_ref[:, sl] = rotary(acc[:, sl]).astype(BF16)
        sk = slice(W_RET + j * LANES, W_RET + (j + 1) * LANES)
        ret_ref[:, sk] = (rotary(acc[:, sk]) * scale).astype(BF16)
    ret_ref[:, 2 * W_RET:3 * W_RET] = acc[:, 2 * W_RET:3 * W_RET].astype(BF16)
    g = acc[:, 3 * W_RET:4 * W_RET]
    ret_ref[:, 3 * W_RET:4 * W_RET] = (g * jax.nn.sigmoid(g)).astype(BF16)

    acc = _dot_nt(hn, wgla_ref[...])
    gla_ref[:, 0:W_GLA] = (acc[:, 0:W_GLA] * scale).astype(BF16)
    gla_ref[:, W_GLA:3 * W_GLA] = acc[:, W_GLA:3 * W_GLA].astype(BF16)
    g = acc[:, 3 * W_GLA:4 * W_GLA]
    gla_ref[:, 3 * W_GLA:4 * W_GLA] = (g * jax.nn.sigmoid(g)).astype(BF16)
    low = acc[:, 4 * W_GLA:4 * W_GLA + LOW_PAD].astype(BF16)
    z = _dot(low, wa2_ref[...]) + ba_ref[...]
    log_sig = jnp.minimum(z, 0.0) - jnp.log(1.0 + jnp.exp(-jnp.abs(z)))
    lf_ref[...] = log_sig * (1.0 / GLA_TAU)

    acc = _dot_nt(hn, watt_ref[...])
    att_ref[:, 0:W_ATT] = (acc[:, 0:W_ATT] * (scale * LOG2E)).astype(BF16)
    att_ref[:, W_ATT:3 * W_ATT] = acc[:, W_ATT:3 * W_ATT].astype(BF16)
    return acc[:, W_ATT:2 * W_ATT], acc[:, 2 * W_ATT:3 * W_ATT]


def _in_proj_kernel(*refs, n_long_tiles, tiles_per_seq, n_prev):
    refs = list(refs)
    take = lambda k: [refs.pop(0) for _ in range(k)]
    x_l, x_s, n1w_ref, win_ref, wa2in_ref, ba_ref, cos_l, sin_l, cos_s, sin_s = take(10)
    kprev_ref, vprev_ref = take(2) if n_prev else (None, None)
    outs_l = take(4)
    kc_l, vc_l = take(2)
    outs_s = take(4)
    kc_s, vc_s = take(2)
    weights = take(4)
    wret_ref, wgla_ref, watt_ref, wa2_ref = weights
    i = pl.program_id(0)

    @pl.when(i == 0)
    def _():
        n_main = 4 * W_RET + 4 * W_GLA
        wret_ref[...] = win_ref[0:4 * W_RET, :].astype(BF16)
        wgla_ref[0:4 * W_GLA, :] = win_ref[4 * W_RET:n_main, :].astype(BF16)
        wgla_ref[4 * W_GLA:4 * W_GLA + GLA_RANK, :] = (
            win_ref[n_main:n_main + GLA_RANK, :].astype(BF16))
        wgla_ref[4 * W_GLA + GLA_RANK:, :] = jnp.zeros((LOW_PAD - GLA_RANK, D_MODEL), BF16)
        watt_ref[...] = win_ref[n_main + GLA_RANK:IN_COLS, :].astype(BF16)
        wa2_ref[0:GLA_RANK, :] = wa2in_ref[...].astype(BF16)
        wa2_ref[GLA_RANK:, :] = jnp.zeros((LOW_PAD - GLA_RANK, W_GLA), BF16)

    @pl.when(i < n_long_tiles)
    def _():
        k32, v32 = _project_rows(x_l, cos_l, sin_l, n1w_ref, ba_ref, weights, *outs_l)

        @pl.when(i % tiles_per_seq == tiles_per_seq - 1)
        def _():
            if n_prev:
                kc_l[0:n_prev] = kprev_ref[...]
                vc_l[0:n_prev] = vprev_ref[...]
            kc_l[n_prev] = k32.T
            vc_l[n_prev] = v32.T

    @pl.when(i == n_long_tiles)
    def _():
        k32, v32 = _project_rows(x_s, cos_s, sin_s, n1w_ref, ba_ref, weights, *outs_s)
        kc_s[...] = k32
        vc_s[...] = v32


def _in_proj(x_long, x_short, n1w, w_in, w_a2, ba, tabs_long, tabs_short, prev_kv,
             *, layer, tm, seq_len):
    m = x_long.shape[0]
    ms = x_short.shape[0]
    n_tiles = m // tm
    tab_blocks = tabs_long[0].shape[0] // tm
    tiles_per_seq = seq_len // tm
    at_layer = functools.partial(_layer_spec, layer=layer)
    tile = lambda i: jnp.minimum(i, n_tiles - 1)
    row = lambda w: pl.BlockSpec((tm, w), lambda i: (tile(i), 0))
    tab = pl.BlockSpec((tm, LANES), lambda i: (tile(i) % tab_blocks, 0))
    whole = lambda w: pl.BlockSpec((ms, w), lambda i: (0, 0))
    n_prev = 0 if prev_kv is None else prev_kv[0].shape[0]
    stacked = lambda n: pl.BlockSpec((n, None, W_ATT, tm),
                                     lambda i: (0, tile(i) // tiles_per_seq, 0, 0))
    prev_specs = [stacked(n_prev)] * 2 if n_prev else []
    groups = (4 * W_RET, 4 * W_GLA, W_GLA, 3 * W_ATT)
    dtypes = (BF16, BF16, F32, BF16)
    kern = functools.partial(_in_proj_kernel, n_long_tiles=n_tiles, tiles_per_seq=tiles_per_seq,
                             n_prev=n_prev)
    outs = pl.pallas_call(
        kern,
        grid=(n_tiles + 1,),
        in_specs=[row(D_MODEL), whole(D_MODEL), at_layer(n1w), at_layer(w_in), at_layer(w_a2),
                  at_layer(ba), tab, tab, whole(LANES), whole(LANES)] + prev_specs,
        out_specs=([row(w) for w in groups] + [stacked(n_prev + 1)] * 2
                   + [whole(w) for w in groups] + [whole(W_ATT)] * 2),
        out_shape=([jax.ShapeDtypeStruct((m, w), d) for w, d in zip(groups, dtypes)]
                   + [jax.ShapeDtypeStruct((n_prev + 1, m // seq_len, W_ATT, tm), F32)] * 2
                   + [jax.ShapeDtypeStruct((ms, w), d) for w, d in zip(groups, dtypes)]
                   + [jax.ShapeDtypeStruct((ms, W_ATT), F32)] * 2),
        scratch_shapes=[pltpu.VMEM((4 * W_RET, D_MODEL), BF16),
                        pltpu.VMEM((4 * W_GLA + LOW_PAD, D_MODEL), BF16),
                        pltpu.VMEM((3 * W_ATT, D_MODEL), BF16),
                        pltpu.VMEM((LOW_PAD, W_GLA), BF16)],
        compiler_params=_params(1),
        name="in_proj",
    )(x_long, x_short, n1w, w_in, w_a2, ba, *tabs_long, *tabs_short,
      *(prev_kv if n_prev else ()))
    return outs[:6], outs[6:]


def _init_states(s_scr, s0_ref, n_seq, n_heads):
    if s0_ref is None:
        s_scr[...] = jnp.zeros_like(s_scr)
        return
    zero = jnp.zeros((HEAD_DIM, HEAD_DIM), F32)
    for i in range(n_seq):
        s_scr[i] = jnp.concatenate(
            [jnp.concatenate([s0_ref[i, h] if j == h else zero for j in range(n_heads)], axis=1)
             for h in range(n_heads)], axis=0)


def _store_states(s_scr, out_ref, n_seq, n_heads):
    for i in range(n_seq):
        s = s_scr[i]
        for h in range(n_heads):
            sl = slice(h * HEAD_DIM, (h + 1) * HEAD_DIM)
            out_ref[i, h] = s[sl, sl]


def _chunk_rows(lc, n_chunks, n_seq):
    return [[slice((i * n_chunks + c) * lc, (i * n_chunks + c + 1) * lc)
             for c in range(n_chunks)] for i in range(n_seq)]


def _retention_body(q_ref, k_ref, v_ref, s0_ref, tab_refs, o_ref, sfin_ref, s_scr,
                    *, lc, n_chunks, n_seq):
    dtab_ref, qdec_ref, kdec_ref, gtab_ref, bd_ref = tab_refs
    g = pl.program_id(1)

    @pl.when(g == 0)
    def _():
        _init_states(s_scr, s0_ref, n_seq, H_RET)

    dtab = dtab_ref[...]
    q = q_ref[...]
    k = k_ref[...]
    v = v_ref[...]
    kd = (k.astype(F32) * kdec_ref[...]).astype(BF16)
    gtab = gtab_ref[...]
    bd = bd_ref[...]
    for i, rows in enumerate(_chunk_rows(lc, n_chunks, n_seq)):
        states = [s_scr[i]]
        for r in rows:
            upd = _dot_tn(kd[r, :], v[r, :])
            states.append(states[-1] * gtab + upd * bd)
        s_scr[i] = states[-1]
        for c, r in enumerate(rows):
            scores = _dot_nt(q[r, :], _block_diag_rows(k[r, :], H_RET))
            p = (scores * dtab).astype(BF16)
            intra = _dot(p, _block_diag_rows(v[r, :], H_RET))
            inter = _dot(q[r, :], states[c].astype(BF16))
            o_ref[r, :] = intra + inter * qdec_ref[r, :]

    @pl.when(g == pl.num_programs(1) - 1)
    def _():
        _store_states(s_scr, sfin_ref, n_seq, H_RET)


def _retention_tables(lc, n_chunks):
    lg = np.log1p(-np.exp2(-5.0 - np.arange(H_RET, dtype=np.float64)))
    idx = np.arange(lc)
    diff = idx[:, None] - idx[None, :]
    causal = diff >= 0
    dpos = np.where(causal, diff, 0).astype(np.float64)
    decay = np.where(causal[None], np.exp(dpos[None] * lg[:, None, None]), 0.0)
    dtab = np.transpose(decay, (1, 0, 2)).reshape(lc, H_RET * lc)
    q_decay = np.exp((idx + 1).astype(np.float64)[:, None] * lg[None, :])
    k_decay = np.exp((lc - 1 - idx).astype(np.float64)[:, None] * lg[None, :])
    qdec = np.tile(np.repeat(q_decay, HEAD_DIM, axis=1), (n_chunks, 1))
    kdec = np.tile(np.repeat(k_decay, HEAD_DIM, axis=1), (n_chunks, 1))
    gtab = np.broadcast_to(np.repeat(np.exp(lc * lg), HEAD_DIM)[:, None], (W_RET, W_RET))
    head = np.arange(W_RET) // HEAD_DIM
    bd = head[:, None] == head[None, :]
    return tuple(jnp.asarray(t, F32) for t in (dtab, qdec, kdec, gtab, bd))


def _split_bf16(x):
    hi = x.astype(BF16)
    lo = (x - hi.astype(F32)).astype(BF16)
    return jnp.concatenate([hi, lo], axis=1)


def _group_reference(b, row, half):
    n, w = b.shape
    group = 2 * half
    if group >= 8:
        b3 = b.reshape(n // group, group, w)
        return jnp.broadcast_to(b3[:, half - 1:half, :], b3.shape).reshape(n, w)
    if group == 4:
        i = row & 3
        return jnp.where(i == 0, pltpu.roll(b, n - 1, 0),
                         jnp.where(i == 1, b,
                                   jnp.where(i == 2, pltpu.roll(b, 1, 0), pltpu.roll(b, 2, 0))))
    return jnp.where((row & 1) == 1, pltpu.roll(b, 1, 0), b)


def _gla_body(q_ref, k_ref, v_ref, lf_ref, s0_ref, tab_refs, o_ref, sfin_ref, s_scr, sc_scr,
              *, lc, n_chunks, n_seq):
    tri_ref, bd_ref, onehot_ref = tab_refs
    g = pl.program_id(1)

    @pl.when(g == 0)
    def _():
        _init_states(s_scr, s0_ref, n_seq, H_GLA)

    n_lev = lc.bit_length() - 1
    w = W_GLA
    n = lc * n_chunks * n_seq
    seq_rows = _chunk_rows(lc, n_chunks, n_seq)
    chunk_rows = [r for rows in seq_rows for r in rows]
    row = lax.broadcasted_iota(jnp.int32, (n, w), 0)
    trow = lax.broadcasted_iota(jnp.int32, (lc, H_GLA * lc), 0)
    scol = lax.broadcasted_iota(jnp.int32, (lc, H_GLA * lc), 1) & (lc - 1)
    q = q_ref[...]
    k = k_ref[...]
    v = v_ref[...]
    q32 = q.astype(F32)
    k32 = k.astype(F32)
    tri = tri_ref[...]
    lfs = _split_bf16(lf_ref[...])
    cs = jnp.concatenate([_dot(tri, lfs[r, :]) for r in chunk_rows], axis=0)
    b = cs[:, :w] + cs[:, w:]

    qe = (q32 * jnp.exp(b)).astype(BF16)
    mild = jnp.min(b) >= -GLA_SAFE_LOG_DECAY

    @pl.when(mild)
    def _():
        ke = (k32 * jnp.exp(-b)).astype(BF16)
        for r in chunk_rows:
            sc = _dot_nt(qe[r, :], _block_diag_rows(ke[r, :], H_GLA))
            sc_scr[r, :] = jnp.where(trow >= scol, sc, 0.0)

    @pl.when(jnp.logical_not(mild))
    def _():
        for r in chunk_rows:
            sc = _dot_nt(q[r, :], _block_diag_rows(k[r, :], H_GLA))
            sc_scr[r, :] = jnp.where(trow == scol, sc, 0.0)
        for lev in range(n_lev):
            half = lc >> (lev + 1)
            shift = half.bit_length()
            upper = (row & half) != 0
            e = jnp.exp(-jnp.abs(b - _group_reference(b, row, half)))
            z = (jnp.where(upper, q32, k32) * e).astype(BF16)
            valid = (((trow >> shift) == (scol >> shift))
                     & ((trow & half) != 0) & ((scol & half) == 0))
            for r in chunk_rows:
                zc = z[r, :]
                sc = _dot_nt(zc, _block_diag_rows(zc, H_GLA))
                sc_scr[r, :] = jnp.where(valid, sc, sc_scr[r, :])

    b3 = b.reshape(n_chunks * n_seq, lc, w)
    b_last = b3[:, lc - 1:lc, :]
    kd = (k32 * jnp.exp(jnp.broadcast_to(b_last, b3.shape).reshape(n, w) - b)).astype(BF16)
    col_sum = _dot_tn(lfs, onehot_ref[...])
    decay_col = jnp.exp(col_sum[:w, :] + col_sum[w:, :])
    bd = bd_ref[...]
    for i, rows in enumerate(seq_rows):
        states = [s_scr[i]]
        for c, r in enumerate(rows):
            j = i * n_chunks + c
            upd = _dot_tn(kd[r, :], v[r, :])
            states.append(states[-1] * decay_col[:, j:j + 1] + upd * bd)
        s_scr[i] = states[-1]
        for c, r in enumerate(rows):
            inter = _dot(qe[r, :], states[c].astype(BF16))
            intra = _dot(sc_scr[r, :].astype(BF16), _block_diag_rows(v[r, :], H_GLA))
            o_ref[r, :] = intra + inter

    @pl.when(g == pl.num_programs(1) - 1)
    def _():
        _store_states(s_scr, sfin_ref, n_seq, H_GLA)


def _linear_mixers_kernel(*refs, lc, n_chunks, n_seq, has_init):
    refs = list(refs)
    take = lambda k: [refs.pop(0) for _ in range(k)]
    rq, rk, rv, gq, gk, gv, lf = take(7)
    s0r, s0g = take(2) if has_init else (None, None)
    ret_tabs = take(5)
    gla_tabs = take(3)
    oa_ref, sr_ref, ob_ref, sg_ref, sr_scr, sg_scr, sc_scr = take(7)
    dims = dict(lc=lc, n_chunks=n_chunks, n_seq=n_seq)
    _retention_body(rq, rk, rv, s0r, ret_tabs, oa_ref, sr_ref, sr_scr, **dims)
    _gla_body(gq, gk, gv, lf, s0g, gla_tabs, ob_ref, sg_ref, sg_scr, sc_scr, **dims)


def _linear_mixers(ret, gla, lf, states, layer, *, batch, seq_len, lc, n_chunks, n_seq):
    rows = lc * n_chunks * n_seq
    ng = seq_len * n_seq // rows
    m = batch * seq_len
    width = W_RET
    col = lambda c: pl.BlockSpec((rows, width), lambda b, g: (b * ng + g, c))
    full = lambda a: pl.BlockSpec(a.shape, lambda b, g: (0,) * a.ndim)
    blk = (n_seq, H_RET, HEAD_DIM, HEAD_DIM)
    s_out_spec = pl.BlockSpec(blk, lambda b, g: (b, 0, 0, 0))
    s_out_shape = jax.ShapeDtypeStruct((batch, H_RET, HEAD_DIM, HEAD_DIM), F32)
    s_in_specs = []
    if states is not None:
        s_in_specs = [pl.BlockSpec((None,) + blk, lambda b, g: (layer, b, 0, 0, 0))] * 2
    ret_tabs = _retention_tables(lc, n_chunks * n_seq)
    gla_tabs = _gla_tables(lc, n_chunks * n_seq)
    kern = functools.partial(_linear_mixers_kernel, lc=lc, n_chunks=n_chunks, n_seq=n_seq,
                             has_init=states is not None)
    out_rows = pl.BlockSpec((rows, width), lambda b, g: (b * ng + g, 0))
    return pl.pallas_call(
        kern,
        grid=(batch // n_seq, ng),
        in_specs=([col(0), col(1), col(2), col(0), col(1), col(2), col(0)] + s_in_specs
                  + [full(t) for t in ret_tabs + gla_tabs]),
        out_specs=[out_rows, s_out_spec, out_rows, s_out_spec],
        out_shape=[jax.ShapeDtypeStruct((m, width), F32), s_out_shape,
                   jax.ShapeDtypeStruct((m, width), F32), s_out_shape],
        scratch_shapes=[pltpu.VMEM((n_seq, width, width), F32),
                        pltpu.VMEM((n_seq, width, width), F32),
                        pltpu.VMEM((rows, H_GLA * lc), F32)],
        compiler_params=_params(2),
        name="linear_mixers",
    )(ret, ret, ret, gla, gla, gla, lf, *(states or ()), *ret_tabs, *gla_tabs)


def _gla_tables(lc, n_chunks):
    idx = np.arange(lc)
    tri = idx[:, None] >= idx[None, :]
    head = np.arange(W_GLA) // HEAD_DIM
    bd = head[:, None] == head[None, :]
    chunk_of_row = np.arange(lc * n_chunks) // lc
    onehot = chunk_of_row[:, None] == np.arange(LANES)[None, :]
    return jnp.asarray(tri, BF16), jnp.asarray(bd, F32), jnp.asarray(onehot, BF16)


def _band_attn_kernel(q_ref, kc_ref, vc_ref, gv_ref, hones_ref, o_ref,
                      kwin_t, vwin, bias_scr, s_scr, m_scr, ksq_prev, *, lc, rpi, n_iter):
    b_id = pl.program_id(0)
    g = pl.program_id(1)
    win = ATT_REACH + rpi
    rows = rpi * n_iter

    @pl.when((b_id == 0) & (g == 0))
    def _():
        kwin_t[:, 0:rows] = jnp.zeros((W_ATT, rows), BF16)
        vwin[0:rows, :] = jnp.zeros((rows, W_ATT), BF16)
        ksq_prev[0] = 0.0
        row = lax.broadcasted_iota(jnp.int32, (rpi, BIAS_W), 0)
        rel = lax.broadcasted_iota(jnp.int32, (rpi, BIAS_W), 1) - (row // lc) * lc
        in_band = (rel >= 0) & (rel < ATT_REACH + lc)
        for h in range(H_ATT):
            x = jnp.broadcast_to(gv_ref[h:h + 1, :], (rpi, BIAS_W))
            x = pltpu.roll(x, BIAS_W - (lc - 1), 1, stride=1, stride_axis=0)
            bias_scr[h // 2, (h % 2) * rpi:(h % 2 + 1) * rpi, :] = jnp.where(
                in_band, x * LOG2E, NEG_BIG)

    base = pl.multiple_of(g * rows, rows)
    k_cur = kc_ref[...].astype(F32)
    kwin_t[:, pl.ds(base + rows, rows)] = k_cur.T.astype(BF16)
    vwin[pl.ds(base + rows, rows), :] = vc_ref[...]

    head_ones = hones_ref[...]
    q_cur = q_ref[...].astype(F32)
    q_sq = jnp.max(_dot((q_cur * q_cur).astype(BF16), head_ones))
    k_sq_cur = jnp.max(_dot((k_cur * k_cur).astype(BF16), head_ones))
    k_sq = jnp.maximum(k_sq_cur, jnp.where(g == 0, 0.0, ksq_prev[0]))
    ksq_prev[0] = k_sq_cur
    room = ATT_SAFE_LOG2 - jnp.max(jnp.abs(gv_ref[...])) * LOG2E
    bounded = (room > 0.0) & (q_sq * k_sq * NORM_ROUNDING_MARGIN <= room * room)

    lane = lax.broadcasted_iota(jnp.int32, (rpi, LANES), 1)
    lane2 = lax.broadcasted_iota(jnp.int32, (2 * rpi, LANES), 1)
    head2 = lax.broadcasted_iota(jnp.int32, (2 * rpi, LANES), 0) // rpi
    col = lax.broadcasted_iota(jnp.int32, (2 * rpi, win), 1)

    def step(i, carry, *, masked, small):
        r0 = pl.multiple_of(i * rpi, rpi)
        first_valid = ATT_REACH - (g * rows + r0)

        def scores(pair):
            lanes = slice(pair * LANES, (pair + 1) * LANES)
            qp = q_ref[pl.ds(r0, rpi), lanes]
            q2 = jnp.concatenate([qp, qp], axis=0)
            q2 = jnp.where((lane2 // HEAD_DIM) == head2, q2, jnp.zeros_like(q2))
            s = _dot(q2, kwin_t[lanes, pl.ds(base + r0, win)]) + bias_scr[pair][:, :win]
            if masked:
                s = jnp.where(col >= first_valid, s, NEG_BIG)
            return s

        def finish(pair, e):
            lanes = slice(pair * LANES, (pair + 1) * LANES)
            den = jnp.sum(e, axis=-1, keepdims=True)
            pv = _dot(e.astype(BF16), vwin[pl.ds(base + r0, win), lanes]) / den
            o_ref[pl.ds(r0, rpi), lanes] = jnp.where(
                lane < HEAD_DIM, pv[:rpi, :], pv[rpi:, :]).astype(BF16)

        if small:
            for pair in range(H_ATT // 2):
                finish(pair, jnp.exp2(scores(pair)))
        else:
            for pair in range(H_ATT // 2):
                s = scores(pair)
                s_scr[pair, :, :win] = s
                m_scr[pair] = jnp.broadcast_to(
                    jnp.max(s, axis=-1, keepdims=True), (2 * rpi, LANES))
            for pair in range(H_ATT // 2):
                finish(pair, jnp.exp2(s_scr[pair, :, :win] - m_scr[pair][:, :1]))
        return carry

    def run(masked):
        @pl.when(bounded)
        def _():
            lax.fori_loop(0, n_iter, functools.partial(step, masked=masked, small=True), 0)

        @pl.when(jnp.logical_not(bounded))
        def _():
            lax.fori_loop(0, n_iter, functools.partial(step, masked=masked, small=False), 0)

    @pl.when(g == 0)
    def _():
        run(True)

    @pl.when(g != 0)
    def _():
        run(False)


def _band_attn(att, gv, layer, *, batch, seq_len, lc, rpi, n_iter):
    rows = rpi * n_iter
    assert rows == ATT_REACH, "the carried window is exactly one row group"
    ng = seq_len // rows
    m = batch * seq_len
    cur = lambda c: pl.BlockSpec((rows, W_ATT), lambda b, g: (b * ng + g, c))
    kern = functools.partial(_band_attn_kernel, lc=lc, rpi=rpi, n_iter=n_iter)
    head_ones = _head_ones()
    return pl.pallas_call(
        kern,
        grid=(batch, ng),
        in_specs=[cur(0), cur(1), cur(2), _layer_spec(gv, layer),
                  pl.BlockSpec(head_ones.shape, lambda b, g: (0, 0))],
        out_specs=pl.BlockSpec((rows, W_ATT), lambda b, g: (b * ng + g, 0)),
        out_shape=jax.ShapeDtypeStruct((m, W_ATT), BF16),
        scratch_shapes=[pltpu.VMEM((W_ATT, (ng + 1) * rows), BF16),
                        pltpu.VMEM(((ng + 1) * rows, W_ATT), BF16),
                        pltpu.VMEM((H_ATT // 2, 2 * rpi, BIAS_W), F32),
                        pltpu.VMEM((H_ATT // 2, 2 * rpi, BIAS_W), F32),
                        pltpu.VMEM((H_ATT // 2, 2 * rpi, LANES), F32),
                        pltpu.SMEM((1,), F32)],
        compiler_params=_params(2),
        name="band_attn",
    )(att, att, att, gv, head_ones)


def _sample_attn_kernel(q_ref, kn_ref, vn_ref, kt_ref, vt_ref, gv_ref, hones_ref, o_ref, *, ls):
    win = ATT_REACH + ls
    lane = lax.broadcasted_iota(jnp.int32, (ls, LANES), 1)
    lane2 = lax.broadcasted_iota(jnp.int32, (2 * ls, LANES), 1)
    head2 = lax.broadcasted_iota(jnp.int32, (2 * ls, LANES), 0) // ls

    head_ones = hones_ref[...]
    q32 = q_ref[...].astype(F32)
    kn32 = kn_ref[...].astype(F32)
    kc32 = kt_ref[...]
    q_sq = jnp.max(_dot((q32 * q32).astype(BF16), head_ones))
    k_sq = jnp.maximum(jnp.max(_dot((kn32 * kn32).astype(BF16), head_ones)),
                       jnp.max(jnp.sum(kc32 * kc32, axis=1)))
    room = ATT_SAFE_LOG2 - jnp.max(jnp.abs(gv_ref[...])) * LOG2E
    bounded = (room > 0.0) & (q_sq * k_sq * NORM_ROUNDING_MARGIN <= room * room)

    def head_pair(pair, small):
        lanes = slice(pair * LANES, (pair + 1) * LANES)
        qp = q_ref[:, lanes]
        q2 = jnp.concatenate([qp, qp], axis=0)
        q2 = jnp.where((lane2 // HEAD_DIM) == head2, q2, jnp.zeros_like(q2))
        kt = kt_ref[2 * pair:2 * pair + 2].reshape(2 * HEAD_DIM, ATT_REACH).astype(BF16)
        vt = vt_ref[2 * pair:2 * pair + 2].reshape(2 * HEAD_DIM, ATT_REACH).astype(BF16)
        bias = jnp.concatenate(
            [pltpu.roll(jnp.broadcast_to(gv_ref[h:h + 1, :], (ls, BIAS_W)),
                        BIAS_W - (ls - 1), 1, stride=1, stride_axis=0)
             for h in (2 * pair, 2 * pair + 1)], axis=0) * LOG2E
        s_old = _dot(q2, kt) + bias[:, :ATT_REACH]
        s_new = _dot_nt(q2, kn_ref[:, lanes]) + bias[:, ATT_REACH:win]
        if not small:
            mx = jnp.maximum(jnp.max(s_old, axis=-1, keepdims=True),
                             jnp.max(s_new, axis=-1, keepdims=True))
            s_old = s_old - mx
            s_new = s_new - mx
        e_old = jnp.exp2(s_old)
        e_new = jnp.exp2(s_new)
        den = jnp.sum(e_old, axis=-1, keepdims=True) + jnp.sum(e_new, axis=-1, keepdims=True)
        pv = (_dot_nt(e_old.astype(BF16), vt) + _dot(e_new.astype(BF16), vn_ref[:, lanes])) / den
        o_ref[:, lanes] = jnp.where(lane < HEAD_DIM, pv[:ls, :], pv[ls:, :]).astype(BF16)

    @pl.when(bounded)
    def _():
        for pair in range(H_ATT // 2):
            head_pair(pair, True)

    @pl.when(jnp.logical_not(bounded))
    def _():
        for pair in range(H_ATT // 2):
            head_pair(pair, False)


def _head_ones():
    lane_head = np.arange(W_ATT) // HEAD_DIM
    return jnp.asarray(lane_head[:, None] == np.arange(LANES)[None, :], BF16)


def _sample_attn(att, cache_kt, cache_vt, layer, gv, *, batch, ls):
    new = lambda c: pl.BlockSpec((ls, W_ATT), lambda b: (b, c))
    old = pl.BlockSpec((None, None, H_ATT, HEAD_DIM, ATT_REACH), lambda b: (layer, b, 0, 0, 0))
    head_ones = _head_ones()
    kern = functools.partial(_sample_attn_kernel, ls=ls)
    return pl.pallas_call(
        kern,
        grid=(batch,),
        in_specs=[new(0), new(1), new(2), old, old, _layer_spec(gv, layer),
                  pl.BlockSpec(head_ones.shape, lambda b: (0, 0))],
        out_specs=pl.BlockSpec((ls, W_ATT), lambda b: (b, 0)),
        out_shape=jax.ShapeDtypeStruct((batch * ls, W_ATT), BF16),
        compiler_params=_params(1),
        name="sample_attn",
    )(att, att, att, cache_kt, cache_vt, gv, head_ones)


def _bias_rows(rel_bias_l, lc):
    wv = np.arange(BIAS_W)
    dist = ATT_REACH + lc - 1 - wv
    ridx = np.clip(dist, -(CHUNK - 1), REL_MAX) + (CHUNK - 1)
    return rel_bias_l[..., ridx]


def _mix_rows(x_ref, oa_ref, ob_ref, oc_ref, sga_ref, sgb_ref, params, y_ref, *, final_norm):
    gnw_ref, glw_ref, n2w_ref, fnw_ref, ones_ref, wout_ref, wup_ref, wdown_ref = params
    ones = ones_ref[...]

    def head_sum(t):
        return _dot(t.astype(BF16), ones)

    inv_d = 1.0 / HEAD_DIM
    oa = oa_ref[...]
    mu = head_sum(oa) * inv_d
    da = oa - mu
    var = head_sum(da * da) * inv_d
    a = da * lax.rsqrt(var + EPS) * gnw_ref[...] * sga_ref[...].astype(F32)
    ob = ob_ref[...]
    ms = head_sum(ob * ob) * inv_d
    bb = ob * lax.rsqrt(ms + EPS) * glw_ref[...] * sgb_ref[...].astype(F32)
    cat = jnp.concatenate([a.astype(BF16), bb.astype(BF16), oc_ref[...]], axis=1)
    h1 = x_ref[...] + _dot(cat, wout_ref[...])
    hn = _rmsnorm(h1, n2w_ref[...]).astype(BF16)
    acc = jnp.zeros_like(h1)
    n_slab = D_FF // D_MODEL
    for j in range(n_slab):
        sl = slice(j * D_MODEL, (j + 1) * D_MODEL)
        u = jnp.maximum(_dot(hn, wup_ref[:, sl]), 0.0)
        acc = acc + _dot((u * u).astype(BF16), wdown_ref[sl, :])
    h2 = h1 + acc
    if final_norm:
        h2 = _rmsnorm(h2, fnw_ref[...])
    y_ref[...] = h2


def _out_ffn_kernel(*refs, n_long_tiles, final_norm):
    rows_l, rows_s, params = refs[0:6], refs[6:12], refs[12:20]
    y_l, y_s = refs[20:22]
    i = pl.program_id(0)

    @pl.when(i < n_long_tiles)
    def _():
        _mix_rows(*rows_l, params, y_l, final_norm=final_norm)

    @pl.when(i == n_long_tiles)
    def _():
        _mix_rows(*rows_s, params, y_s, final_norm=final_norm)


def _out_ffn(long_in, short_in, gnw, glw, n2w, fnw, ones, wout, wup, wdown,
             *, layer, tm, final_norm):
    m = long_in[0].shape[0]
    ms = short_in[0].shape[0]
    n_tiles = m // tm
    full = lambda a: pl.BlockSpec(a.shape, lambda i: (0,) * a.ndim)
    at_layer = functools.partial(_layer_spec, layer=layer)
    tile = lambda i: jnp.minimum(i, n_tiles - 1)
    row = lambda w, c=0: pl.BlockSpec((tm, w), lambda i: (tile(i), c))
    whole = lambda w, c=0: pl.BlockSpec((ms, w), lambda i: (0, c))
    row_specs = lambda spec: [spec(D_MODEL), spec(W_RET), spec(W_GLA), spec(W_ATT),
                              spec(W_RET, 3), spec(W_GLA, 3)]
    kern = functools.partial(_out_ffn_kernel, n_long_tiles=n_tiles, final_norm=final_norm)
    return pl.pallas_call(
        kern,
        grid=(n_tiles + 1,),
        in_specs=(row_specs(row) + row_specs(whole)
                  + [at_layer(gnw), at_layer(glw), at_layer(n2w), full(fnw), full(ones),
                     at_layer(wout), at_layer(wup), at_layer(wdown)]),
        out_specs=[row(D_MODEL), whole(D_MODEL)],
        out_shape=[jax.ShapeDtypeStruct((m, D_MODEL), F32),
                   jax.ShapeDtypeStruct((ms, D_MODEL), F32)],
        compiler_params=_params(1),
        name="out_ffn",
    )(*long_in, *short_in, gnw, glw, n2w, fnw, ones, wout, wup, wdown)


def _rotary_tables(pos, rows):
    half = HEAD_DIM // 2
    inv_freq = ROPE_BASE ** (-np.arange(half, dtype=np.float64) / half)
    ang = np.asarray(pos, np.float64)[:, None] * inv_freq[None, :]
    reps = (rows // len(pos), LANES // half)
    return (jnp.asarray(np.tile(np.cos(ang), reps), F32),
            jnp.asarray(np.tile(np.sin(ang), reps), F32))


def _mixers(proj, rel_bias, *, batch, seq_len, lc, n_chunks, layer, states, caches):
    ret, gla, lf, att = proj
    n_seq = batch if lc * n_chunks == seq_len else 1
    oa, s_ret, ob, s_gla = _linear_mixers(
        ret, gla, lf, states, layer, batch=batch, seq_len=seq_len, lc=lc, n_chunks=n_chunks,
        n_seq=n_seq)
    gv = _bias_rows(rel_bias, lc)
    if caches is None:
        rpi = min(lc * n_chunks, ATT_ROWS_PER_STEP)
        oc = _band_attn(att, gv, layer, batch=batch, seq_len=seq_len, lc=lc, rpi=rpi,
                        n_iter=lc * n_chunks // rpi)
    else:
        oc = _sample_attn(att, caches[0], caches[1], layer, gv, batch=batch, ls=seq_len)
    return (oa, ob, oc, ret, gla), s_ret, s_gla


def kernel(x_prompt, x_sample, state_ret, state_gla, cache_attn_k, cache_attn_v, norm1_w, norm2_w, final_norm_w, w_in, w_gla_a2, b_gla_a, ret_gn_w, gla_norm_w, rel_bias, w_out, w_up, w_down):
    depth = w_in.shape[0]
    bp, lp, _ = x_prompt.shape
    bs, ls, _ = x_sample.shape
    keep = min(ATT_REACH, lp)
    head = np.arange(W_RET) // HEAD_DIM
    ones = jnp.asarray(head[:, None] == head[None, :], BF16)
    hp = x_prompt.reshape(bp * lp, D_MODEL)
    hs = x_sample.reshape(bs * ls, D_MODEL)
    pos_p = np.arange(lp)
    pos_s = PAST_LEN + np.arange(ls)
    tm_p = keep
    to_head_major = lambda c: jnp.transpose(c, (0, 1, 3, 4, 2))
    caches = (to_head_major(cache_attn_k.astype(F32)), to_head_major(cache_attn_v.astype(F32)))
    from_head_major = lambda t: jnp.transpose(
        t.reshape(depth, bp, H_ATT, HEAD_DIM, keep), (0, 1, 4, 2, 3))
    states = (state_ret.astype(F32), state_gla.astype(F32))
    outs = {k: [] for k in ("ret_p", "gla_p", "ret_s", "gla_s", "ks", "vs")}
    prompt_kv = None
    rows3 = lambda v: v.astype(F32).reshape(depth, 1, v.shape[-1])
    n1w, n2w, fnw = rows3(norm1_w), rows3(norm2_w), final_norm_w.astype(F32)[None, :]
    gnw, glw, ba = rows3(ret_gn_w), rows3(gla_norm_w), rows3(b_gla_a)
    w_in_t = jnp.swapaxes(w_in.astype(F32), 1, 2)
    w_a2 = w_gla_a2.astype(F32)
    bias_tab = rel_bias.astype(F32)
    wout, wup, wdown = w_out.astype(BF16), w_up.astype(BF16), w_down.astype(BF16)
    tabs_p = _rotary_tables(pos_p, lp)
    tabs_s = _rotary_tables(pos_s, bs * ls)
    for l in range(depth):
        last = l == depth - 1
        proj_p, proj_s = _in_proj(hp, hs, n1w, w_in_t, w_a2, ba, tabs_p, tabs_s, prompt_kv,
                                  layer=l, tm=tm_p, seq_len=lp)
        prompt_kv = proj_p[4:6]
        mix_p, s_ret, s_gla = _mixers(
            proj_p[:4], bias_tab, batch=bp, seq_len=lp, lc=CHUNK, n_chunks=ATT_REACH // CHUNK,
            layer=l, states=None, caches=None)
        outs["ret_p"].append(s_ret)
        outs["gla_p"].append(s_gla)
        mix_s, s_ret, s_gla = _mixers(
            proj_s[:4], bias_tab, batch=bs, seq_len=ls, lc=ls, n_chunks=1,
            layer=l, states=states, caches=caches)
        outs["ret_s"].append(s_ret)
        outs["gla_s"].append(s_gla)
        outs["ks"].append(proj_s[4].reshape(bs, ls, H_ATT, HEAD_DIM))
        outs["vs"].append(proj_s[5].reshape(bs, ls, H_ATT, HEAD_DIM))
        hp, hs = _out_ffn((hp,) + mix_p, (hs,) + mix_s, gnw, glw, n2w, fnw, ones,
                          wout, wup, wdown, layer=l, tm=tm_p, final_norm=last)
    y_prompt = hp.reshape(bp, lp, D_MODEL)
    y_sample = hs.reshape(bs, ls, D_MODEL)
    st = lambda k: jnp.stack(outs[k])
    return (y_prompt, y_sample, st("ret_p"), st("gla_p"),
            from_head_major(prompt_kv[0]), from_head_major(prompt_kv[1]),
            st("ret_s"), st("gla_s"), st("ks"), st("vs"))
```

```python
import functools

import numpy as np
import jax
import jax.numpy as jnp
from jax import lax
from jax.experimental import pallas as pl
from jax.experimental.pallas import tpu as pltpu

D_MODEL = 1024
CHUNK = 64
HEAD_DIM = 64
H_RET = 4
H_GLA = 4
H_ATT = 8
W_RET = H_RET * HEAD_DIM
W_GLA = H_GLA * HEAD_DIM
W_ATT = H_ATT * HEAD_DIM
GLA_RANK = 16
GLA_TAU = 16.0
ATT_REACH = 512
REL_MAX = 256
N_REL = CHUNK + REL_MAX
D_FF = 4 * D_MODEL
IN_COLS = 4 * W_RET + 4 * W_GLA + GLA_RANK + 3 * W_ATT
ROPE_BASE = 10000.0
EPS = 1e-6
PAST_LEN = 4096

LANES = 128
LOW_PAD = LANES
BIAS_W = 640
GLA_SAFE_LOG_DECAY = 50.0
ATT_SAFE_LOG2 = 80.0
NORM_ROUNDING_MARGIN = 1.05
FFN_ROWS_PER_STEP = 1024
ATT_ROWS_PER_STEP = 128
LOG2E = 1.4426950408889634
NEG_BIG = -1e30
VMEM_LIMIT = 60 * 1024 * 1024

F32 = jnp.float32
BF16 = jnp.bfloat16

_NT = (((1,), (1,)), ((), ()))
_TN = (((0,), (0,)), ((), ()))


def _dot(a, b):
    return jnp.dot(a, b, preferred_element_type=F32)


def _dot_nt(a, b):
    return lax.dot_general(a, b, _NT, preferred_element_type=F32)


def _dot_tn(a, b):
    return lax.dot_general(a, b, _TN, preferred_element_type=F32)


def _rmsnorm(x, w):
    return x * lax.rsqrt(jnp.mean(x * x, axis=-1, keepdims=True) + EPS) * w


def _params(n_grid_dims):
    return pltpu.CompilerParams(
        dimension_semantics=("arbitrary",) * n_grid_dims,
        vmem_limit_bytes=VMEM_LIMIT,
    )


def _layer_spec(a, layer):
    index = (layer,) + (0,) * (a.ndim - 1)
    return pl.BlockSpec((None,) + a.shape[1:], lambda *_: index, pipeline_mode=pl.Buffered(1))


def _block_diag_rows(x, n_heads):
    lane_head = lax.broadcasted_iota(jnp.int32, x.shape, 1) // HEAD_DIM
    zero = jnp.zeros_like(x)
    return jnp.concatenate(
        [jnp.where(lane_head == h, x, zero) for h in range(n_heads)], axis=0)


def _project_rows(x_ref, cos_ref, sin_ref, n1w_ref, ba_ref, weights,
                  ret_ref, gla_ref, lf_ref, att_ref):
    wret_ref, wgla_ref, watt_ref, wa2_ref = weights
    hn = _rmsnorm(x_ref[...], n1w_ref[...]).astype(BF16)
    cos = cos_ref[...]
    sin = sin_ref[...]
    first_half = (lax.broadcasted_iota(jnp.int32, cos.shape, 1) % HEAD_DIM) < (HEAD_DIM // 2)

    def rotary(t):
        swapped = jnp.where(first_half, -pltpu.roll(t, LANES - HEAD_DIM // 2, 1),
                            pltpu.roll(t, HEAD_DIM // 2, 1))
        return t * cos + swapped * sin

    scale = HEAD_DIM ** -0.5

    acc = _dot_nt(hn, wret_ref[...])
    for j in range(W_RET // LANES):
        sl = slice(j * LANES, (j + 1) * LANES)
        ret_ref[:, sl] = rotary(acc[:, sl]).astype(BF16)
        sk = slice(W_RET + j * LANES, W_RET + (j + 1) * LANES)
        ret_ref[:, sk] = (rotary(acc[:, sk]) * scale).astype(BF16)
    ret_ref[:, 2 * W_RET:3 * W_RET] = acc[:, 2 * W_RET:3 * W_RET].astype(BF16)
    g = acc[:, 3 * W_RET:4 * W_RET]
    ret_ref[:, 3 * W_RET:4 * W_RET] = (g * jax.nn.sigmoid(g)).astype(BF16)

    acc = _dot_nt(hn, wgla_ref[...])
    gla_ref[:, 0:W_GLA] = (acc[:, 0:W_GLA] * scale).astype(BF16)
    gla_ref[:, W_GLA:3 * W_GLA] = acc[:, W_GLA:3 * W_GLA].astype(BF16)
    g = acc[:, 3 * W_GLA:4 * W_GLA]
    gla_ref[:, 3 * W_GLA:4 * W_GLA] = (g * jax.nn.sigmoid(g)).astype(BF16)
    low = acc[:, 4 * W_GLA:4 * W_GLA + LOW_PAD].astype(BF16)
    z = _dot(low, wa2_ref[...]) + ba_ref[...]
    log_sig = jnp.minimum(z, 0.0) - jnp.log(1.0 + jnp.exp(-jnp.abs(z)))
    lf_ref[...] = log_sig * (1.0 / GLA_TAU)

    acc = _dot_nt(hn, watt_ref[...])
    att_ref[:, 0:W_ATT] = (acc[:, 0:W_ATT] * (scale * LOG2E)).astype(BF16)
    att_ref[:, W_ATT:3 * W_ATT] = acc[:, W_ATT:3 * W_ATT].astype(BF16)
    return acc[:, W_ATT:2 * W_ATT], acc[:, 2 * W_ATT:3 * W_ATT]


def _in_proj_kernel(*refs, n_long_tiles, tiles_per_seq, n_prev):
    refs = list(refs)
    take = lambda k: [refs.pop(0) for _ in range(k)]
    x_l, x_s, n1w_ref, win_ref, wa2in_ref, ba_ref, cos_l, sin_l, cos_s, sin_s = take(10)
    kprev_ref, vprev_ref = take(2) if n_prev else (None, None)
    outs_l = take(4)
    kc_l, vc_l = take(2)
    outs_s = take(4)
    kc_s, vc_s = take(2)
    weights = take(4)
    wret_ref, wgla_ref, watt_ref, wa2_ref = weights
    i = pl.program_id(0)

    @pl.when(i == 0)
    def _():
        n_main = 4 * W_RET + 4 * W_GLA
        wret_ref[...] = win_ref[0:4 * W_RET, :].astype(BF16)
        wgla_ref[0:4 * W_GLA, :] = win_ref[4 * W_RET:n_main, :].astype(BF16)
        wgla_ref[4 * W_GLA:4 * W_GLA + GLA_RANK, :] = (
            win_ref[n_main:n_main + GLA_RANK, :].astype(BF16))
        wgla_ref[4 * W_GLA + GLA_RANK:, :] = jnp.zeros((LOW_PAD - GLA_RANK, D_MODEL), BF16)
        watt_ref[...] = win_ref[n_main + GLA_RANK:IN_COLS, :].astype(BF16)
        wa2_ref[0:GLA_RANK, :] = wa2in_ref[...].astype(BF16)
        wa2_ref[GLA_RANK:, :] = jnp.zeros((LOW_PAD - GLA_RANK, W_GLA), BF16)

    @pl.when(i < n_long_tiles)
    def _():
        k32, v32 = _project_rows(x_l, cos_l, sin_l, n1w_ref, ba_ref, weights, *outs_l)

        @pl.when(i % tiles_per_seq == tiles_per_seq - 1)
        def _():
            if n_prev:
                kc_l[0:n_prev] = kprev_ref[...]
                vc_l[0:n_prev] = vprev_ref[...]
            kc_l[n_prev] = k32.T
            vc_l[n_prev] = v32.T

    @pl.when(i == n_long_tiles)
    def _():
        k32, v32 = _project_rows(x_s, cos_s, sin_s, n1w_ref, ba_ref, weights, *outs_s)
        kc_s[...] = k32
        vc_s[...] = v32


def _in_proj(x_long, x_short, n1w, w_in, w_a2, ba, tabs_long, tabs_short, prev_kv,
             *, layer, tm, seq_len):
    m = x_long.shape[0]
    ms = x_short.shape[0]
    n_tiles = m // tm
    tab_blocks = tabs_long[0].shape[0] // tm
    tiles_per_seq = seq_len // tm
    at_layer = functools.partial(_layer_spec, layer=layer)
    tile = lambda i: jnp.minimum(i, n_tiles - 1)
    row = lambda w: pl.BlockSpec((tm, w), lambda i: (tile(i), 0))
    tab = pl.BlockSpec((tm, LANES), lambda i: (tile(i) % tab_blocks, 0))
    whole = lambda w: pl.BlockSpec((ms, w), lambda i: (0, 0))
    n_prev = 0 if prev_kv is None else prev_kv[0].shape[0]
    stacked = lambda n: pl.BlockSpec((n, None, W_ATT, tm),
                                     lambda i: (0, tile(i) // tiles_per_seq, 0, 0))
    prev_specs = [stacked(n_prev)] * 2 if n_prev else []
    groups = (4 * W_RET, 4 * W_GLA, W_GLA, 3 * W_ATT)
    dtypes = (BF16, BF16, F32, BF16)
    kern = functools.partial(_in_proj_kernel, n_long_tiles=n_tiles, tiles_per_seq=tiles_per_seq,
                             n_prev=n_prev)
    outs = pl.pallas_call(
        kern,
        grid=(n_tiles + 1,),
        in_specs=[row(D_MODEL), whole(D_MODEL), at_layer(n1w), at_layer(w_in), at_layer(w_a2),
                  at_layer(ba), tab, tab, whole(LANES), whole(LANES)] + prev_specs,
        out_specs=([row(w) for w in groups] + [stacked(n_prev + 1)] * 2
                   + [whole(w) for w in groups] + [whole(W_ATT)] * 2),
        out_shape=([jax.ShapeDtypeStruct((m, w), d) for w, d in zip(groups, dtypes)]
                   + [jax.ShapeDtypeStruct((n_prev + 1, m // seq_len, W_ATT, tm), F32)] * 2
                   + [jax.ShapeDtypeStruct((ms, w), d) for w, d in zip(groups, dtypes)]
                   + [jax.ShapeDtypeStruct((ms, W_ATT), F32)] * 2),
        scratch_shapes=[pltpu.VMEM((4 * W_RET, D_MODEL), BF16),
                        pltpu.VMEM((4 * W_GLA + LOW_PAD, D_MODEL), BF16),
                        pltpu.VMEM((3 * W_ATT, D_MODEL), BF16),
                        pltpu.VMEM((LOW_PAD, W_GLA), BF16)],
        compiler_params=_params(1),
        name="in_proj",
    )(x_long, x_short, n1w, w_in, w_a2, ba, *tabs_long, *tabs_short,
      *(prev_kv if n_prev else ()))
    return outs[:6], outs[6:]


def _init_states(s_scr, s0_ref, n_seq, n_heads):
    if s0_ref is None:
        s_scr[...] = jnp.zeros_like(s_scr)
        return
    zero = jnp.zeros((HEAD_DIM, HEAD_DIM), F32)
    for i in range(n_seq):
        s_scr[i] = jnp.concatenate(
            [jnp.concatenate([s0_ref[i, h] if j == h else zero for j in range(n_heads)], axis=1)
             for h in range(n_heads)], axis=0)


def _store_states(s_scr, out_ref, n_seq, n_heads):
    for i in range(n_seq):
        s = s_scr[i]
        for h in range(n_heads):
            sl = slice(h * HEAD_DIM, (h + 1) * HEAD_DIM)
            out_ref[i, h] = s[sl, sl]


def _chunk_rows(lc, n_chunks, n_seq):
    return [[slice((i * n_chunks + c) * lc, (i * n_chunks + c + 1) * lc)
             for c in range(n_chunks)] for i in range(n_seq)]


def _retention_body(q_ref, k_ref, v_ref, s0_ref, tab_refs, o_ref, sfin_ref, s_scr,
                    *, lc, n_chunks, n_seq):
    dtab_ref, qdec_ref, kdec_ref, gtab_ref, bd_ref = tab_refs
    g = pl.program_id(1)

    @pl.when(g == 0)
    def _():
        _init_states(s_scr, s0_ref, n_seq, H_RET)

    dtab = dtab_ref[...]
    q = q_ref[...]
    k = k_ref[...]
    v = v_ref[...]
    kd = (k.astype(F32) * kdec_ref[...]).astype(BF16)
    gtab = gtab_ref[...]
    bd = bd_ref[...]
    for i, rows in enumerate(_chunk_rows(lc, n_chunks, n_seq)):
        states = [s_scr[i]]
        for r in rows:
            upd = _dot_tn(kd[r, :], v[r, :])
            states.append(states[-1] * gtab + upd * bd)
        s_scr[i] = states[-1]
        for c, r in enumerate(rows):
            scores = _dot_nt(q[r, :], _block_diag_rows(k[r, :], H_RET))
            p = (scores * dtab).astype(BF16)
            intra = _dot(p, _block_diag_rows(v[r, :], H_RET))
            inter = _dot(q[r, :], states[c].astype(BF16))
            o_ref[r, :] = intra + inter * qdec_ref[r, :]

    @pl.when(g == pl.num_programs(1) - 1)
    def _():
        _store_states(s_scr, sfin_ref, n_seq, H_RET)


def _retention_tables(lc, n_chunks):
    lg = np.log1p(-np.exp2(-5.0 - np.arange(H_RET, dtype=np.float64)))
    idx = np.arange(lc)
    diff = idx[:, None] - idx[None, :]
    causal = diff >= 0
    dpos = np.where(causal, diff, 0).astype(np.float64)
    decay = np.where(causal[None], np.exp(dpos[None] * lg[:, None, None]), 0.0)
    dtab = np.transpose(decay, (1, 0, 2)).reshape(lc, H_RET * lc)
    q_decay = np.exp((idx + 1).astype(np.float64)[:, None] * lg[None, :])
    k_decay = np.exp((lc - 1 - idx).astype(np.float64)[:, None] * lg[None, :])
    qdec = np.tile(np.repeat(q_decay, HEAD_DIM, axis=1), (n_chunks, 1))
    kdec = np.tile(np.repeat(k_decay, HEAD_DIM, axis=1), (n_chunks, 1))
    gtab = np.broadcast_to(np.repeat(np.exp(lc * lg), HEAD_DIM)[:, None], (W_RET, W_RET))
    head = np.arange(W_RET) // HEAD_DIM
    bd = head[:, None] == head[None, :]
    return tuple(jnp.asarray(t, F32) for t in (dtab, qdec, kdec, gtab, bd))


def _split_bf16(x):
    hi = x.astype(BF16)
    lo = (x - hi.astype(F32)).astype(BF16)
    return jnp.concatenate([hi, lo], axis=1)


def _group_reference(b, row, half):
    n, w = b.shape
    group = 2 * half
    if group >= 8:
        b3 = b.reshape(n // group, group, w)
        return jnp.broadcast_to(b3[:, half - 1:half, :], b3.shape).reshape(n, w)
    if group == 4:
        i = row & 3
        return jnp.where(i == 0, pltpu.roll(b, n - 1, 0),
                         jnp.where(i == 1, b,
                                   jnp.where(i == 2, pltpu.roll(b, 1, 0), pltpu.roll(b, 2, 0))))
    return jnp.where((row & 1) == 1, pltpu.roll(b, 1, 0), b)


def _gla_body(q_ref, k_ref, v_ref, lf_ref, s0_ref, tab_refs, o_ref, sfin_ref, s_scr, sc_scr,
              *, lc, n_chunks, n_seq):
    tri_ref, bd_ref, onehot_ref = tab_refs
    g = pl.program_id(1)

    @pl.when(g == 0)
    def _():
        _init_states(s_scr, s0_ref, n_seq, H_GLA)

    n_lev = lc.bit_length() - 1
    w = W_GLA
    n = lc * n_chunks * n_seq
    seq_rows = _chunk_rows(lc, n_chunks, n_seq)
    chunk_rows = [r for rows in seq_rows for r in rows]
    row = lax.broadcasted_iota(jnp.int32, (n, w), 0)
    trow = lax.broadcasted_iota(jnp.int32, (lc, H_GLA * lc), 0)
    scol = lax.broadcasted_iota(jnp.int32, (lc, H_GLA * lc), 1) & (lc - 1)
    q = q_ref[...]
    k = k_ref[...]
    v = v_ref[...]
    q32 = q.astype(F32)
    k32 = k.astype(F32)
    tri = tri_ref[...]
    lfs = _split_bf16(lf_ref[...])
    cs = jnp.concatenate([_dot(tri, lfs[r, :]) for r in chunk_rows], axis=0)
    b = cs[:, :w] + cs[:, w:]

    qe = (q32 * jnp.exp(b)).astype(BF16)
    mild = jnp.min(b) >= -GLA_SAFE_LOG_DECAY

    @pl.when(mild)
    def _():
        ke = (k32 * jnp.exp(-b)).astype(BF16)
        for r in chunk_rows:
            sc = _dot_nt(qe[r, :], _block_diag_rows(ke[r, :], H_GLA))
            sc_scr[r, :] = jnp.where(trow >= scol, sc, 0.0)

    @pl.when(jnp.logical_not(mild))
    def _():
        for r in chunk_rows:
            sc = _dot_nt(q[r, :], _block_diag_rows(k[r, :], H_GLA))
            sc_scr[r, :] = jnp.where(trow == scol, sc, 0.0)
        for lev in range(n_lev):
            half = lc >> (lev + 1)
            shift = half.bit_length()
            upper = (row & half) != 0
            e = jnp.exp(-jnp.abs(b - _group_reference(b, row, half)))
            z = (jnp.where(upper, q32, k32) * e).astype(BF16)
            valid = (((trow >> shift) == (scol >> shift))
                     & ((trow & half) != 0) & ((scol & half) == 0))
            for r in chunk_rows:
                zc = z[r, :]
                sc = _dot_nt(zc, _block_diag_rows(zc, H_GLA))
                sc_scr[r, :] = jnp.where(valid, sc, sc_scr[r, :])

    b3 = b.reshape(n_chunks * n_seq, lc, w)
    b_last = b3[:, lc - 1:lc, :]
    kd = (k32 * jnp.exp(jnp.broadcast_to(b_last, b3.shape).reshape(n, w) - b)).astype(BF16)
    col_sum = _dot_tn(lfs, onehot_ref[...])
    decay_col = jnp.exp(col_sum[:w, :] + col_sum[w:, :])
    bd = bd_ref[...]
    for i, rows in enumerate(seq_rows):
        states = [s_scr[i]]
        for c, r in enumerate(rows):
            j = i * n_chunks + c
            upd = _dot_tn(kd[r, :], v[r, :])
            states.append(states[-1] * decay_col[:, j:j + 1] + upd * bd)
        s_scr[i] = states[-1]
        for c, r in enumerate(rows):
            inter = _dot(qe[r, :], states[c].astype(BF16))
            intra = _dot(sc_scr[r, :].astype(BF16), _block_diag_rows(v[r, :], H_GLA))
            o_ref[r, :] = intra + inter

    @pl.when(g == pl.num_programs(1) - 1)
    def _():
        _store_states(s_scr, sfin_ref, n_seq, H_GLA)


def _linear_mixers_kernel(*refs, lc, n_chunks, n_seq, has_init):
    refs = list(refs)
    take = lambda k: [refs.pop(0) for _ in range(k)]
    rq, rk, rv, gq, gk, gv, lf = take(7)
    s0r, s0g = take(2) if has_init else (None, None)
    ret_tabs = take(5)
    gla_tabs = take(3)
    oa_ref, sr_ref, ob_ref, sg_ref, sr_scr, sg_scr, sc_scr = take(7)
    dims = dict(lc=lc, n_chunks=n_chunks, n_seq=n_seq)
    _retention_body(rq, rk, rv, s0r, ret_tabs, oa_ref, sr_ref, sr_scr, **dims)
    _gla_body(gq, gk, gv, lf, s0g, gla_tabs, ob_ref, sg_ref, sg_scr, sc_scr, **dims)


def _linear_mixers(ret, gla, lf, states, layer, *, batch, seq_len, lc, n_chunks, n_seq):
    rows = lc * n_chunks * n_seq
    ng = seq_len * n_seq // rows
    m = batch * seq_len
    width = W_RET
    col = lambda c: pl.BlockSpec((rows, width), lambda b, g: (b * ng + g, c))
    full = lambda a: pl.BlockSpec(a.shape, lambda b, g: (0,) * a.ndim)
    blk = (n_seq, H_RET, HEAD_DIM, HEAD_DIM)
    s_out_spec = pl.BlockSpec(blk, lambda b, g: (b, 0, 0, 0))
    s_out_shape = jax.ShapeDtypeStruct((batch, H_RET, HEAD_DIM, HEAD_DIM), F32)
    s_in_specs = []
    if states is not None:
        s_in_specs = [pl.BlockSpec((None,) + blk, lambda b, g: (layer, b, 0, 0, 0))] * 2
    ret_tabs = _retention_tables(lc, n_chunks * n_seq)
    gla_tabs = _gla_tables(lc, n_chunks * n_seq)
    kern = functools.partial(_linear_mixers_kernel, lc=lc, n_chunks=n_chunks, n_seq=n_seq,
                             has_init=states is not None)
    out_rows = pl.BlockSpec((rows, width), lambda b, g: (b * ng + g, 0))
    return pl.pallas_call(
        kern,
        grid=(batch // n_seq, ng),
        in_specs=([col(0), col(1), col(2), col(0), col(1), col(2), col(0)] + s_in_specs
                  + [full(t) for t in ret_tabs + gla_tabs]),
        out_specs=[out_rows, s_out_spec, out_rows, s_out_spec],
        out_shape=[jax.ShapeDtypeStruct((m, width), F32), s_out_shape,
                   jax.ShapeDtypeStruct((m, width), F32), s_out_shape],
        scratch_shapes=[pltpu.VMEM((n_seq, width, width), F32),
                        pltpu.VMEM((n_seq, width, width), F32),
                        pltpu.VMEM((rows, H_GLA * lc), F32)],
        compiler_params=_params(2),
        name="linear_mixers",
    )(ret, ret, ret, gla, gla, gla, lf, *(states or ()), *ret_tabs, *gla_tabs)


def _gla_tables(lc, n_chunks):
    idx = np.arange(lc)
    tri = idx[:, None] >= idx[None, :]
    head = np.arange(W_GLA) // HEAD_DIM
    bd = head[:, None] == head[None, :]
    chunk_of_row = np.arange(lc * n_chunks) // lc
    onehot = chunk_of_row[:, None] == np.arange(LANES)[None, :]
    return jnp.asarray(tri, BF16), jnp.asarray(bd, F32), jnp.asarray(onehot, BF16)


def _band_attn_kernel(q_ref, kc_ref, vc_ref, gv_ref, hones_ref, o_ref,
                      kwin_t, vwin, bias_scr, s_scr, m_scr, ksq_prev, *, lc, rpi, n_iter):
    b_id = pl.program_id(0)
    g = pl.program_id(1)
    win = ATT_REACH + rpi
    rows = rpi * n_iter

    @pl.when((b_id == 0) & (g == 0))
    def _():
        kwin_t[:, 0:rows] = jnp.zeros((W_ATT, rows), BF16)
        vwin[0:rows, :] = jnp.zeros((rows, W_ATT), BF16)
        ksq_prev[0] = 0.0
        row = lax.broadcasted_iota(jnp.int32, (rpi, BIAS_W), 0)
        rel = lax.broadcasted_iota(jnp.int32, (rpi, BIAS_W), 1) - (row // lc) * lc
        in_band = (rel >= 0) & (rel < ATT_REACH + lc)
        for h in range(H_ATT):
            x = jnp.broadcast_to(gv_ref[h:h + 1, :], (rpi, BIAS_W))
            x = pltpu.roll(x, BIAS_W - (lc - 1), 1, stride=1, stride_axis=0)
            bias_scr[h // 2, (h % 2) * rpi:(h % 2 + 1) * rpi, :] = jnp.where(
                in_band, x * LOG2E, NEG_BIG)

    base = pl.multiple_of(g * rows, rows)
    k_cur = kc_ref[...].astype(F32)
    kwin_t[:, pl.ds(base + rows, rows)] = k_cur.T.astype(BF16)
    vwin[pl.ds(base + rows, rows), :] = vc_ref[...]

    head_ones = hones_ref[...]
    q_cur = q_ref[...].astype(F32)
    q_sq = jnp.max(_dot((q_cur * q_cur).astype(BF16), head_ones))
    k_sq_cur = jnp.max(_dot((k_cur * k_cur).astype(BF16), head_ones))
    k_sq = jnp.maximum(k_sq_cur, jnp.where(g == 0, 0.0, ksq_prev[0]))
    ksq_prev[0] = k_sq_cur
    room = ATT_SAFE_LOG2 - jnp.max(jnp.abs(gv_ref[...])) * LOG2E
    bounded = (room > 0.0) & (q_sq * k_sq * NORM_ROUNDING_MARGIN <= room * room)

    lane = lax.broadcasted_iota(jnp.int32, (rpi, LANES), 1)
    lane2 = lax.broadcasted_iota(jnp.int32, (2 * rpi, LANES), 1)
    head2 = lax.broadcasted_iota(jnp.int32, (2 * rpi, LANES), 0) // rpi
    col = lax.broadcasted_iota(jnp.int32, (2 * rpi, win), 1)

    def step(i, carry, *, masked, small):
        r0 = pl.multiple_of(i * rpi, rpi)
        first_valid = ATT_REACH - (g * rows + r0)

        def scores(pair):
            lanes = slice(pair * LANES, (pair + 1) * LANES)
            qp = q_ref[pl.ds(r0, rpi), lanes]
            q2 = jnp.concatenate([qp, qp], axis=0)
            q2 = jnp.where((lane2 // HEAD_DIM) == head2, q2, jnp.zeros_like(q2))
            s = _dot(q2, kwin_t[lanes, pl.ds(base + r0, win)]) + bias_scr[pair][:, :win]
            if masked:
                s = jnp.where(col >= first_valid, s, NEG_BIG)
            return s

        def finish(pair, e):
            lanes = slice(pair * LANES, (pair + 1) * LANES)
            den = jnp.sum(e, axis=-1, keepdims=True)
            pv = _dot(e.astype(BF16), vwin[pl.ds(base + r0, win), lanes]) / den
            o_ref[pl.ds(r0, rpi), lanes] = jnp.where(
                lane < HEAD_DIM, pv[:rpi, :], pv[rpi:, :]).astype(BF16)

        if small:
            for pair in range(H_ATT // 2):
                finish(pair, jnp.exp2(scores(pair)))
        else:
            for pair in range(H_ATT // 2):
                s = scores(pair)
                s_scr[pair, :, :win] = s
                m_scr[pair] = jnp.broadcast_to(
                    jnp.max(s, axis=-1, keepdims=True), (2 * rpi, LANES))
            for pair in range(H_ATT // 2):
                finish(pair, jnp.exp2(s_scr[pair, :, :win] - m_scr[pair][:, :1]))
        return carry

    def run(masked):
        @pl.when(bounded)
        def _():
            lax.fori_loop(0, n_iter, functools.partial(step, masked=masked, small=True), 0)

        @pl.when(jnp.logical_not(bounded))
        def _():
            lax.fori_loop(0, n_iter, functools.partial(step, masked=masked, small=False), 0)

    @pl.when(g == 0)
    def _():
        run(True)

    @pl.when(g != 0)
    def _():
        run(False)


def _band_attn(att, gv, layer, *, batch, seq_len, lc, rpi, n_iter):
    rows = rpi * n_iter
    assert rows == ATT_REACH, "the carried window is exactly one row group"
    ng = seq_len // rows
    m = batch * seq_len
    cur = lambda c: pl.BlockSpec((rows, W_ATT), lambda b, g: (b * ng + g, c))
    kern = functools.partial(_band_attn_kernel, lc=lc, rpi=rpi, n_iter=n_iter)
    head_ones = _head_ones()
    return pl.pallas_call(
        kern,
        grid=(batch, ng),
        in_specs=[cur(0), cur(1), cur(2), _layer_spec(gv, layer),
                  pl.BlockSpec(head_ones.shape, lambda b, g: (0, 0))],
        out_specs=pl.BlockSpec((rows, W_ATT), lambda b, g: (b * ng + g, 0)),
        out_shape=jax.ShapeDtypeStruct((m, W_ATT), BF16),
        scratch_shapes=[pltpu.VMEM((W_ATT, (ng + 1) * rows), BF16),
                        pltpu.VMEM(((ng + 1) * rows, W_ATT), BF16),
                        pltpu.VMEM((H_ATT // 2, 2 * rpi, BIAS_W), F32),
                        pltpu.VMEM((H_ATT // 2, 2 * rpi, BIAS_W), F32),
                        pltpu.VMEM((H_ATT // 2, 2 * rpi, LANES), F32),
                        pltpu.SMEM((1,), F32)],
        compiler_params=_params(2),
        name="band_attn",
    )(att, att, att, gv, head_ones)


def _sample_attn_kernel(q_ref, kn_ref, vn_ref, kt_ref, vt_ref, gv_ref, hones_ref, o_ref, *, ls):
    win = ATT_REACH + ls
    lane = lax.broadcasted_iota(jnp.int32, (ls, LANES), 1)
    lane2 = lax.broadcasted_iota(jnp.int32, (2 * ls, LANES), 1)
    head2 = lax.broadcasted_iota(jnp.int32, (2 * ls, LANES), 0) // ls

    head_ones = hones_ref[...]
    q32 = q_ref[...].astype(F32)
    kn32 = kn_ref[...].astype(F32)
    kc32 = kt_ref[...]
    q_sq = jnp.max(_dot((q32 * q32).astype(BF16), head_ones))
    k_sq = jnp.maximum(jnp.max(_dot((kn32 * kn32).astype(BF16), head_ones)),
                       jnp.max(jnp.sum(kc32 * kc32, axis=1)))
    room = ATT_SAFE_LOG2 - jnp.max(jnp.abs(gv_ref[...])) * LOG2E
    bounded = (room > 0.0) & (q_sq * k_sq * NORM_ROUNDING_MARGIN <= room * room)

    def head_pair(pair, small):
        lanes = slice(pair * LANES, (pair + 1) * LANES)
        qp = q_ref[:, lanes]
        q2 = jnp.concatenate([qp, qp], axis=0)
        q2 = jnp.where((lane2 // HEAD_DIM) == head2, q2, jnp.zeros_like(q2))
        kt = kt_ref[2 * pair:2 * pair + 2].reshape(2 * HEAD_DIM, ATT_REACH).astype(BF16)
        vt = vt_ref[2 * pair:2 * pair + 2].reshape(2 * HEAD_DIM, ATT_REACH).astype(BF16)
        bias = jnp.concatenate(
            [pltpu.roll(jnp.broadcast_to(gv_ref[h:h + 1, :], (ls, BIAS_W)),
                        BIAS_W - (ls - 1), 1, stride=1, stride_axis=0)
             for h in (2 * pair, 2 * pair + 1)], axis=0) * LOG2E
        s_old = _dot(q2, kt) + bias[:, :ATT_REACH]
        s_new = _dot_nt(q2, kn_ref[:, lanes]) + bias[:, ATT_REACH:win]
        if not small:
            mx = jnp.maximum(jnp.max(s_old, axis=-1, keepdims=True),
                             jnp.max(s_new, axis=-1, keepdims=True))
            s_old = s_old - mx
            s_new = s_new - mx
        e_old = jnp.exp2(s_old)
        e_new = jnp.exp2(s_new)
        den = jnp.sum(e_old, axis=-1, keepdims=True) + jnp.sum(e_new, axis=-1, keepdims=True)
        pv = (_dot_nt(e_old.astype(BF16), vt) + _dot(e_new.astype(BF16), vn_ref[:, lanes])) / den
        o_ref[:, lanes] = jnp.where(lane < HEAD_DIM, pv[:ls, :], pv[ls:, :]).astype(BF16)

    @pl.when(bounded)
    def _():
        for pair in range(H_ATT // 2):
            head_pair(pair, True)

    @pl.when(jnp.logical_not(bounded))
    def _():
        for pair in range(H_ATT // 2):
            head_pair(pair, False)


def _head_ones():
    lane_head = np.arange(W_ATT) // HEAD_DIM
    return jnp.asarray(lane_head[:, None] == np.arange(LANES)[None, :], BF16)


def _sample_attn(att, cache_kt, cache_vt, layer, gv, *, batch, ls):
    new = lambda c: pl.BlockSpec((ls, W_ATT), lambda b: (b, c))
    old = pl.BlockSpec((None, None, H_ATT, HEAD_DIM, ATT_REACH), lambda b: (layer, b, 0, 0, 0))
    head_ones = _head_ones()
    kern = functools.partial(_sample_attn_kernel, ls=ls)
    return pl.pallas_call(
        kern,
        grid=(batch,),
        in_specs=[new(0), new(1), new(2), old, old, _layer_spec(gv, layer),
                  pl.BlockSpec(head_ones.shape, lambda b: (0, 0))],
        out_specs=pl.BlockSpec((ls, W_ATT), lambda b: (b, 0)),
        out_shape=jax.ShapeDtypeStruct((batch * ls, W_ATT), BF16),
        compiler_params=_params(1),
        name="sample_attn",
    )(att, att, att, cache_kt, cache_vt, gv, head_ones)


def _bias_rows(rel_bias_l, lc):
    wv = np.arange(BIAS_W)
    dist = ATT_REACH + lc - 1 - wv
    ridx = np.clip(dist, -(CHUNK - 1), REL_MAX) + (CHUNK - 1)
    n_far = int(np.argmax(ridx < N_REL - 1)) - 1
    n_near = BIAS_W - n_far - N_REL
    assert np.array_equal(
        ridx, np.concatenate([np.full(n_far, N_REL - 1), np.arange(N_REL - 1, -1, -1),
                              np.zeros(n_near, np.int64)]))
    lead = rel_bias_l.shape[:-1]
    return jnp.concatenate(
        [jnp.broadcast_to(rel_bias_l[..., N_REL - 1:], lead + (n_far,)),
         jnp.flip(rel_bias_l, axis=-1),
         jnp.broadcast_to(rel_bias_l[..., :1], lead + (n_near,))], axis=-1)


def _mix_rows(x_ref, oa_ref, ob_ref, oc_ref, sga_ref, sgb_ref, params, y_ref, *, final_norm):
    gnw_ref, glw_ref, n2w_ref, fnw_ref, ones_ref, wout_ref, wup_ref, wdown_ref = params
    ones = ones_ref[...]

    def head_sum(t):
        return _dot(t.astype(BF16), ones)

    inv_d = 1.0 / HEAD_DIM
    oa = oa_ref[...]
    mu = head_sum(oa) * inv_d
    da = oa - mu
    var = head_sum(da * da) * inv_d
    a = da * lax.rsqrt(var + EPS) * gnw_ref[...] * sga_ref[...].astype(F32)
    ob = ob_ref[...]
    ms = head_sum(ob * ob) * inv_d
    bb = ob * lax.rsqrt(ms + EPS) * glw_ref[...] * sgb_ref[...].astype(F32)
    cat = jnp.concatenate([a.astype(BF16), bb.astype(BF16), oc_ref[...]], axis=1)
    h1 = x_ref[...] + _dot(cat, wout_ref[...])
    hn = _rmsnorm(h1, n2w_ref[...]).astype(BF16)
    acc = jnp.zeros_like(h1)
    n_slab = D_FF // D_MODEL
    for j in range(n_slab):
        sl = slice(j * D_MODEL, (j + 1) * D_MODEL)
        u = jnp.maximum(_dot(hn, wup_ref[:, sl]), 0.0)
        acc = acc + _dot((u * u).astype(BF16), wdown_ref[sl, :])
    h2 = h1 + acc
    if final_norm:
        h2 = _rmsnorm(h2, fnw_ref[...])
    y_ref[...] = h2


def _out_ffn_kernel(*refs, n_long_tiles, final_norm):
    rows_l, rows_s, params = refs[0:6], refs[6:12], refs[12:20]
    y_l, y_s = refs[20:22]
    i = pl.program_id(0)

    @pl.when(i < n_long_tiles)
    def _():
        _mix_rows(*rows_l, params, y_l, final_norm=final_norm)

    @pl.when(i == n_long_tiles)
    def _():
        _mix_rows(*rows_s, params, y_s, final_norm=final_norm)


def _out_ffn(long_in, short_in, gnw, glw, n2w, fnw, ones, wout, wup, wdown,
             *, layer, tm, final_norm):
    m = long_in[0].shape[0]
    ms = short_in[0].shape[0]
    n_tiles = m // tm
    full = lambda a: pl.BlockSpec(a.shape, lambda i: (0,) * a.ndim)
    at_layer = functools.partial(_layer_spec, layer=layer)
    tile = lambda i: jnp.minimum(i, n_tiles - 1)
    row = lambda w, c=0: pl.BlockSpec((tm, w), lambda i: (tile(i), c))
    whole = lambda w, c=0: pl.BlockSpec((ms, w), lambda i: (0, c))
    row_specs = lambda spec: [spec(D_MODEL), spec(W_RET), spec(W_GLA), spec(W_ATT),
                              spec(W_RET, 3), spec(W_GLA, 3)]
    kern = functools.partial(_out_ffn_kernel, n_long_tiles=n_tiles, final_norm=final_norm)
    return pl.pallas_call(
        kern,
        grid=(n_tiles + 1,),
        in_specs=(row_specs(row) + row_specs(whole)
                  + [at_layer(gnw), at_layer(glw), at_layer(n2w), full(fnw), full(ones),
                     at_layer(wout), at_layer(wup), at_layer(wdown)]),
        out_specs=[row(D_MODEL), whole(D_MODEL)],
        out_shape=[jax.ShapeDtypeStruct((m, D_MODEL), F32),
                   jax.ShapeDtypeStruct((ms, D_MODEL), F32)],
        compiler_params=_params(1),
        name="out_ffn",
    )(*long_in, *short_in, gnw, glw, n2w, fnw, ones, wout, wup, wdown)


def _rotary_tables(pos, rows):
    half = HEAD_DIM // 2
    inv_freq = ROPE_BASE ** (-np.arange(half, dtype=np.float64) / half)
    ang = np.asarray(pos, np.float64)[:, None] * inv_freq[None, :]
    reps = (rows // len(pos), LANES // half)
    return (jnp.asarray(np.tile(np.cos(ang), reps), F32),
            jnp.asarray(np.tile(np.sin(ang), reps), F32))


def _mixers(proj, rel_bias, *, batch, seq_len, lc, n_chunks, layer, states, caches):
    ret, gla, lf, att = proj
    n_seq = batch if lc * n_chunks == seq_len else 1
    oa, s_ret, ob, s_gla = _linear_mixers(
        ret, gla, lf, states, layer, batch=batch, seq_len=seq_len, lc=lc, n_chunks=n_chunks,
        n_seq=n_seq)
    gv = _bias_rows(rel_bias, lc)
    if caches is None:
        rpi = min(lc * n_chunks, ATT_ROWS_PER_STEP)
        oc = _band_attn(att, gv, layer, batch=batch, seq_len=seq_len, lc=lc, rpi=rpi,
                        n_iter=lc * n_chunks // rpi)
    else:
        oc = _sample_attn(att, caches[0], caches[1], layer, gv, batch=batch, ls=seq_len)
    return (oa, ob, oc, ret, gla), s_ret, s_gla


def kernel(x_prompt, x_sample, state_ret, state_gla, cache_attn_k, cache_attn_v, norm1_w, norm2_w, final_norm_w, w_in, w_gla_a2, b_gla_a, ret_gn_w, gla_norm_w, rel_bias, w_out, w_up, w_down):
    depth = w_in.shape[0]
    bp, lp, _ = x_prompt.shape
    bs, ls, _ = x_sample.shape
    keep = min(ATT_REACH, lp)
    head = np.arange(W_RET) // HEAD_DIM
    ones = jnp.asarray(head[:, None] == head[None, :], BF16)
    hp = x_prompt.reshape(bp * lp, D_MODEL)
    hs = x_sample.reshape(bs * ls, D_MODEL)
    pos_p = np.arange(lp)
    pos_s = PAST_LEN + np.arange(ls)
    tm_p = keep
    to_head_major = lambda c: jnp.transpose(c, (0, 1, 3, 4, 2))
    caches = (to_head_major(cache_attn_k.astype(F32)), to_head_major(cache_attn_v.astype(F32)))
    from_head_major = lambda t: jnp.transpose(
        t.reshape(depth, bp, H_ATT, HEAD_DIM, keep), (0, 1, 4, 2, 3))
    states = (state_ret.astype(F32), state_gla.astype(F32))
    outs = {k: [] for k in ("ret_p", "gla_p", "ret_s", "gla_s", "ks", "vs")}
    prompt_kv = None
    rows3 = lambda v: v.astype(F32).reshape(depth, 1, v.shape[-1])
    n1w, n2w, fnw = rows3(norm1_w), rows3(norm2_w), final_norm_w.astype(F32)[None, :]
    gnw, glw, ba = rows3(ret_gn_w), rows3(gla_norm_w), rows3(b_gla_a)
    w_in_t = jnp.swapaxes(w_in.astype(F32), 1, 2)
    w_a2 = w_gla_a2.astype(F32)
    bias_tab = rel_bias.astype(F32)
    wout, wup, wdown = w_out.astype(BF16), w_up.astype(BF16), w_down.astype(BF16)
    tabs_p = _rotary_tables(pos_p, lp)
    tabs_s = _rotary_tables(pos_s, bs * ls)
    for l in range(depth):
        last = l == depth - 1
        proj_p, proj_s = _in_proj(hp, hs, n1w, w_in_t, w_a2, ba, tabs_p, tabs_s, prompt_kv,
                                  layer=l, tm=tm_p, seq_len=lp)
        prompt_kv = proj_p[4:6]
        mix_p, s_ret, s_gla = _mixers(
            proj_p[:4], bias_tab, batch=bp, seq_len=lp, lc=CHUNK, n_chunks=ATT_REACH // CHUNK,
            layer=l, states=None, caches=None)
        outs["ret_p"].append(s_ret)
        outs["gla_p"].append(s_gla)
        mix_s, s_ret, s_gla = _mixers(
            proj_s[:4], bias_tab, batch=bs, seq_len=ls, lc=ls, n_chunks=1,
            layer=l, states=states, caches=caches)
        outs["ret_s"].append(s_ret)
        outs["gla_s"].append(s_gla)
        outs["ks"].append(proj_s[4].reshape(bs, ls, H_ATT, HEAD_DIM))
        outs["vs"].append(proj_s[5].reshape(bs, ls, H_ATT, HEAD_DIM))
        hp, hs = _out_ffn((hp,) + mix_p, (hs,) + mix_s, gnw, glw, n2w, fnw, ones,
                          wout, wup, wdown, layer=l, tm=FFN_ROWS_PER_STEP, final_norm=last)
    y_prompt = hp.reshape(bp, lp, D_MODEL)
    y_sample = hs.reshape(bs, ls, D_MODEL)
    st = lambda k: jnp.stack(outs[k])
    return (y_prompt, y_sample, st("ret_p"), st("gla_p"),
            from_head_major(prompt_kv[0]), from_head_major(prompt_kv[1]),
            st("ret_s"), st("gla_s"), st("ks"), st("vs"))
```

```python
import functools

import numpy as np
import jax
import jax.numpy as jnp
from jax import lax
from jax.experimental import pallas as pl
from jax.experimental.pallas import tpu as pltpu

D_MODEL = 1024
CHUNK = 64
HEAD_DIM = 64
H_RET = 4
H_GLA = 4
H_ATT = 8
W_RET = H_RET * HEAD_DIM
W_GLA = H_GLA * HEAD_DIM
W_ATT = H_ATT * HEAD_DIM
GLA_RANK = 16
GLA_TAU = 16.0
ATT_REACH = 512
REL_MAX = 256
N_REL = CHUNK + REL_MAX
D_FF = 4 * D_MODEL
IN_COLS = 4 * W_RET + 4 * W_GLA + GLA_RANK + 3 * W_ATT
ROPE_BASE = 10000.0
EPS = 1e-6
PAST_LEN = 4096

LANES = 128
LOW_PAD = LANES
BIAS_W = 640
GLA_SAFE_LOG_DECAY = 50.0
ATT_SAFE_LOG2 = 80.0
NORM_ROUNDING_MARGIN = 1.05
ATT_ROWS_PER_STEP = 128
LOG2E = 1.4426950408889634
NEG_BIG = -1e30
VMEM_LIMIT = 58 * 1024 * 1024

F32 = jnp.float32
BF16 = jnp.bfloat16

_NT = (((1,), (1,)), ((), ()))
_TN = (((0,), (0,)), ((), ()))


def _dot(a, b):
    return jnp.dot(a, b, preferred_element_type=F32)


def _dot_nt(a, b):
    return lax.dot_general(a, b, _NT, preferred_element_type=F32)


def _dot_tn(a, b):
    return lax.dot_general(a, b, _TN, preferred_element_type=F32)


def _rmsnorm(x, w):
    return x * lax.rsqrt(jnp.mean(x * x, axis=-1, keepdims=True) + EPS) * w


def _params(n_grid_dims):
    return pltpu.CompilerParams(
        dimension_semantics=("arbitrary",) * n_grid_dims,
        vmem_limit_bytes=VMEM_LIMIT,
    )


def _layer_spec(a, layer):
    index = (layer,) + (0,) * (a.ndim - 1)
    return pl.BlockSpec((None,) + a.shape[1:], lambda *_: index, pipeline_mode=pl.Buffered(1))


def _block_diag_rows(x, n_heads):
    lane_head = lax.broadcasted_iota(jnp.int32, x.shape, 1) // HEAD_DIM
    zero = jnp.zeros_like(x)
    return jnp.concatenate(
        [jnp.where(lane_head == h, x, zero) for h in range(n_heads)], axis=0)


def _project_rows(x_ref, cos_ref, sin_ref, n1w_ref, ba_ref, weights,
                  ret_ref, gla_ref, lf_ref, att_ref):
    wret_ref, wgla_ref, watt_ref, wa2_ref = weights
    hn = _rmsnorm(x_ref[...], n1w_ref[...]).astype(BF16)
    cos = cos_ref[...]
    sin = sin_ref[...]
    first_half = (lax.broadcasted_iota(jnp.int32, cos.shape, 1) % HEAD_DIM) < (HEAD_DIM // 2)

    def rotary(t):
        swapped = jnp.where(first_half, -pltpu.roll(t, LANES - HEAD_DIM // 2, 1),
                            pltpu.roll(t, HEAD_DIM // 2, 1))
        return t * cos + swapped * sin

    scale = HEAD_DIM ** -0.5

    acc = _dot_nt(hn, wret_ref[...])
    for j in range(W_RET // LANES):
        sl = slice(j * LANES, (j + 1) * LANES)
        ret_ref[:, sl] = rotary(acc[:, sl]).astype(BF16)
        sk = slice(W_RET + j * LANES, W_RET + (j + 1) * LANES)
        ret_ref[:, sk] = (rotary(acc[:, sk]) * scale).astype(BF16)
    ret_ref[:, 2 * W_RET:3 * W_RET] = acc[:, 2 * W_RET:3 * W_RET].astype(BF16)
    g = acc[:, 3 * W_RET:4 * W_RET]
    ret_ref[:, 3 * W_RET:4 * W_RET] = (g * jax.nn.sigmoid(g)).astype(BF16)

    acc = _dot_nt(hn, wgla_ref[...])
    gla_ref[:, 0:W_GLA] = (acc[:, 0:W_GLA] * scale).astype(BF16)
    gla_ref[:, W_GLA:3 * W_GLA] = acc[:, W_GLA:3 * W_GLA].astype(BF16)
    g = acc[:, 3 * W_GLA:4 * W_GLA]
    gla_ref[:, 3 * W_GLA:4 * W_GLA] = (g * jax.nn.sigmoid(g)).astype(BF16)
    low = acc[:, 4 * W_GLA:4 * W_GLA + LOW_PAD].astype(BF16)
    z = _dot(low, wa2_ref[...]) + ba_ref[...]
    log_sig = jnp.minimum(z, 0.0) - jnp.log(1.0 + jnp.exp(-jnp.abs(z)))
    lf_ref[...] = log_sig * (1.0 / GLA_TAU)

    acc = _dot_nt(hn, watt_ref[...])
    att_ref[:, 0:W_ATT] = (acc[:, 0:W_ATT] * (scale * LOG2E)).astype(BF16)
    att_ref[:, W_ATT:3 * W_ATT] = acc[:, W_ATT:3 * W_ATT].astype(BF16)
    return acc[:, W_ATT:2 * W_ATT], acc[:, 2 * W_ATT:3 * W_ATT]


def _in_proj_kernel(*refs, n_long_tiles, tiles_per_seq, n_prev):
    refs = list(refs)
    take = lambda k: [refs.pop(0) for _ in range(k)]
    x_l, x_s, n1w_ref, win_ref, wa2in_ref, ba_ref, cos_l, sin_l, cos_s, sin_s = take(10)
    later_f32 = take(3)
    kprev_ref, vprev_ref = take(2) if n_prev else (None, None)
    outs_l = take(4)
    kc_l, vc_l = take(2)
    outs_s = take(4)
    kc_s, vc_s = take(2)
    later_bf16 = take(3)
    weights = take(4)
    wret_ref, wgla_ref, watt_ref, wa2_ref = weights
    i = pl.program_id(0)

    @pl.when(i < n_long_tiles)
    def _():
        for src, dst in zip(later_f32, later_bf16):
            dst[...] = src[...].astype(BF16)

    @pl.when(i == 0)
    def _():
        n_main = 4 * W_RET + 4 * W_GLA
        wret_ref[...] = win_ref[0:4 * W_RET, :].astype(BF16)
        wgla_ref[0:4 * W_GLA, :] = win_ref[4 * W_RET:n_main, :].astype(BF16)
        wgla_ref[4 * W_GLA:4 * W_GLA + GLA_RANK, :] = (
            win_ref[n_main:n_main + GLA_RANK, :].astype(BF16))
        wgla_ref[4 * W_GLA + GLA_RANK:, :] = jnp.zeros((LOW_PAD - GLA_RANK, D_MODEL), BF16)
        watt_ref[...] = win_ref[n_main + GLA_RANK:IN_COLS, :].astype(BF16)
        wa2_ref[0:GLA_RANK, :] = wa2in_ref[...].astype(BF16)
        wa2_ref[GLA_RANK:, :] = jnp.zeros((LOW_PAD - GLA_RANK, W_GLA), BF16)

    @pl.when(i < n_long_tiles)
    def _():
        k32, v32 = _project_rows(x_l, cos_l, sin_l, n1w_ref, ba_ref, weights, *outs_l)

        @pl.when(i % tiles_per_seq == tiles_per_seq - 1)
        def _():
            if n_prev:
                kc_l[0:n_prev] = kprev_ref[...]
                vc_l[0:n_prev] = vprev_ref[...]
            kc_l[n_prev] = k32.T
            vc_l[n_prev] = v32.T

    @pl.when(i == n_long_tiles)
    def _():
        k32, v32 = _project_rows(x_s, cos_s, sin_s, n1w_ref, ba_ref, weights, *outs_s)
        kc_s[...] = k32
        vc_s[...] = v32


def _in_proj(x_long, x_short, n1w, w_in, w_a2, ba, tabs_long, tabs_short, prev_kv, later_w,
             *, layer, tm, seq_len):
    m = x_long.shape[0]
    ms = x_short.shape[0]
    n_tiles = m // tm
    w_out, w_up, w_down = later_w
    slice_axis = (1, 2, 1)
    later_in, later_out, later_shape = [], [], []
    for w, ax in zip(later_w, slice_axis):
        blk = list(w.shape[1:])
        blk[ax - 1] //= n_tiles
        index = lambda i, ax=ax: (tile(i),) * (ax == 1) + (0,) + (tile(i),) * (ax == 2)
        later_in.append(pl.BlockSpec((None,) + tuple(blk), lambda i, f=index: (layer,) + f(i)))
        later_out.append(pl.BlockSpec(tuple(blk), index))
        later_shape.append(jax.ShapeDtypeStruct(w.shape[1:], BF16))
    tab_blocks = tabs_long[0].shape[0] // tm
    tiles_per_seq = seq_len // tm
    at_layer = functools.partial(_layer_spec, layer=layer)
    tile = lambda i: jnp.minimum(i, n_tiles - 1)
    row = lambda w: pl.BlockSpec((tm, w), lambda i: (tile(i), 0))
    tab = pl.BlockSpec((tm, LANES), lambda i: (tile(i) % tab_blocks, 0))
    whole = lambda w: pl.BlockSpec((ms, w), lambda i: (0, 0))
    n_prev = 0 if prev_kv is None else prev_kv[0].shape[0]
    stacked = lambda n: pl.BlockSpec((n, None, W_ATT, tm),
                                     lambda i: (0, tile(i) // tiles_per_seq, 0, 0))
    prev_specs = [stacked(n_prev)] * 2 if n_prev else []
    groups = (4 * W_RET, 4 * W_GLA, W_GLA, 3 * W_ATT)
    dtypes = (BF16, BF16, F32, BF16)
    kern = functools.partial(_in_proj_kernel, n_long_tiles=n_tiles, tiles_per_seq=tiles_per_seq,
                             n_prev=n_prev)
    outs = pl.pallas_call(
        kern,
        grid=(n_tiles + 1,),
        in_specs=[row(D_MODEL), whole(D_MODEL), at_layer(n1w), at_layer(w_in), at_layer(w_a2),
                  at_layer(ba), tab, tab, whole(LANES), whole(LANES)] + later_in + prev_specs,
        out_specs=([row(w) for w in groups] + [stacked(n_prev + 1)] * 2
                   + [whole(w) for w in groups] + [whole(W_ATT)] * 2 + later_out),
        out_shape=([jax.ShapeDtypeStruct((m, w), d) for w, d in zip(groups, dtypes)]
                   + [jax.ShapeDtypeStruct((n_prev + 1, m // seq_len, W_ATT, tm), F32)] * 2
                   + [jax.ShapeDtypeStruct((ms, w), d) for w, d in zip(groups, dtypes)]
                   + [jax.ShapeDtypeStruct((ms, W_ATT), F32)] * 2 + later_shape),
        scratch_shapes=[pltpu.VMEM((4 * W_RET, D_MODEL), BF16),
                        pltpu.VMEM((4 * W_GLA + LOW_PAD, D_MODEL), BF16),
                        pltpu.VMEM((3 * W_ATT, D_MODEL), BF16),
                        pltpu.VMEM((LOW_PAD, W_GLA), BF16)],
        compiler_params=_params(1),
        name="in_proj",
    )(x_long, x_short, n1w, w_in, w_a2, ba, *tabs_long, *tabs_short, w_out, w_up, w_down,
      *(prev_kv if n_prev else ()))
    return outs[:6], outs[6:12], outs[12:]


def _init_states(s_scr, s0_ref, n_seq, n_heads):
    if s0_ref is None:
        s_scr[...] = jnp.zeros_like(s_scr)
        return
    zero = jnp.zeros((HEAD_DIM, HEAD_DIM), F32)
    for i in range(n_seq):
        s_scr[i] = jnp.concatenate(
            [jnp.concatenate([s0_ref[i, h] if j == h else zero for j in range(n_heads)], axis=1)
             for h in range(n_heads)], axis=0)


def _store_states(s_scr, out_ref, n_seq, n_heads):
    for i in range(n_seq):
        s = s_scr[i]
        for h in range(n_heads):
            sl = slice(h * HEAD_DIM, (h + 1) * HEAD_DIM)
            out_ref[i, h] = s[sl, sl]


def _chunk_rows(lc, n_chunks, n_seq):
    return [[slice((i * n_chunks + c) * lc, (i * n_chunks + c + 1) * lc)
             for c in range(n_chunks)] for i in range(n_seq)]


def _retention_body(q_ref, k_ref, v_ref, s0_ref, tab_refs, o_ref, sfin_ref, s_scr,
                    *, lc, n_chunks, n_seq):
    dtab_ref, qdec_ref, kdec_ref, gtab_ref, bd_ref = tab_refs
    g = pl.program_id(1)

    @pl.when(g == 0)
    def _():
        _init_states(s_scr, s0_ref, n_seq, H_RET)

    dtab = dtab_ref[...]
    q = q_ref[...]
    k = k_ref[...]
    v = v_ref[...]
    kd = (k.astype(F32) * kdec_ref[...]).astype(BF16)
    gtab = gtab_ref[...]
    bd = bd_ref[...]
    for i, rows in enumerate(_chunk_rows(lc, n_chunks, n_seq)):
        states = [s_scr[i]]
        for r in rows:
            upd = _dot_tn(kd[r, :], v[r, :])
            states.append(states[-1] * gtab + upd * bd)
        s_scr[i] = states[-1]
        for c, r in enumerate(rows):
            scores = _dot_nt(q[r, :], _block_diag_rows(k[r, :], H_RET))
            p = (scores * dtab).astype(BF16)
            intra = _dot(p, _block_diag_rows(v[r, :], H_RET))
            inter = _dot(q[r, :], states[c].astype(BF16))
            o_ref[r, :] = intra + inter * qdec_ref[r, :]

    @pl.when(g == pl.num_programs(1) - 1)
    def _():
        _store_states(s_scr, sfin_ref, n_seq, H_RET)


def _retention_tables(lc, n_chunks):
    lg = np.log1p(-np.exp2(-5.0 - np.arange(H_RET, dtype=np.float64)))
    idx = np.arange(lc)
    diff = idx[:, None] - idx[None, :]
    causal = diff >= 0
    dpos = np.where(causal, diff, 0).astype(np.float64)
    decay = np.where(causal[None], np.exp(dpos[None] * lg[:, None, None]), 0.0)
    dtab = np.transpose(decay, (1, 0, 2)).reshape(lc, H_RET * lc)
    q_decay = np.exp((idx + 1).astype(np.float64)[:, None] * lg[None, :])
    k_decay = np.exp((lc - 1 - idx).astype(np.float64)[:, None] * lg[None, :])
    qdec = np.tile(np.repeat(q_decay, HEAD_DIM, axis=1), (n_chunks, 1))
    kdec = np.tile(np.repeat(k_decay, HEAD_DIM, axis=1), (n_chunks, 1))
    gtab = np.broadcast_to(np.repeat(np.exp(lc * lg), HEAD_DIM)[:, None], (W_RET, W_RET))
    head = np.arange(W_RET) // HEAD_DIM
    bd = head[:, None] == head[None, :]
    return tuple(jnp.asarray(t, F32) for t in (dtab, qdec, kdec, gtab, bd))


def _split_bf16(x):
    hi = x.astype(BF16)
    lo = (x - hi.astype(F32)).astype(BF16)
    return jnp.concatenate([hi, lo], axis=1)


def _group_reference(b, row, half):
    n, w = b.shape
    group = 2 * half
    if group >= 8:
        b3 = b.reshape(n // group, group, w)
        return jnp.broadcast_to(b3[:, half - 1:half, :], b3.shape).reshape(n, w)
    if group == 4:
        i = row & 3
        return jnp.where(i == 0, pltpu.roll(b, n - 1, 0),
                         jnp.where(i == 1, b,
                                   jnp.where(i == 2, pltpu.roll(b, 1, 0), pltpu.roll(b, 2, 0))))
    return jnp.where((row & 1) == 1, pltpu.roll(b, 1, 0), b)


def _gla_body(q_ref, k_ref, v_ref, lf_ref, s0_ref, tab_refs, o_ref, sfin_ref, s_scr, sc_scr,
              *, lc, n_chunks, n_seq):
    tri_ref, bd_ref, onehot_ref = tab_refs
    g = pl.program_id(1)

    @pl.when(g == 0)
    def _():
        _init_states(s_scr, s0_ref, n_seq, H_GLA)

    n_lev = lc.bit_length() - 1
    w = W_GLA
    n = lc * n_chunks * n_seq
    seq_rows = _chunk_rows(lc, n_chunks, n_seq)
    chunk_rows = [r for rows in seq_rows for r in rows]
    row = lax.broadcasted_iota(jnp.int32, (n, w), 0)
    trow = lax.broadcasted_iota(jnp.int32, (lc, H_GLA * lc), 0)
    scol = lax.broadcasted_iota(jnp.int32, (lc, H_GLA * lc), 1) & (lc - 1)
    q = q_ref[...]
    k = k_ref[...]
    v = v_ref[...]
    q32 = q.astype(F32)
    k32 = k.astype(F32)
    tri = tri_ref[...]
    lfs = _split_bf16(lf_ref[...])
    cs = jnp.concatenate([_dot(tri, lfs[r, :]) for r in chunk_rows], axis=0)
    b = cs[:, :w] + cs[:, w:]

    qe = (q32 * jnp.exp(b)).astype(BF16)
    mild = jnp.min(b) >= -GLA_SAFE_LOG_DECAY

    @pl.when(mild)
    def _():
        ke = (k32 * jnp.exp(-b)).astype(BF16)
        for r in chunk_rows:
            sc = _dot_nt(qe[r, :], _block_diag_rows(ke[r, :], H_GLA))
            sc_scr[r, :] = jnp.where(trow >= scol, sc, 0.0)

    @pl.when(jnp.logical_not(mild))
    def _():
        for r in chunk_rows:
            sc = _dot_nt(q[r, :], _block_diag_rows(k[r, :], H_GLA))
            sc_scr[r, :] = jnp.where(trow == scol, sc, 0.0)
        for lev in range(n_lev):
            half = lc >> (lev + 1)
            shift = half.bit_length()
            upper = (row & half) != 0
            e = jnp.exp(-jnp.abs(b - _group_reference(b, row, half)))
            z = (jnp.where(upper, q32, k32) * e).astype(BF16)
            valid = (((trow >> shift) == (scol >> shift))
                     & ((trow & half) != 0) & ((scol & half) == 0))
            for r in chunk_rows:
                zc = z[r, :]
                sc = _dot_nt(zc, _block_diag_rows(zc, H_GLA))
                sc_scr[r, :] = jnp.where(valid, sc, sc_scr[r, :])

    b3 = b.reshape(n_chunks * n_seq, lc, w)
    b_last = b3[:, lc - 1:lc, :]
    kd = (k32 * jnp.exp(jnp.broadcast_to(b_last, b3.shape).reshape(n, w) - b)).astype(BF16)
    col_sum = _dot_tn(lfs, onehot_ref[...])
    decay_col = jnp.exp(col_sum[:w, :] + col_sum[w:, :])
    bd = bd_ref[...]
    for i, rows in enumerate(seq_rows):
        states = [s_scr[i]]
        for c, r in enumerate(rows):
            j = i * n_chunks + c
            upd = _dot_tn(kd[r, :], v[r, :])
            states.append(states[-1] * decay_col[:, j:j + 1] + upd * bd)
        s_scr[i] = states[-1]
        for c, r in enumerate(rows):
            inter = _dot(qe[r, :], states[c].astype(BF16))
            intra = _dot(sc_scr[r, :].astype(BF16), _block_diag_rows(v[r, :], H_GLA))
            o_ref[r, :] = intra + inter

    @pl.when(g == pl.num_programs(1) - 1)
    def _():
        _store_states(s_scr, sfin_ref, n_seq, H_GLA)


def _linear_mixers_kernel(*refs, lc, n_chunks, n_seq, has_init):
    refs = list(refs)
    take = lambda k: [refs.pop(0) for _ in range(k)]
    rq, rk, rv, gq, gk, gv, lf = take(7)
    s0r, s0g = take(2) if has_init else (None, None)
    ret_tabs = take(5)
    gla_tabs = take(3)
    oa_ref, sr_ref, ob_ref, sg_ref, sr_scr, sg_scr, sc_scr = take(7)
    dims = dict(lc=lc, n_chunks=n_chunks, n_seq=n_seq)
    _retention_body(rq, rk, rv, s0r, ret_tabs, oa_ref, sr_ref, sr_scr, **dims)
    _gla_body(gq, gk, gv, lf, s0g, gla_tabs, ob_ref, sg_ref, sg_scr, sc_scr, **dims)


def _linear_mixers(ret, gla, lf, states, layer, *, batch, seq_len, lc, n_chunks, n_seq):
    rows = lc * n_chunks * n_seq
    ng = seq_len * n_seq // rows
    m = batch * seq_len
    width = W_RET
    col = lambda c: pl.BlockSpec((rows, width), lambda b, g: (b * ng + g, c))
    full = lambda a: pl.BlockSpec(a.shape, lambda b, g: (0,) * a.ndim)
    blk = (n_seq, H_RET, HEAD_DIM, HEAD_DIM)
    s_out_spec = pl.BlockSpec(blk, lambda b, g: (b, 0, 0, 0))
    s_out_shape = jax.ShapeDtypeStruct((batch, H_RET, HEAD_DIM, HEAD_DIM), F32)
    s_in_specs = []
    if states is not None:
        s_in_specs = [pl.BlockSpec((None,) + blk, lambda b, g: (layer, b, 0, 0, 0))] * 2
    ret_tabs = _retention_tables(lc, n_chunks * n_seq)
    gla_tabs = _gla_tables(lc, n_chunks * n_seq)
    kern = functools.partial(_linear_mixers_kernel, lc=lc, n_chunks=n_chunks, n_seq=n_seq,
                             has_init=states is not None)
    out_rows = pl.BlockSpec((rows, width), lambda b, g: (b * ng + g, 0))
    return pl.pallas_call(
        kern,
        grid=(batch // n_seq, ng),
        in_specs=([col(0), col(1), col(2), col(0), col(1), col(2), col(0)] + s_in_specs
                  + [full(t) for t in ret_tabs + gla_tabs]),
        out_specs=[out_rows, s_out_spec, out_rows, s_out_spec],
        out_shape=[jax.ShapeDtypeStruct((m, width), F32), s_out_shape,
                   jax.ShapeDtypeStruct((m, width), F32), s_out_shape],
        scratch_shapes=[pltpu.VMEM((n_seq, width, width), F32),
                        pltpu.VMEM((n_seq, width, width), F32),
                        pltpu.VMEM((rows, H_GLA * lc), F32)],
        compiler_params=_params(2),
        name="linear_mixers",
    )(ret, ret, ret, gla, gla, gla, lf, *(states or ()), *ret_tabs, *gla_tabs)


def _gla_tables(lc, n_chunks):
    idx = np.arange(lc)
    tri = idx[:, None] >= idx[None, :]
    head = np.arange(W_GLA) // HEAD_DIM
    bd = head[:, None] == head[None, :]
    chunk_of_row = np.arange(lc * n_chunks) // lc
    onehot = chunk_of_row[:, None] == np.arange(LANES)[None, :]
    return jnp.asarray(tri, BF16), jnp.asarray(bd, F32), jnp.asarray(onehot, BF16)


def _band_attn_kernel(q_ref, kc_ref, vc_ref, gv_ref, hones_ref, o_ref,
                      kwin_t, vwin, bias_scr, s_scr, m_scr, ksq_prev, *, lc, rpi, n_iter):
    b_id = pl.program_id(0)
    g = pl.program_id(1)
    win = ATT_REACH + rpi
    rows = rpi * n_iter

    @pl.when((b_id == 0) & (g == 0))
    def _():
        kwin_t[:, 0:rows] = jnp.zeros((W_ATT, rows), BF16)
        vwin[0:rows, :] = jnp.zeros((rows, W_ATT), BF16)
        ksq_prev[0] = 0.0
        row = lax.broadcasted_iota(jnp.int32, (rpi, BIAS_W), 0)
        rel = lax.broadcasted_iota(jnp.int32, (rpi, BIAS_W), 1) - (row // lc) * lc
        in_band = (rel >= 0) & (rel < ATT_REACH + lc)
        for h in range(H_ATT):
            x = jnp.broadcast_to(gv_ref[h:h + 1, :], (rpi, BIAS_W))
            x = pltpu.roll(x, BIAS_W - (lc - 1), 1, stride=1, stride_axis=0)
            bias_scr[h // 2, (h % 2) * rpi:(h % 2 + 1) * rpi, :] = jnp.where(
                in_band, x * LOG2E, NEG_BIG)

    base = pl.multiple_of(g * rows, rows)
    k_cur = kc_ref[...].astype(F32)
    kwin_t[:, pl.ds(base + rows, rows)] = k_cur.T.astype(BF16)
    vwin[pl.ds(base + rows, rows), :] = vc_ref[...]

    head_ones = hones_ref[...]
    q_cur = q_ref[...].astype(F32)
    q_sq = jnp.max(_dot((q_cur * q_cur).astype(BF16), head_ones))
    k_sq_cur = jnp.max(_dot((k_cur * k_cur).astype(BF16), head_ones))
    k_sq = jnp.maximum(k_sq_cur, jnp.where(g == 0, 0.0, ksq_prev[0]))
    ksq_prev[0] = k_sq_cur
    room = ATT_SAFE_LOG2 - jnp.max(jnp.abs(gv_ref[...])) * LOG2E
    bounded = (room > 0.0) & (q_sq * k_sq * NORM_ROUNDING_MARGIN <= room * room)

    lane = lax.broadcasted_iota(jnp.int32, (rpi, LANES), 1)
    lane2 = lax.broadcasted_iota(jnp.int32, (2 * rpi, LANES), 1)
    head2 = lax.broadcasted_iota(jnp.int32, (2 * rpi, LANES), 0) // rpi
    col = lax.broadcasted_iota(jnp.int32, (2 * rpi, win), 1)

    def step(i, carry, *, masked, small):
        r0 = pl.multiple_of(i * rpi, rpi)
        first_valid = ATT_REACH - (g * rows + r0)

        def scores(pair):
            lanes = slice(pair * LANES, (pair + 1) * LANES)
            qp = q_ref[pl.ds(r0, rpi), lanes]
            q2 = jnp.concatenate([qp, qp], axis=0)
            q2 = jnp.where((lane2 // HEAD_DIM) == head2, q2, jnp.zeros_like(q2))
            s = _dot(q2, kwin_t[lanes, pl.ds(base + r0, win)]) + bias_scr[pair][:, :win]
            if masked:
                s = jnp.where(col >= first_valid, s, NEG_BIG)
            return s

        def finish(pair, e):
            lanes = slice(pair * LANES, (pair + 1) * LANES)
            den = jnp.sum(e, axis=-1, keepdims=True)
            pv = _dot(e.astype(BF16), vwin[pl.ds(base + r0, win), lanes]) / den
            o_ref[pl.ds(r0, rpi), lanes] = jnp.where(
                lane < HEAD_DIM, pv[:rpi, :], pv[rpi:, :]).astype(BF16)

        if small:
            for pair in range(H_ATT // 2):
                finish(pair, jnp.exp2(scores(pair)))
        else:
            for pair in range(H_ATT // 2):
                s = scores(pair)
                s_scr[pair, :, :win] = s
                m_scr[pair] = jnp.broadcast_to(
                    jnp.max(s, axis=-1, keepdims=True), (2 * rpi, LANES))
            for pair in range(H_ATT // 2):
                finish(pair, jnp.exp2(s_scr[pair, :, :win] - m_scr[pair][:, :1]))
        return carry

    def run(masked):
        @pl.when(bounded)
        def _():
            lax.fori_loop(0, n_iter, functools.partial(step, masked=masked, small=True), 0)

        @pl.when(jnp.logical_not(bounded))
        def _():
            lax.fori_loop(0, n_iter, functools.partial(step, masked=masked, small=False), 0)

    @pl.when(g == 0)
    def _():
        run(True)

    @pl.when(g != 0)
    def _():
        run(False)


def _band_attn(att, gv, layer, *, batch, seq_len, lc, rpi, n_iter):
    rows = rpi * n_iter
    assert rows == ATT_REACH, "the carried window is exactly one row group"
    ng = seq_len // rows
    m = batch * seq_len
    cur = lambda c: pl.BlockSpec((rows, W_ATT), lambda b, g: (b * ng + g, c))
    kern = functools.partial(_band_attn_kernel, lc=lc, rpi=rpi, n_iter=n_iter)
    head_ones = _head_ones()
    return pl.pallas_call(
        kern,
        grid=(batch, ng),
        in_specs=[cur(0), cur(1), cur(2), _layer_spec(gv, layer),
                  pl.BlockSpec(head_ones.shape, lambda b, g: (0, 0))],
        out_specs=pl.BlockSpec((rows, W_ATT), lambda b, g: (b * ng + g, 0)),
        out_shape=jax.ShapeDtypeStruct((m, W_ATT), BF16),
        scratch_shapes=[pltpu.VMEM((W_ATT, (ng + 1) * rows), BF16),
                        pltpu.VMEM(((ng + 1) * rows, W_ATT), BF16),
                        pltpu.VMEM((H_ATT // 2, 2 * rpi, BIAS_W), F32),
                        pltpu.VMEM((H_ATT // 2, 2 * rpi, BIAS_W), F32),
                        pltpu.VMEM((H_ATT // 2, 2 * rpi, LANES), F32),
                        pltpu.SMEM((1,), F32)],
        compiler_params=_params(2),
        name="band_attn",
    )(att, att, att, gv, head_ones)


def _sample_attn_kernel(q_ref, kn_ref, vn_ref, kt_ref, vt_ref, gv_ref, hones_ref, o_ref, *, ls):
    win = ATT_REACH + ls
    lane = lax.broadcasted_iota(jnp.int32, (ls, LANES), 1)
    lane2 = lax.broadcasted_iota(jnp.int32, (2 * ls, LANES), 1)
    head2 = lax.broadcasted_iota(jnp.int32, (2 * ls, LANES), 0) // ls

    head_ones = hones_ref[...]
    q32 = q_ref[...].astype(F32)
    kn32 = kn_ref[...].astype(F32)
    kc32 = kt_ref[...]
    q_sq = jnp.max(_dot((q32 * q32).astype(BF16), head_ones))
    k_sq = jnp.maximum(jnp.max(_dot((kn32 * kn32).astype(BF16), head_ones)),
                       jnp.max(jnp.sum(kc32 * kc32, axis=1)))
    room = ATT_SAFE_LOG2 - jnp.max(jnp.abs(gv_ref[...])) * LOG2E
    bounded = (room > 0.0) & (q_sq * k_sq * NORM_ROUNDING_MARGIN <= room * room)

    def head_pair(pair, small):
        lanes = slice(pair * LANES, (pair + 1) * LANES)
        qp = q_ref[:, lanes]
        q2 = jnp.concatenate([qp, qp], axis=0)
        q2 = jnp.where((lane2 // HEAD_DIM) == head2, q2, jnp.zeros_like(q2))
        kt = kt_ref[2 * pair:2 * pair + 2].reshape(2 * HEAD_DIM, ATT_REACH).astype(BF16)
        vt = vt_ref[2 * pair:2 * pair + 2].reshape(2 * HEAD_DIM, ATT_REACH).astype(BF16)
        bias = jnp.concatenate(
            [pltpu.roll(jnp.broadcast_to(gv_ref[h:h + 1, :], (ls, BIAS_W)),
                        BIAS_W - (ls - 1), 1, stride=1, stride_axis=0)
             for h in (2 * pair, 2 * pair + 1)], axis=0) * LOG2E
        s_old = _dot(q2, kt) + bias[:, :ATT_REACH]
        s_new = _dot_nt(q2, kn_ref[:, lanes]) + bias[:, ATT_REACH:win]
        if not small:
            mx = jnp.maximum(jnp.max(s_old, axis=-1, keepdims=True),
                             jnp.max(s_new, axis=-1, keepdims=True))
            s_old = s_old - mx
            s_new = s_new - mx
        e_old = jnp.exp2(s_old)
        e_new = jnp.exp2(s_new)
        den = jnp.sum(e_old, axis=-1, keepdims=True) + jnp.sum(e_new, axis=-1, keepdims=True)
        pv = (_dot_nt(e_old.astype(BF16), vt) + _dot(e_new.astype(BF16), vn_ref[:, lanes])) / den
        o_ref[:, lanes] = jnp.where(lane < HEAD_DIM, pv[:ls, :], pv[ls:, :]).astype(BF16)

    @pl.when(bounded)
    def _():
        for pair in range(H_ATT // 2):
            head_pair(pair, True)

    @pl.when(jnp.logical_not(bounded))
    def _():
        for pair in range(H_ATT // 2):
            head_pair(pair, False)


def _head_ones():
    lane_head = np.arange(W_ATT) // HEAD_DIM
    return jnp.asarray(lane_head[:, None] == np.arange(LANES)[None, :], BF16)


def _sample_attn(att, cache_kt, cache_vt, layer, gv, *, batch, ls):
    new = lambda c: pl.BlockSpec((ls, W_ATT), lambda b: (b, c))
    old = pl.BlockSpec((None, None, H_ATT, HEAD_DIM, ATT_REACH), lambda b: (layer, b, 0, 0, 0))
    head_ones = _head_ones()
    kern = functools.partial(_sample_attn_kernel, ls=ls)
    return pl.pallas_call(
        kern,
        grid=(batch,),
        in_specs=[new(0), new(1), new(2), old, old, _layer_spec(gv, layer),
                  pl.BlockSpec(head_ones.shape, lambda b: (0, 0))],
        out_specs=pl.BlockSpec((ls, W_ATT), lambda b: (b, 0)),
        out_shape=jax.ShapeDtypeStruct((batch * ls, W_ATT), BF16),
        compiler_params=_params(1),
        name="sample_attn",
    )(att, att, att, cache_kt, cache_vt, gv, head_ones)


def _bias_rows(rel_bias_l, lc):
    wv = np.arange(BIAS_W)
    dist = ATT_REACH + lc - 1 - wv
    ridx = np.clip(dist, -(CHUNK - 1), REL_MAX) + (CHUNK - 1)
    n_far = int(np.argmax(ridx < N_REL - 1)) - 1
    n_near = BIAS_W - n_far - N_REL
    assert np.array_equal(
        ridx, np.concatenate([np.full(n_far, N_REL - 1), np.arange(N_REL - 1, -1, -1),
                              np.zeros(n_near, np.int64)]))
    lead = rel_bias_l.shape[:-1]
    return jnp.concatenate(
        [jnp.broadcast_to(rel_bias_l[..., N_REL - 1:], lead + (n_far,)),
         jnp.flip(rel_bias_l, axis=-1),
         jnp.broadcast_to(rel_bias_l[..., :1], lead + (n_near,))], axis=-1)


def _mix_rows(x_ref, oa_ref, ob_ref, oc_ref, sga_ref, sgb_ref, params, y_ref, *, final_norm):
    gnw_ref, glw_ref, n2w_ref, fnw_ref, ones_ref, wout_ref, wup_ref, wdown_ref = params
    ones = ones_ref[...]

    def head_sum(t):
        return _dot(t.astype(BF16), ones)

    inv_d = 1.0 / HEAD_DIM
    oa = oa_ref[...]
    mu = head_sum(oa) * inv_d
    da = oa - mu
    var = head_sum(da * da) * inv_d
    a = da * lax.rsqrt(var + EPS) * gnw_ref[...] * sga_ref[...].astype(F32)
    ob = ob_ref[...]
    ms = head_sum(ob * ob) * inv_d
    bb = ob * lax.rsqrt(ms + EPS) * glw_ref[...] * sgb_ref[...].astype(F32)
    cat = jnp.concatenate([a.astype(BF16), bb.astype(BF16), oc_ref[...]], axis=1)
    h1 = x_ref[...] + _dot(cat, wout_ref[...])
    hn = _rmsnorm(h1, n2w_ref[...]).astype(BF16)
    acc = jnp.zeros_like(h1)
    n_slab = D_FF // D_MODEL
    for j in range(n_slab):
        sl = slice(j * D_MODEL, (j + 1) * D_MODEL)
        u = jnp.maximum(_dot(hn, wup_ref[:, sl]), 0.0)
        acc = acc + _dot((u * u).astype(BF16), wdown_ref[sl, :])
    h2 = h1 + acc
    if final_norm:
        h2 = _rmsnorm(h2, fnw_ref[...])
    y_ref[...] = h2


def _out_ffn_kernel(*refs, n_long_tiles, final_norm):
    rows_l, rows_s, params = refs[0:6], refs[6:12], refs[12:20]
    y_l, y_s = refs[20:22]
    i = pl.program_id(0)

    @pl.when(i < n_long_tiles)
    def _():
        _mix_rows(*rows_l, params, y_l, final_norm=final_norm)

    @pl.when(i == n_long_tiles)
    def _():
        _mix_rows(*rows_s, params, y_s, final_norm=final_norm)


def _out_ffn(long_in, short_in, gnw, glw, n2w, fnw, ones, wout, wup, wdown,
             *, layer, tm, final_norm):
    m = long_in[0].shape[0]
    ms = short_in[0].shape[0]
    n_tiles = m // tm
    full = lambda a: pl.BlockSpec(a.shape, lambda i: (0,) * a.ndim)
    at_layer = functools.partial(_layer_spec, layer=layer)
    tile = lambda i: jnp.minimum(i, n_tiles - 1)
    row = lambda w, c=0: pl.BlockSpec((tm, w), lambda i: (tile(i), c))
    whole = lambda w, c=0: pl.BlockSpec((ms, w), lambda i: (0, c))
    row_specs = lambda spec: [spec(D_MODEL), spec(W_RET), spec(W_GLA), spec(W_ATT),
                              spec(W_RET, 3), spec(W_GLA, 3)]
    kern = functools.partial(_out_ffn_kernel, n_long_tiles=n_tiles, final_norm=final_norm)
    return pl.pallas_call(
        kern,
        grid=(n_tiles + 1,),
        in_specs=(row_specs(row) + row_specs(whole)
                  + [at_layer(gnw), at_layer(glw), at_layer(n2w), full(fnw), full(ones),
                     full(wout), full(wup), full(wdown)]),
        out_specs=[row(D_MODEL), whole(D_MODEL)],
        out_shape=[jax.ShapeDtypeStruct((m, D_MODEL), F32),
                   jax.ShapeDtypeStruct((ms, D_MODEL), F32)],
        compiler_params=_params(1),
        name="out_ffn",
    )(*long_in, *short_in, gnw, glw, n2w, fnw, ones, wout, wup, wdown)


def _rotary_tables(pos, rows):
    half = HEAD_DIM // 2
    inv_freq = ROPE_BASE ** (-np.arange(half, dtype=np.float64) / half)
    ang = np.asarray(pos, np.float64)[:, None] * inv_freq[None, :]
    reps = (rows // len(pos), LANES // half)
    return (jnp.asarray(np.tile(np.cos(ang), reps), F32),
            jnp.asarray(np.tile(np.sin(ang), reps), F32))


def _mixers(proj, rel_bias, *, batch, seq_len, lc, n_chunks, layer, states, caches):
    ret, gla, lf, att = proj
    n_seq = batch if lc * n_chunks == seq_len else 1
    oa, s_ret, ob, s_gla = _linear_mixers(
        ret, gla, lf, states, layer, batch=batch, seq_len=seq_len, lc=lc, n_chunks=n_chunks,
        n_seq=n_seq)
    gv = _bias_rows(rel_bias, lc)
    if caches is None:
        rpi = min(lc * n_chunks, ATT_ROWS_PER_STEP)
        oc = _band_attn(att, gv, layer, batch=batch, seq_len=seq_len, lc=lc, rpi=rpi,
                        n_iter=lc * n_chunks // rpi)
    else:
        oc = _sample_attn(att, caches[0], caches[1], layer, gv, batch=batch, ls=seq_len)
    return (oa, ob, oc, ret, gla), s_ret, s_gla


def kernel(x_prompt, x_sample, state_ret, state_gla, cache_attn_k, cache_attn_v, norm1_w, norm2_w, final_norm_w, w_in, w_gla_a2, b_gla_a, ret_gn_w, gla_norm_w, rel_bias, w_out, w_up, w_down):
    depth = w_in.shape[0]
    bp, lp, _ = x_prompt.shape
    bs, ls, _ = x_sample.shape
    keep = min(ATT_REACH, lp)
    head = np.arange(W_RET) // HEAD_DIM
    ones = jnp.asarray(head[:, None] == head[None, :], BF16)
    hp = x_prompt.reshape(bp * lp, D_MODEL)
    hs = x_sample.reshape(bs * ls, D_MODEL)
    pos_p = np.arange(lp)
    pos_s = PAST_LEN + np.arange(ls)
    tm_p = keep
    to_head_major = lambda c: jnp.transpose(c, (0, 1, 3, 4, 2))
    caches = (to_head_major(cache_attn_k.astype(F32)), to_head_major(cache_attn_v.astype(F32)))
    from_head_major = lambda t: jnp.transpose(
        t.reshape(depth, bp, H_ATT, HEAD_DIM, keep), (0, 1, 4, 2, 3))
    states = (state_ret.astype(F32), state_gla.astype(F32))
    outs = {k: [] for k in ("ret_p", "gla_p", "ret_s", "gla_s", "ks", "vs")}
    prompt_kv = None
    rows3 = lambda v: v.astype(F32).reshape(depth, 1, v.shape[-1])
    n1w, n2w, fnw = rows3(norm1_w), rows3(norm2_w), final_norm_w.astype(F32)[None, :]
    gnw, glw, ba = rows3(ret_gn_w), rows3(gla_norm_w), rows3(b_gla_a)
    w_in_t = jnp.swapaxes(w_in.astype(F32), 1, 2)
    w_a2 = w_gla_a2.astype(F32)
    bias_tab = rel_bias.astype(F32)
    later_w = (w_out.astype(F32), w_up.astype(F32), w_down.astype(F32))
    tabs_p = _rotary_tables(pos_p, lp)
    tabs_s = _rotary_tables(pos_s, bs * ls)
    for l in range(depth):
        last = l == depth - 1
        proj_p, proj_s, (wout, wup, wdown) = _in_proj(
            hp, hs, n1w, w_in_t, w_a2, ba, tabs_p, tabs_s, prompt_kv, later_w,
            layer=l, tm=tm_p, seq_len=lp)
        prompt_kv = proj_p[4:6]
        mix_p, s_ret, s_gla = _mixers(
            proj_p[:4], bias_tab, batch=bp, seq_len=lp, lc=CHUNK, n_chunks=ATT_REACH // CHUNK,
            layer=l, states=None, caches=None)
        outs["ret_p"].append(s_ret)
        outs["gla_p"].append(s_gla)
        mix_s, s_ret, s_gla = _mixers(
            proj_s[:4], bias_tab, batch=bs, seq_len=ls, lc=ls, n_chunks=1,
            layer=l, states=states, caches=caches)
        outs["ret_s"].append(s_ret)
        outs["gla_s"].append(s_gla)
        outs["ks"].append(proj_s[4].reshape(bs, ls, H_ATT, HEAD_DIM))
        outs["vs"].append(proj_s[5].reshape(bs, ls, H_ATT, HEAD_DIM))
        hp, hs = _out_ffn((hp,) + mix_p, (hs,) + mix_s, gnw, glw, n2w, fnw, ones,
                          wout, wup, wdown, layer=l, tm=tm_p, final_norm=last)
    y_prompt = hp.reshape(bp, lp, D_MODEL)
    y_sample = hs.reshape(bs, ls, D_MODEL)
    st = lambda k: jnp.stack(outs[k])
    return (y_prompt, y_sample, st("ret_p"), st("gla_p"),
            from_head_major(prompt_kv[0]), from_head_major(prompt_kv[1]),
            st("ret_s"), st("gla_s"), st("ks"), st("vs"))
```

```python
import functools

import numpy as np
import jax
import jax.numpy as jnp
from jax import lax
from jax.experimental import pallas as pl
from jax.experimental.pallas import tpu as pltpu

D_MODEL = 1024
CHUNK = 64
HEAD_DIM = 64
H_RET = 4
H_GLA = 4
H_ATT = 8
W_RET = H_RET * HEAD_DIM
W_GLA = H_GLA * HEAD_DIM
W_ATT = H_ATT * HEAD_DIM
GLA_RANK = 16
GLA_TAU = 16.0
ATT_REACH = 512
REL_MAX = 256
N_REL = CHUNK + REL_MAX
D_FF = 4 * D_MODEL
IN_COLS = 4 * W_RET + 4 * W_GLA + GLA_RANK + 3 * W_ATT
ROPE_BASE = 10000.0
EPS = 1e-6
PAST_LEN = 4096

LANES = 128
LOW_PAD = LANES
BIAS_W = 640
GLA_SAFE_LOG_DECAY = 50.0
ATT_SAFE_LOG2 = 80.0
NORM_ROUNDING_MARGIN = 1.05
LINMIX_SEQS_PER_STEP = 4
SAMPLE_ATT_SEQS_PER_STEP = 4
ATT_ROWS_PER_STEP = 128
LOG2E = 1.4426950408889634
NEG_BIG = -1e30
VMEM_LIMIT = 58 * 1024 * 1024

F32 = jnp.float32
BF16 = jnp.bfloat16

_NT = (((1,), (1,)), ((), ()))
_TN = (((0,), (0,)), ((), ()))


def _dot(a, b):
    return jnp.dot(a, b, preferred_element_type=F32)


def _dot_nt(a, b):
    return lax.dot_general(a, b, _NT, preferred_element_type=F32)


def _dot_tn(a, b):
    return lax.dot_general(a, b, _TN, preferred_element_type=F32)


def _rmsnorm(x, w):
    return x * lax.rsqrt(jnp.mean(x * x, axis=-1, keepdims=True) + EPS) * w


def _params(n_grid_dims):
    return pltpu.CompilerParams(
        dimension_semantics=("arbitrary",) * n_grid_dims,
        vmem_limit_bytes=VMEM_LIMIT,
    )


def _layer_spec(a, layer):
    index = (layer,) + (0,) * (a.ndim - 1)
    return pl.BlockSpec((None,) + a.shape[1:], lambda *_: index, pipeline_mode=pl.Buffered(1))


def _block_diag_rows(x, n_heads):
    lane_head = lax.broadcasted_iota(jnp.int32, x.shape, 1) // HEAD_DIM
    zero = jnp.zeros_like(x)
    return jnp.concatenate(
        [jnp.where(lane_head == h, x, zero) for h in range(n_heads)], axis=0)


def _project_rows(x_ref, cos_ref, sin_ref, n1w_ref, ba_ref, weights,
                  ret_ref, gla_ref, lf_ref, att_ref):
    wret_ref, wgla_ref, watt_ref, wa2_ref = weights
    hn = _rmsnorm(x_ref[...], n1w_ref[...]).astype(BF16)
    cos = cos_ref[...]
    sin = sin_ref[...]
    first_half = (lax.broadcasted_iota(jnp.int32, cos.shape, 1) % HEAD_DIM) < (HEAD_DIM // 2)

    def rotary(t):
        swapped = jnp.where(first_half, -pltpu.roll(t, LANES - HEAD_DIM // 2, 1),
                            pltpu.roll(t, HEAD_DIM // 2, 1))
        return t * cos + swapped * sin

    scale = HEAD_DIM ** -0.5

    acc = _dot_nt(hn, wret_ref[...])
    for j in range(W_RET // LANES):
        sl = slice(j * LANES, (j + 1) * LANES)
        ret_ref[:, sl] = rotary(acc[:, sl]).astype(BF16)
        sk = slice(W_RET + j * LANES, W_RET + (j + 1) * LANES)
        ret_ref[:, sk] = (rotary(acc[:, sk]) * scale).astype(BF16)
    ret_ref[:, 2 * W_RET:3 * W_RET] = acc[:, 2 * W_RET:3 * W_RET].astype(BF16)
    g = acc[:, 3 * W_RET:4 * W_RET]
    ret_ref[:, 3 * W_RET:4 * W_RET] = (g * jax.nn.sigmoid(g)).astype(BF16)

    acc = _dot_nt(hn, wgla_ref[...])
    gla_ref[:, 0:W_GLA] = (acc[:, 0:W_GLA] * scale).astype(BF16)
    gla_ref[:, W_GLA:3 * W_GLA] = acc[:, W_GLA:3 * W_GLA].astype(BF16)
    g = acc[:, 3 * W_GLA:4 * W_GLA]
    gla_ref[:, 3 * W_GLA:4 * W_GLA] = (g * jax.nn.sigmoid(g)).astype(BF16)
    low = acc[:, 4 * W_GLA:4 * W_GLA + LOW_PAD].astype(BF16)
    z = _dot(low, wa2_ref[...]) + ba_ref[...]
    log_sig = jnp.minimum(z, 0.0) - jnp.log(1.0 + jnp.exp(-jnp.abs(z)))
    lf_ref[...] = log_sig * (1.0 / GLA_TAU)

    acc = _dot_nt(hn, watt_ref[...])
    att_ref[:, 0:W_ATT] = (acc[:, 0:W_ATT] * (scale * LOG2E)).astype(BF16)
    att_ref[:, W_ATT:3 * W_ATT] = acc[:, W_ATT:3 * W_ATT].astype(BF16)
    return acc[:, W_ATT:2 * W_ATT], acc[:, 2 * W_ATT:3 * W_ATT]


def _in_proj_kernel(*refs, n_long_tiles, tiles_per_seq, n_prev):
    refs = list(refs)
    take = lambda k: [refs.pop(0) for _ in range(k)]
    x_l, x_s, n1w_ref, win_ref, wa2in_ref, ba_ref, cos_l, sin_l, cos_s, sin_s = take(10)
    later_f32 = take(3)
    kprev_ref, vprev_ref = take(2) if n_prev else (None, None)
    outs_l = take(4)
    kc_l, vc_l = take(2)
    outs_s = take(4)
    kc_s, vc_s = take(2)
    later_bf16 = take(3)
    weights = take(4)
    wret_ref, wgla_ref, watt_ref, wa2_ref = weights
    i = pl.program_id(0)

    @pl.when(i < n_long_tiles)
    def _():
        for src, dst in zip(later_f32, later_bf16):
            dst[...] = src[...].astype(BF16)

    @pl.when(i == 0)
    def _():
        n_main = 4 * W_RET + 4 * W_GLA
        wret_ref[...] = win_ref[0:4 * W_RET, :].astype(BF16)
        wgla_ref[0:4 * W_GLA, :] = win_ref[4 * W_RET:n_main, :].astype(BF16)
        wgla_ref[4 * W_GLA:4 * W_GLA + GLA_RANK, :] = (
            win_ref[n_main:n_main + GLA_RANK, :].astype(BF16))
        wgla_ref[4 * W_GLA + GLA_RANK:, :] = jnp.zeros((LOW_PAD - GLA_RANK, D_MODEL), BF16)
        watt_ref[...] = win_ref[n_main + GLA_RANK:IN_COLS, :].astype(BF16)
        wa2_ref[0:GLA_RANK, :] = wa2in_ref[...].astype(BF16)
        wa2_ref[GLA_RANK:, :] = jnp.zeros((LOW_PAD - GLA_RANK, W_GLA), BF16)

    @pl.when(i < n_long_tiles)
    def _():
        k32, v32 = _project_rows(x_l, cos_l, sin_l, n1w_ref, ba_ref, weights, *outs_l)

        @pl.when(i % tiles_per_seq == tiles_per_seq - 1)
        def _():
            if n_prev:
                kc_l[0:n_prev] = kprev_ref[...]
                vc_l[0:n_prev] = vprev_ref[...]
            kc_l[n_prev] = k32.T
            vc_l[n_prev] = v32.T

    @pl.when(i == n_long_tiles)
    def _():
        k32, v32 = _project_rows(x_s, cos_s, sin_s, n1w_ref, ba_ref, weights, *outs_s)
        kc_s[...] = k32
        vc_s[...] = v32


def _in_proj(x_long, x_short, n1w, w_in, w_a2, ba, tabs_long, tabs_short, prev_kv, later_w,
             *, layer, tm, seq_len):
    m = x_long.shape[0]
    ms = x_short.shape[0]
    n_tiles = m // tm
    w_out, w_up, w_down = later_w
    slice_axis = (1, 2, 1)
    later_in, later_out, later_shape = [], [], []
    for w, ax in zip(later_w, slice_axis):
        blk = list(w.shape[1:])
        blk[ax - 1] //= n_tiles
        index = lambda i, ax=ax: (tile(i),) * (ax == 1) + (0,) + (tile(i),) * (ax == 2)
        later_in.append(pl.BlockSpec((None,) + tuple(blk), lambda i, f=index: (layer,) + f(i)))
        later_out.append(pl.BlockSpec(tuple(blk), index))
        later_shape.append(jax.ShapeDtypeStruct(w.shape[1:], BF16))
    tab_blocks = tabs_long[0].shape[0] // tm
    tiles_per_seq = seq_len // tm
    at_layer = functools.partial(_layer_spec, layer=layer)
    tile = lambda i: jnp.minimum(i, n_tiles - 1)
    row = lambda w: pl.BlockSpec((tm, w), lambda i: (tile(i), 0))
    tab = pl.BlockSpec((tm, LANES), lambda i: (tile(i) % tab_blocks, 0))
    whole = lambda w: pl.BlockSpec((ms, w), lambda i: (0, 0))
    n_prev = 0 if prev_kv is None else prev_kv[0].shape[0]
    stacked = lambda n: pl.BlockSpec((n, None, W_ATT, tm),
                                     lambda i: (0, tile(i) // tiles_per_seq, 0, 0))
    prev_specs = [stacked(n_prev)] * 2 if n_prev else []
    groups = (4 * W_RET, 4 * W_GLA, W_GLA, 3 * W_ATT)
    dtypes = (BF16, BF16, F32, BF16)
    kern = functools.partial(_in_proj_kernel, n_long_tiles=n_tiles, tiles_per_seq=tiles_per_seq,
                             n_prev=n_prev)
    outs = pl.pallas_call(
        kern,
        grid=(n_tiles + 1,),
        in_specs=[row(D_MODEL), whole(D_MODEL), at_layer(n1w), at_layer(w_in), at_layer(w_a2),
                  at_layer(ba), tab, tab, whole(LANES), whole(LANES)] + later_in + prev_specs,
        out_specs=([row(w) for w in groups] + [stacked(n_prev + 1)] * 2
                   + [whole(w) for w in groups] + [whole(W_ATT)] * 2 + later_out),
        out_shape=([jax.ShapeDtypeStruct((m, w), d) for w, d in zip(groups, dtypes)]
                   + [jax.ShapeDtypeStruct((n_prev + 1, m // seq_len, W_ATT, tm), F32)] * 2
                   + [jax.ShapeDtypeStruct((ms, w), d) for w, d in zip(groups, dtypes)]
                   + [jax.ShapeDtypeStruct((ms, W_ATT), F32)] * 2 + later_shape),
        scratch_shapes=[pltpu.VMEM((4 * W_RET, D_MODEL), BF16),
                        pltpu.VMEM((4 * W_GLA + LOW_PAD, D_MODEL), BF16),
                        pltpu.VMEM((3 * W_ATT, D_MODEL), BF16),
                        pltpu.VMEM((LOW_PAD, W_GLA), BF16)],
        compiler_params=_params(1),
        name="in_proj",
    )(x_long, x_short, n1w, w_in, w_a2, ba, *tabs_long, *tabs_short, w_out, w_up, w_down,
      *(prev_kv if n_prev else ()))
    return outs[:6], outs[6:12], outs[12:]


def _init_states(s_scr, s0_ref, n_seq, n_heads):
    if s0_ref is None:
        s_scr[...] = jnp.zeros_like(s_scr)
        return
    zero = jnp.zeros((HEAD_DIM, HEAD_DIM), F32)
    for i in range(n_seq):
        s_scr[i] = jnp.concatenate(
            [jnp.concatenate([s0_ref[i, h] if j == h else zero for j in range(n_heads)], axis=1)
             for h in range(n_heads)], axis=0)


def _store_states(s_scr, out_ref, n_seq, n_heads):
    for i in range(n_seq):
        s = s_scr[i]
        for h in range(n_heads):
            sl = slice(h * HEAD_DIM, (h + 1) * HEAD_DIM)
            out_ref[i, h] = s[sl, sl]


def _chunk_rows(lc, n_chunks, n_seq):
    return [[slice((i * n_chunks + c) * lc, (i * n_chunks + c + 1) * lc)
             for c in range(n_chunks)] for i in range(n_seq)]


def _retention_body(q_ref, k_ref, v_ref, s0_ref, tab_refs, o_ref, sfin_ref, s_scr,
                    *, lc, n_chunks, n_seq):
    dtab_ref, qdec_ref, kdec_ref, gtab_ref, bd_ref = tab_refs
    g = pl.program_id(1)

    @pl.when(g == 0)
    def _():
        _init_states(s_scr, s0_ref, n_seq, H_RET)

    dtab = dtab_ref[...]
    flat = lambda ref: ref[...].reshape(n_seq * n_chunks * lc, ref.shape[-1])
    q = flat(q_ref)
    k = flat(k_ref)
    v = flat(v_ref)
    kd = (k.astype(F32) * kdec_ref[...]).astype(BF16)
    gtab = gtab_ref[...]
    bd = bd_ref[...]
    for i, rows in enumerate(_chunk_rows(lc, n_chunks, n_seq)):
        states = [s_scr[i]]
        for r in rows:
            upd = _dot_tn(kd[r, :], v[r, :])
            states.append(states[-1] * gtab + upd * bd)
        s_scr[i] = states[-1]
        for c, r in enumerate(rows):
            scores = _dot_nt(q[r, :], _block_diag_rows(k[r, :], H_RET))
            p = (scores * dtab).astype(BF16)
            intra = _dot(p, _block_diag_rows(v[r, :], H_RET))
            inter = _dot(q[r, :], states[c].astype(BF16))
            o_ref[i, c * lc:(c + 1) * lc, :] = intra + inter * qdec_ref[r, :]

    @pl.when(g == pl.num_programs(1) - 1)
    def _():
        _store_states(s_scr, sfin_ref, n_seq, H_RET)


def _retention_tables(lc, n_chunks):
    lg = np.log1p(-np.exp2(-5.0 - np.arange(H_RET, dtype=np.float64)))
    idx = np.arange(lc)
    diff = idx[:, None] - idx[None, :]
    causal = diff >= 0
    dpos = np.where(causal, diff, 0).astype(np.float64)
    decay = np.where(causal[None], np.exp(dpos[None] * lg[:, None, None]), 0.0)
    dtab = np.transpose(decay, (1, 0, 2)).reshape(lc, H_RET * lc)
    q_decay = np.exp((idx + 1).astype(np.float64)[:, None] * lg[None, :])
    k_decay = np.exp((lc - 1 - idx).astype(np.float64)[:, None] * lg[None, :])
    qdec = np.tile(np.repeat(q_decay, HEAD_DIM, axis=1), (n_chunks, 1))
    kdec = np.tile(np.repeat(k_decay, HEAD_DIM, axis=1), (n_chunks, 1))
    gtab = np.broadcast_to(np.repeat(np.exp(lc * lg), HEAD_DIM)[:, None], (W_RET, W_RET))
    head = np.arange(W_RET) // HEAD_DIM
    bd = head[:, None] == head[None, :]
    return tuple(jnp.asarray(t, F32) for t in (dtab, qdec, kdec, gtab, bd))


def _split_bf16(x):
    hi = x.astype(BF16)
    lo = (x - hi.astype(F32)).astype(BF16)
    return jnp.concatenate([hi, lo], axis=1)


def _group_reference(b, row, half):
    n, w = b.shape
    group = 2 * half
    if group >= 8:
        b3 = b.reshape(n // group, group, w)
        return jnp.broadcast_to(b3[:, half - 1:half, :], b3.shape).reshape(n, w)
    if group == 4:
        i = row & 3
        return jnp.where(i == 0, pltpu.roll(b, n - 1, 0),
                         jnp.where(i == 1, b,
                                   jnp.where(i == 2, pltpu.roll(b, 1, 0), pltpu.roll(b, 2, 0))))
    return jnp.where((row & 1) == 1, pltpu.roll(b, 1, 0), b)


def _gla_body(q_ref, k_ref, v_ref, lf_ref, s0_ref, tab_refs, o_ref, sfin_ref, s_scr, sc_scr,
              *, lc, n_chunks, n_seq):
    tri_ref, bd_ref, onehot_ref = tab_refs
    g = pl.program_id(1)

    @pl.when(g == 0)
    def _():
        _init_states(s_scr, s0_ref, n_seq, H_GLA)

    n_lev = lc.bit_length() - 1
    w = W_GLA
    n = lc * n_chunks * n_seq
    seq_rows = _chunk_rows(lc, n_chunks, n_seq)
    chunk_rows = [r for rows in seq_rows for r in rows]
    row = lax.broadcasted_iota(jnp.int32, (n, w), 0)
    trow = lax.broadcasted_iota(jnp.int32, (lc, H_GLA * lc), 0)
    scol = lax.broadcasted_iota(jnp.int32, (lc, H_GLA * lc), 1) & (lc - 1)
    flat = lambda ref: ref[...].reshape(n, ref.shape[-1])
    q = flat(q_ref)
    k = flat(k_ref)
    v = flat(v_ref)
    q32 = q.astype(F32)
    k32 = k.astype(F32)
    tri = tri_ref[...]
    lfs = _split_bf16(flat(lf_ref))
    cs = jnp.concatenate([_dot(tri, lfs[r, :]) for r in chunk_rows], axis=0)
    b = cs[:, :w] + cs[:, w:]

    qe = (q32 * jnp.exp(b)).astype(BF16)
    mild = jnp.min(b) >= -GLA_SAFE_LOG_DECAY

    @pl.when(mild)
    def _():
        ke = (k32 * jnp.exp(-b)).astype(BF16)
        for r in chunk_rows:
            sc = _dot_nt(qe[r, :], _block_diag_rows(ke[r, :], H_GLA))
            sc_scr[r, :] = jnp.where(trow >= scol, sc, 0.0)

    @pl.when(jnp.logical_not(mild))
    def _():
        for r in chunk_rows:
            sc = _dot_nt(q[r, :], _block_diag_rows(k[r, :], H_GLA))
            sc_scr[r, :] = jnp.where(trow == scol, sc, 0.0)
        for lev in range(n_lev):
            half = lc >> (lev + 1)
            shift = half.bit_length()
            upper = (row & half) != 0
            e = jnp.exp(-jnp.abs(b - _group_reference(b, row, half)))
            z = (jnp.where(upper, q32, k32) * e).astype(BF16)
            valid = (((trow >> shift) == (scol >> shift))
                     & ((trow & half) != 0) & ((scol & half) == 0))
            for r in chunk_rows:
                zc = z[r, :]
                sc = _dot_nt(zc, _block_diag_rows(zc, H_GLA))
                sc_scr[r, :] = jnp.where(valid, sc, sc_scr[r, :])

    b3 = b.reshape(n_chunks * n_seq, lc, w)
    b_last = b3[:, lc - 1:lc, :]
    kd = (k32 * jnp.exp(jnp.broadcast_to(b_last, b3.shape).reshape(n, w) - b)).astype(BF16)
    col_sum = _dot_tn(lfs, onehot_ref[...])
    decay_col = jnp.exp(col_sum[:w, :] + col_sum[w:, :])
    bd = bd_ref[...]
    for i, rows in enumerate(seq_rows):
        states = [s_scr[i]]
        for c, r in enumerate(rows):
            j = i * n_chunks + c
            upd = _dot_tn(kd[r, :], v[r, :])
            states.append(states[-1] * decay_col[:, j:j + 1] + upd * bd)
        s_scr[i] = states[-1]
        for c, r in enumerate(rows):
            inter = _dot(qe[r, :], states[c].astype(BF16))
            intra = _dot(sc_scr[r, :].astype(BF16), _block_diag_rows(v[r, :], H_GLA))
            o_ref[i, c * lc:(c + 1) * lc, :] = intra + inter

    @pl.when(g == pl.num_programs(1) - 1)
    def _():
        _store_states(s_scr, sfin_ref, n_seq, H_GLA)


def _linear_mixers_kernel(*refs, lc, n_chunks, n_seq, has_init):
    refs = list(refs)
    take = lambda k: [refs.pop(0) for _ in range(k)]
    rq, rk, rv, gq, gk, gv, lf = take(7)
    s0r, s0g = take(2) if has_init else (None, None)
    ret_tabs = take(5)
    gla_tabs = take(3)
    oa_ref, sr_ref, ob_ref, sg_ref, sr_scr, sg_scr, sc_scr = take(7)
    dims = dict(lc=lc, n_chunks=n_chunks, n_seq=n_seq)
    _retention_body(rq, rk, rv, s0r, ret_tabs, oa_ref, sr_ref, sr_scr, **dims)
    _gla_body(gq, gk, gv, lf, s0g, gla_tabs, ob_ref, sg_ref, sg_scr, sc_scr, **dims)


def _linear_mixers(ret, gla, lf, states, layer, *, batch, seq_len, lc, n_chunks, n_seq):
    rows_seq = lc * n_chunks
    rows = rows_seq * n_seq
    ng = seq_len // rows_seq
    m = batch * seq_len
    width = W_RET
    by_seq = lambda a: a.reshape(batch, seq_len, a.shape[-1])
    ret, gla, lf = by_seq(ret), by_seq(gla), by_seq(lf)
    col = lambda c: pl.BlockSpec((n_seq, rows_seq, width), lambda b, g: (b, g, c))
    full = lambda a: pl.BlockSpec(a.shape, lambda b, g: (0,) * a.ndim)
    blk = (n_seq, H_RET, HEAD_DIM, HEAD_DIM)
    s_out_spec = pl.BlockSpec(blk, lambda b, g: (b, 0, 0, 0))
    s_out_shape = jax.ShapeDtypeStruct((batch, H_RET, HEAD_DIM, HEAD_DIM), F32)
    s_in_specs = []
    if states is not None:
        s_in_specs = [pl.BlockSpec((None,) + blk, lambda b, g: (layer, b, 0, 0, 0))] * 2
    ret_tabs = _retention_tables(lc, n_chunks * n_seq)
    gla_tabs = _gla_tables(lc, n_chunks * n_seq)
    kern = functools.partial(_linear_mixers_kernel, lc=lc, n_chunks=n_chunks, n_seq=n_seq,
                             has_init=states is not None)
    out_rows = col(0)
    out_shape = jax.ShapeDtypeStruct((batch, seq_len, width), F32)
    oa, s_ret, ob, s_gla = pl.pallas_call(
        kern,
        grid=(batch // n_seq, ng),
        in_specs=([col(0), col(1), col(2), col(0), col(1), col(2), col(0)] + s_in_specs
                  + [full(t) for t in ret_tabs + gla_tabs]),
        out_specs=[out_rows, s_out_spec, out_rows, s_out_spec],
        out_shape=[out_shape, s_out_shape, out_shape, s_out_shape],
        scratch_shapes=[pltpu.VMEM((n_seq, width, width), F32),
                        pltpu.VMEM((n_seq, width, width), F32),
                        pltpu.VMEM((rows, H_GLA * lc), F32)],
        compiler_params=_params(2),
        name="linear_mixers",
    )(ret, ret, ret, gla, gla, gla, lf, *(states or ()), *ret_tabs, *gla_tabs)
    return oa.reshape(m, width), s_ret, ob.reshape(m, width), s_gla


def _gla_tables(lc, n_chunks):
    idx = np.arange(lc)
    tri = idx[:, None] >= idx[None, :]
    head = np.arange(W_GLA) // HEAD_DIM
    bd = head[:, None] == head[None, :]
    chunk_of_row = np.arange(lc * n_chunks) // lc
    onehot = chunk_of_row[:, None] == np.arange(LANES)[None, :]
    return jnp.asarray(tri, BF16), jnp.asarray(bd, F32), jnp.asarray(onehot, BF16)


def _band_attn_kernel(q_ref, kc_ref, vc_ref, gv_ref, hones_ref, o_ref,
                      kwin_t, vwin, bias_scr, s_scr, m_scr, ksq_prev, *, lc, rpi, n_iter):
    b_id = pl.program_id(0)
    g = pl.program_id(1)
    win = ATT_REACH + rpi
    rows = rpi * n_iter

    @pl.when((b_id == 0) & (g == 0))
    def _():
        kwin_t[:, 0:rows] = jnp.zeros((W_ATT, rows), BF16)
        vwin[0:rows, :] = jnp.zeros((rows, W_ATT), BF16)
        ksq_prev[0] = 0.0
        row = lax.broadcasted_iota(jnp.int32, (rpi, BIAS_W), 0)
        rel = lax.broadcasted_iota(jnp.int32, (rpi, BIAS_W), 1) - (row // lc) * lc
        in_band = (rel >= 0) & (rel < ATT_REACH + lc)
        for h in range(H_ATT):
            x = jnp.broadcast_to(gv_ref[h:h + 1, :], (rpi, BIAS_W))
            x = pltpu.roll(x, BIAS_W - (lc - 1), 1, stride=1, stride_axis=0)
            bias_scr[h // 2, (h % 2) * rpi:(h % 2 + 1) * rpi, :] = jnp.where(
                in_band, x * LOG2E, NEG_BIG)

    base = pl.multiple_of(g * rows, rows)
    k_cur = kc_ref[...].astype(F32)
    kwin_t[:, pl.ds(base + rows, rows)] = k_cur.T.astype(BF16)
    vwin[pl.ds(base + rows, rows), :] = vc_ref[...]

    head_ones = hones_ref[...]
    q_cur = q_ref[...].astype(F32)
    q_sq = jnp.max(_dot((q_cur * q_cur).astype(BF16), head_ones))
    k_sq_cur = jnp.max(_dot((k_cur * k_cur).astype(BF16), head_ones))
    k_sq = jnp.maximum(k_sq_cur, jnp.where(g == 0, 0.0, ksq_prev[0]))
    ksq_prev[0] = k_sq_cur
    room = ATT_SAFE_LOG2 - jnp.max(jnp.abs(gv_ref[...])) * LOG2E
    bounded = (room > 0.0) & (q_sq * k_sq * NORM_ROUNDING_MARGIN <= room * room)

    lane = lax.broadcasted_iota(jnp.int32, (rpi, LANES), 1)
    lane2 = lax.broadcasted_iota(jnp.int32, (2 * rpi, LANES), 1)
    head2 = lax.broadcasted_iota(jnp.int32, (2 * rpi, LANES), 0) // rpi
    col = lax.broadcasted_iota(jnp.int32, (2 * rpi, win), 1)

    def step(i, carry, *, masked, small):
        r0 = pl.multiple_of(i * rpi, rpi)
        first_valid = ATT_REACH - (g * rows + r0)

        def scores(pair):
            lanes = slice(pair * LANES, (pair + 1) * LANES)
            qp = q_ref[pl.ds(r0, rpi), lanes]
            q2 = jnp.concatenate([qp, qp], axis=0)
            q2 = jnp.where((lane2 // HEAD_DIM) == head2, q2, jnp.zeros_like(q2))
            s = _dot(q2, kwin_t[lanes, pl.ds(base + r0, win)]) + bias_scr[pair][:, :win]
            if masked:
                s = jnp.where(col >= first_valid, s, NEG_BIG)
            return s

        def finish(pair, e):
            lanes = slice(pair * LANES, (pair + 1) * LANES)
            den = jnp.sum(e, axis=-1, keepdims=True)
            pv = _dot(e.astype(BF16), vwin[pl.ds(base + r0, win), lanes]) / den
            o_ref[pl.ds(r0, rpi), lanes] = jnp.where(
                lane < HEAD_DIM, pv[:rpi, :], pv[rpi:, :]).astype(BF16)

        if small:
            for pair in range(H_ATT // 2):
                finish(pair, jnp.exp2(scores(pair)))
        else:
            for pair in range(H_ATT // 2):
                s = scores(pair)
                s_scr[pair, :, :win] = s
                m_scr[pair] = jnp.broadcast_to(
                    jnp.max(s, axis=-1, keepdims=True), (2 * rpi, LANES))
            for pair in range(H_ATT // 2):
                finish(pair, jnp.exp2(s_scr[pair, :, :win] - m_scr[pair][:, :1]))
        return carry

    def run(masked):
        @pl.when(bounded)
        def _():
            lax.fori_loop(0, n_iter, functools.partial(step, masked=masked, small=True), 0)

        @pl.when(jnp.logical_not(bounded))
        def _():
            lax.fori_loop(0, n_iter, functools.partial(step, masked=masked, small=False), 0)

    @pl.when(g == 0)
    def _():
        run(True)

    @pl.when(g != 0)
    def _():
        run(False)


def _band_attn(att, gv, layer, *, batch, seq_len, lc, rpi, n_iter):
    rows = rpi * n_iter
    assert rows == ATT_REACH, "the carried window is exactly one row group"
    ng = seq_len // rows
    m = batch * seq_len
    cur = lambda c: pl.BlockSpec((rows, W_ATT), lambda b, g: (b * ng + g, c))
    kern = functools.partial(_band_attn_kernel, lc=lc, rpi=rpi, n_iter=n_iter)
    head_ones = _head_ones()
    return pl.pallas_call(
        kern,
        grid=(batch, ng),
        in_specs=[cur(0), cur(1), cur(2), _layer_spec(gv, layer),
                  pl.BlockSpec(head_ones.shape, lambda b, g: (0, 0))],
        out_specs=pl.BlockSpec((rows, W_ATT), lambda b, g: (b * ng + g, 0)),
        out_shape=jax.ShapeDtypeStruct((m, W_ATT), BF16),
        scratch_shapes=[pltpu.VMEM((W_ATT, (ng + 1) * rows), BF16),
                        pltpu.VMEM(((ng + 1) * rows, W_ATT), BF16),
                        pltpu.VMEM((H_ATT // 2, 2 * rpi, BIAS_W), F32),
                        pltpu.VMEM((H_ATT // 2, 2 * rpi, BIAS_W), F32),
                        pltpu.VMEM((H_ATT // 2, 2 * rpi, LANES), F32),
                        pltpu.SMEM((1,), F32)],
        compiler_params=_params(2),
        name="band_attn",
    )(att, att, att, gv, head_ones)


def _sample_attn_kernel(q_ref, kn_ref, vn_ref, kt_ref, vt_ref, gv_ref, hones_ref, o_ref,
                        *, ls, n_seq):
    win = ATT_REACH + ls
    lane = lax.broadcasted_iota(jnp.int32, (ls, LANES), 1)
    lane2 = lax.broadcasted_iota(jnp.int32, (2 * ls, LANES), 1)
    head2 = lax.broadcasted_iota(jnp.int32, (2 * ls, LANES), 0) // ls

    head_ones = hones_ref[...]
    q32 = q_ref[...].astype(F32)
    kn32 = kn_ref[...].astype(F32)
    kc32 = kt_ref[...]
    q_sq = jnp.max(_dot((q32 * q32).astype(BF16), head_ones))
    k_sq = jnp.maximum(jnp.max(_dot((kn32 * kn32).astype(BF16), head_ones)),
                       jnp.max(jnp.sum(kc32 * kc32, axis=2)))
    room = ATT_SAFE_LOG2 - jnp.max(jnp.abs(gv_ref[...])) * LOG2E
    bounded = (room > 0.0) & (q_sq * k_sq * NORM_ROUNDING_MARGIN <= room * room)

    def head_pair(seq, pair, small):
        lanes = slice(pair * LANES, (pair + 1) * LANES)
        rows = slice(seq * ls, (seq + 1) * ls)
        qp = q_ref[rows, lanes]
        q2 = jnp.concatenate([qp, qp], axis=0)
        q2 = jnp.where((lane2 // HEAD_DIM) == head2, q2, jnp.zeros_like(q2))
        heads = slice(2 * pair, 2 * pair + 2)
        kt = kt_ref[seq, heads].reshape(2 * HEAD_DIM, ATT_REACH).astype(BF16)
        vt = vt_ref[seq, heads].reshape(2 * HEAD_DIM, ATT_REACH).astype(BF16)
        bias = jnp.concatenate(
            [pltpu.roll(jnp.broadcast_to(gv_ref[h:h + 1, :], (ls, BIAS_W)),
                        BIAS_W - (ls - 1), 1, stride=1, stride_axis=0)
             for h in (2 * pair, 2 * pair + 1)], axis=0) * LOG2E
        s_old = _dot(q2, kt) + bias[:, :ATT_REACH]
        s_new = _dot_nt(q2, kn_ref[rows, lanes]) + bias[:, ATT_REACH:win]
        if not small:
            mx = jnp.maximum(jnp.max(s_old, axis=-1, keepdims=True),
                             jnp.max(s_new, axis=-1, keepdims=True))
            s_old = s_old - mx
            s_new = s_new - mx
        e_old = jnp.exp2(s_old)
        e_new = jnp.exp2(s_new)
        den = jnp.sum(e_old, axis=-1, keepdims=True) + jnp.sum(e_new, axis=-1, keepdims=True)
        pv = (_dot_nt(e_old.astype(BF16), vt)
              + _dot(e_new.astype(BF16), vn_ref[rows, lanes])) / den
        o_ref[rows, lanes] = jnp.where(lane < HEAD_DIM, pv[:ls, :], pv[ls:, :]).astype(BF16)

    @pl.when(bounded)
    def _():
        for seq in range(n_seq):
            for pair in range(H_ATT // 2):
                head_pair(seq, pair, True)

    @pl.when(jnp.logical_not(bounded))
    def _():
        for seq in range(n_seq):
            for pair in range(H_ATT // 2):
                head_pair(seq, pair, False)


def _head_ones():
    lane_head = np.arange(W_ATT) // HEAD_DIM
    return jnp.asarray(lane_head[:, None] == np.arange(LANES)[None, :], BF16)


def _sample_attn(att, cache_kt, cache_vt, layer, gv, *, batch, ls):
    n_seq = SAMPLE_ATT_SEQS_PER_STEP
    new = lambda c: pl.BlockSpec((n_seq * ls, W_ATT), lambda b: (b, c))
    old = pl.BlockSpec((None, n_seq, H_ATT, HEAD_DIM, ATT_REACH),
                       lambda b: (layer, b, 0, 0, 0))
    head_ones = _head_ones()
    kern = functools.partial(_sample_attn_kernel, ls=ls, n_seq=n_seq)
    return pl.pallas_call(
        kern,
        grid=(batch // n_seq,),
        in_specs=[new(0), new(1), new(2), old, old, _layer_spec(gv, layer),
                  pl.BlockSpec(head_ones.shape, lambda b: (0, 0))],
        out_specs=pl.BlockSpec((n_seq * ls, W_ATT), lambda b: (b, 0)),
        out_shape=jax.ShapeDtypeStruct((batch * ls, W_ATT), BF16),
        compiler_params=_params(1),
        name="sample_attn",
    )(att, att, att, cache_kt, cache_vt, gv, head_ones)


def _bias_rows(rel_bias_l, lc):
    wv = np.arange(BIAS_W)
    dist = ATT_REACH + lc - 1 - wv
    ridx = np.clip(dist, -(CHUNK - 1), REL_MAX) + (CHUNK - 1)
    n_far = int(np.argmax(ridx < N_REL - 1)) - 1
    n_near = BIAS_W - n_far - N_REL
    assert np.array_equal(
        ridx, np.concatenate([np.full(n_far, N_REL - 1), np.arange(N_REL - 1, -1, -1),
                              np.zeros(n_near, np.int64)]))
    lead = rel_bias_l.shape[:-1]
    return jnp.concatenate(
        [jnp.broadcast_to(rel_bias_l[..., N_REL - 1:], lead + (n_far,)),
         jnp.flip(rel_bias_l, axis=-1),
         jnp.broadcast_to(rel_bias_l[..., :1], lead + (n_near,))], axis=-1)


def _mix_rows(x_ref, oa_ref, ob_ref, oc_ref, sga_ref, sgb_ref, params, y_ref, *, final_norm):
    gnw_ref, glw_ref, n2w_ref, fnw_ref, ones_ref, wout_ref, wup_ref, wdown_ref = params
    ones = ones_ref[...]

    def head_sum(t):
        return _dot(t.astype(BF16), ones)

    inv_d = 1.0 / HEAD_DIM
    oa = oa_ref[...]
    mu = head_sum(oa) * inv_d
    da = oa - mu
    var = head_sum(da * da) * inv_d
    a = da * lax.rsqrt(var + EPS) * gnw_ref[...] * sga_ref[...].astype(F32)
    ob = ob_ref[...]
    ms = head_sum(ob * ob) * inv_d
    bb = ob * lax.rsqrt(ms + EPS) * glw_ref[...] * sgb_ref[...].astype(F32)
    cat = jnp.concatenate([a.astype(BF16), bb.astype(BF16), oc_ref[...]], axis=1)
    h1 = x_ref[...] + _dot(cat, wout_ref[...])
    hn = _rmsnorm(h1, n2w_ref[...]).astype(BF16)
    acc = jnp.zeros_like(h1)
    n_slab = D_FF // D_MODEL
    for j in range(n_slab):
        sl = slice(j * D_MODEL, (j + 1) * D_MODEL)
        u = jnp.maximum(_dot(hn, wup_ref[:, sl]), 0.0)
        acc = acc + _dot((u * u).astype(BF16), wdown_ref[sl, :])
    h2 = h1 + acc
    if final_norm:
        h2 = _rmsnorm(h2, fnw_ref[...])
    y_ref[...] = h2


def _out_ffn_kernel(*refs, n_long_tiles, final_norm):
    rows_l, rows_s, params = refs[0:6], refs[6:12], refs[12:20]
    y_l, y_s = refs[20:22]
    i = pl.program_id(0)

    @pl.when(i < n_long_tiles)
    def _():
        _mix_rows(*rows_l, params, y_l, final_norm=final_norm)

    @pl.when(i == n_long_tiles)
    def _():
        _mix_rows(*rows_s, params, y_s, final_norm=final_norm)


def _out_ffn(long_in, short_in, gnw, glw, n2w, fnw, ones, wout, wup, wdown,
             *, layer, tm, final_norm):
    m = long_in[0].shape[0]
    ms = short_in[0].shape[0]
    n_tiles = m // tm
    full = lambda a: pl.BlockSpec(a.shape, lambda i: (0,) * a.ndim)
    at_layer = functools.partial(_layer_spec, layer=layer)
    tile = lambda i: jnp.minimum(i, n_tiles - 1)
    row = lambda w, c=0: pl.BlockSpec((tm, w), lambda i: (tile(i), c))
    whole = lambda w, c=0: pl.BlockSpec((ms, w), lambda i: (0, c))
    row_specs = lambda spec: [spec(D_MODEL), spec(W_RET), spec(W_GLA), spec(W_ATT),
                              spec(W_RET, 3), spec(W_GLA, 3)]
    kern = functools.partial(_out_ffn_kernel, n_long_tiles=n_tiles, final_norm=final_norm)
    return pl.pallas_call(
        kern,
        grid=(n_tiles + 1,),
        in_specs=(row_specs(row) + row_specs(whole)
                  + [at_layer(gnw), at_layer(glw), at_layer(n2w), full(fnw), full(ones),
                     full(wout), full(wup), full(wdown)]),
        out_specs=[row(D_MODEL), whole(D_MODEL)],
        out_shape=[jax.ShapeDtypeStruct((m, D_MODEL), F32),
                   jax.ShapeDtypeStruct((ms, D_MODEL), F32)],
        compiler_params=_params(1),
        name="out_ffn",
    )(*long_in, *short_in, gnw, glw, n2w, fnw, ones, wout, wup, wdown)


def _rotary_tables(pos, rows):
    half = HEAD_DIM // 2
    inv_freq = ROPE_BASE ** (-np.arange(half, dtype=np.float64) / half)
    ang = np.asarray(pos, np.float64)[:, None] * inv_freq[None, :]
    reps = (rows // len(pos), LANES // half)
    return (jnp.asarray(np.tile(np.cos(ang), reps), F32),
            jnp.asarray(np.tile(np.sin(ang), reps), F32))


def _mixers(proj, rel_bias, *, batch, seq_len, lc, n_chunks, layer, states, caches):
    ret, gla, lf, att = proj
    n_seq = batch if lc * n_chunks == seq_len else LINMIX_SEQS_PER_STEP
    oa, s_ret, ob, s_gla = _linear_mixers(
        ret, gla, lf, states, layer, batch=batch, seq_len=seq_len, lc=lc, n_chunks=n_chunks,
        n_seq=n_seq)
    gv = _bias_rows(rel_bias, lc)
    if caches is None:
        rpi = min(lc * n_chunks, ATT_ROWS_PER_STEP)
        oc = _band_attn(att, gv, layer, batch=batch, seq_len=seq_len, lc=lc, rpi=rpi,
                        n_iter=lc * n_chunks // rpi)
    else:
        oc = _sample_attn(att, caches[0], caches[1], layer, gv, batch=batch, ls=seq_len)
    return (oa, ob, oc, ret, gla), s_ret, s_gla


def kernel(x_prompt, x_sample, state_ret, state_gla, cache_attn_k, cache_attn_v, norm1_w, norm2_w, final_norm_w, w_in, w_gla_a2, b_gla_a, ret_gn_w, gla_norm_w, rel_bias, w_out, w_up, w_down):
    depth = w_in.shape[0]
    bp, lp, _ = x_prompt.shape
    bs, ls, _ = x_sample.shape
    keep = min(ATT_REACH, lp)
    head = np.arange(W_RET) // HEAD_DIM
    ones = jnp.asarray(head[:, None] == head[None, :], BF16)
    hp = x_prompt.reshape(bp * lp, D_MODEL)
    hs = x_sample.reshape(bs * ls, D_MODEL)
    pos_p = np.arange(lp)
    pos_s = PAST_LEN + np.arange(ls)
    tm_p = keep
    to_head_major = lambda c: jnp.transpose(c, (0, 1, 3, 4, 2))
    caches = (to_head_major(cache_attn_k.astype(F32)), to_head_major(cache_attn_v.astype(F32)))
    from_head_major = lambda t: jnp.transpose(
        t.reshape(depth, bp, H_ATT, HEAD_DIM, keep), (0, 1, 4, 2, 3))
    states = (state_ret.astype(F32), state_gla.astype(F32))
    outs = {k: [] for k in ("ret_p", "gla_p", "ret_s", "gla_s", "ks", "vs")}
    prompt_kv = None
    rows3 = lambda v: v.astype(F32).reshape(depth, 1, v.shape[-1])
    n1w, n2w, fnw = rows3(norm1_w), rows3(norm2_w), final_norm_w.astype(F32)[None, :]
    gnw, glw, ba = rows3(ret_gn_w), rows3(gla_norm_w), rows3(b_gla_a)
    w_in_t = jnp.swapaxes(w_in.astype(F32), 1, 2)
    w_a2 = w_gla_a2.astype(F32)
    bias_tab = rel_bias.astype(F32)
    later_w = (w_out.astype(F32), w_up.astype(F32), w_down.astype(F32))
    tabs_p = _rotary_tables(pos_p, lp)
    tabs_s = _rotary_tables(pos_s, bs * ls)
    for l in range(depth):
        last = l == depth - 1
        proj_p, proj_s, (wout, wup, wdown) = _in_proj(
            hp, hs, n1w, w_in_t, w_a2, ba, tabs_p, tabs_s, prompt_kv, later_w,
            layer=l, tm=tm_p, seq_len=lp)
        prompt_kv = proj_p[4:6]
        mix_p, s_ret, s_gla = _mixers(
            proj_p[:4], bias_tab, batch=bp, seq_len=lp, lc=CHUNK, n_chunks=ATT_REACH // CHUNK,
            layer=l, states=None, caches=None)
        outs["ret_p"].append(s_ret)
        outs["gla_p"].append(s_gla)
        mix_s, s_ret, s_gla = _mixers(
            proj_s[:4], bias_tab, batch=bs, seq_len=ls, lc=ls, n_chunks=1,
            layer=l, states=states, caches=caches)
        outs["ret_s"].append(s_ret)
        outs["gla_s"].append(s_gla)
        outs["ks"].append(proj_s[4].reshape(bs, ls, H_ATT, HEAD_DIM))
        outs["vs"].append(proj_s[5].reshape(bs, ls, H_ATT, HEAD_DIM))
        hp, hs = _out_ffn((hp,) + mix_p, (hs,) + mix_s, gnw, glw, n2w, fnw, ones,
                          wout, wup, wdown, layer=l, tm=tm_p, final_norm=last)
    y_prompt = hp.reshape(bp, lp, D_MODEL)
    y_sample = hs.reshape(bs, ls, D_MODEL)
    st = lambda k: jnp.stack(outs[k])
    return (y_prompt, y_sample, st("ret_p"), st("gla_p"),
            from_head_major(prompt_kv[0]), from_head_major(prompt_kv[1]),
            st("ret_s"), st("gla_s"), st("ks"), st("vs"))
```

```python
import functools

import numpy as np
import jax
import jax.numpy as jnp
from jax import lax
from jax.experimental import pallas as pl
from jax.experimental.pallas import tpu as pltpu

D_MODEL = 1024
CHUNK = 64
HEAD_DIM = 64
H_RET = 4
H_GLA = 4
H_ATT = 8
W_RET = H_RET * HEAD_DIM
W_GLA = H_GLA * HEAD_DIM
W_ATT = H_ATT * HEAD_DIM
GLA_RANK = 16
GLA_TAU = 16.0
ATT_REACH = 512
REL_MAX = 256
N_REL = CHUNK + REL_MAX
D_FF = 4 * D_MODEL
IN_COLS = 4 * W_RET + 4 * W_GLA + GLA_RANK + 3 * W_ATT
ROPE_BASE = 10000.0
EPS = 1e-6
PAST_LEN = 4096

LANES = 128
LOW_PAD = LANES
BIAS_W = 640
GLA_SAFE_LOG_DECAY = 50.0
ATT_SAFE_LOG2 = 80.0
NORM_ROUNDING_MARGIN = 1.05
LINMIX_SEQS_PER_STEP = 4
SAMPLE_ATT_SEQS_PER_STEP = 4
ATT_ROWS_PER_STEP = 128
LOG2E = 1.4426950408889634
NEG_BIG = -1e30
VMEM_LIMIT = 58 * 1024 * 1024

F32 = jnp.float32
BF16 = jnp.bfloat16

_NT = (((1,), (1,)), ((), ()))
_TN = (((0,), (0,)), ((), ()))


def _dot(a, b):
    return jnp.dot(a, b, preferred_element_type=F32)


def _dot_nt(a, b):
    return lax.dot_general(a, b, _NT, preferred_element_type=F32)


def _dot_tn(a, b):
    return lax.dot_general(a, b, _TN, preferred_element_type=F32)


def _rmsnorm(x, w):
    return x * lax.rsqrt(jnp.mean(x * x, axis=-1, keepdims=True) + EPS) * w


def _params(n_grid_dims):
    return pltpu.CompilerParams(
        dimension_semantics=("arbitrary",) * n_grid_dims,
        vmem_limit_bytes=VMEM_LIMIT,
    )


def _layer_spec(a, layer):
    index = (layer,) + (0,) * (a.ndim - 1)
    return pl.BlockSpec((None,) + a.shape[1:], lambda *_: index, pipeline_mode=pl.Buffered(1))


def _block_diag_rows(x, n_heads):
    lane_head = lax.broadcasted_iota(jnp.int32, x.shape, 1) // HEAD_DIM
    zero = jnp.zeros_like(x)
    return jnp.concatenate(
        [jnp.where(lane_head == h, x, zero) for h in range(n_heads)], axis=0)


def _project_rows(x_ref, cos_ref, sin_ref, n1w_ref, ba_ref, weights,
                  ret_ref, gla_ref, lf_ref, att_ref):
    wret_ref, wgla_ref, watt_ref, wa2_ref = weights
    hn = _rmsnorm(x_ref[...], n1w_ref[...]).astype(BF16)
    cos = cos_ref[...]
    sin = sin_ref[...]
    first_half = (lax.broadcasted_iota(jnp.int32, cos.shape, 1) % HEAD_DIM) < (HEAD_DIM // 2)

    def rotary(t):
        swapped = jnp.where(first_half, -pltpu.roll(t, LANES - HEAD_DIM // 2, 1),
                            pltpu.roll(t, HEAD_DIM // 2, 1))
        return t * cos + swapped * sin

    scale = HEAD_DIM ** -0.5

    acc = _dot_nt(hn, wret_ref[...])
    for j in range(W_RET // LANES):
        sl = slice(j * LANES, (j + 1) * LANES)
        ret_ref[:, sl] = rotary(acc[:, sl]).astype(BF16)
        sk = slice(W_RET + j * LANES, W_RET + (j + 1) * LANES)
        ret_ref[:, sk] = (rotary(acc[:, sk]) * scale).astype(BF16)
    ret_ref[:, 2 * W_RET:3 * W_RET] = acc[:, 2 * W_RET:3 * W_RET].astype(BF16)
    g = acc[:, 3 * W_RET:4 * W_RET]
    ret_ref[:, 3 * W_RET:4 * W_RET] = (g * jax.nn.sigmoid(g)).astype(BF16)

    acc = _dot_nt(hn, wgla_ref[...])
    gla_ref[:, 0:W_GLA] = (acc[:, 0:W_GLA] * scale).astype(BF16)
    gla_ref[:, W_GLA:3 * W_GLA] = acc[:, W_GLA:3 * W_GLA].astype(BF16)
    g = acc[:, 3 * W_GLA:4 * W_GLA]
    gla_ref[:, 3 * W_GLA:4 * W_GLA] = (g * jax.nn.sigmoid(g)).astype(BF16)
    low = acc[:, 4 * W_GLA:4 * W_GLA + LOW_PAD].astype(BF16)
    z = _dot(low, wa2_ref[...]) + ba_ref[...]
    log_sig = jnp.minimum(z, 0.0) - jnp.log(1.0 + jnp.exp(-jnp.abs(z)))
    lf_ref[...] = log_sig * (1.0 / GLA_TAU)

    acc = _dot_nt(hn, watt_ref[...])
    att_ref[:, 0:W_ATT] = (acc[:, 0:W_ATT] * (scale * LOG2E)).astype(BF16)
    att_ref[:, W_ATT:3 * W_ATT] = acc[:, W_ATT:3 * W_ATT].astype(BF16)
    return acc[:, W_ATT:2 * W_ATT], acc[:, 2 * W_ATT:3 * W_ATT]


def _in_proj_kernel(*refs, n_long_tiles, tiles_per_seq, n_prev):
    refs = list(refs)
    take = lambda k: [refs.pop(0) for _ in range(k)]
    x_l, x_s, n1w_ref, win_ref, wa2in_ref, ba_ref, cos_l, sin_l, cos_s, sin_s = take(10)
    later_f32 = take(3)
    kprev_ref, vprev_ref = take(2) if n_prev else (None, None)
    outs_l = take(4)
    kc_l, vc_l = take(2)
    outs_s = take(4)
    kc_s, vc_s = take(2)
    later_bf16 = take(3)
    weights = take(4)
    wret_ref, wgla_ref, watt_ref, wa2_ref = weights
    i = pl.program_id(0)

    @pl.when(i < n_long_tiles)
    def _():
        for src, dst in zip(later_f32, later_bf16):
            dst[...] = src[...].astype(BF16)

    @pl.when(i == 0)
    def _():
        n_main = 4 * W_RET + 4 * W_GLA
        wret_ref[...] = win_ref[0:4 * W_RET, :].astype(BF16)
        wgla_ref[0:4 * W_GLA, :] = win_ref[4 * W_RET:n_main, :].astype(BF16)
        wgla_ref[4 * W_GLA:4 * W_GLA + GLA_RANK, :] = (
            win_ref[n_main:n_main + GLA_RANK, :].astype(BF16))
        wgla_ref[4 * W_GLA + GLA_RANK:, :] = jnp.zeros((LOW_PAD - GLA_RANK, D_MODEL), BF16)
        watt_ref[...] = win_ref[n_main + GLA_RANK:IN_COLS, :].astype(BF16)
        wa2_ref[0:GLA_RANK, :] = wa2in_ref[...].astype(BF16)
        wa2_ref[GLA_RANK:, :] = jnp.zeros((LOW_PAD - GLA_RANK, W_GLA), BF16)

    @pl.when(i < n_long_tiles)
    def _():
        k32, v32 = _project_rows(x_l, cos_l, sin_l, n1w_ref, ba_ref, weights, *outs_l)

        @pl.when(i % tiles_per_seq == tiles_per_seq - 1)
        def _():
            if n_prev:
                kc_l[0:n_prev] = kprev_ref[...]
                vc_l[0:n_prev] = vprev_ref[...]
            kc_l[n_prev] = k32.T
            vc_l[n_prev] = v32.T

    @pl.when(i == n_long_tiles)
    def _():
        k32, v32 = _project_rows(x_s, cos_s, sin_s, n1w_ref, ba_ref, weights, *outs_s)
        kc_s[...] = k32
        vc_s[...] = v32


def _in_proj(x_long, x_short, n1w, w_in, w_a2, ba, tabs_long, tabs_short, prev_kv, later_w,
             *, layer, tm, seq_len):
    m = x_long.shape[0]
    ms = x_short.shape[0]
    n_tiles = m // tm
    w_out, w_up, w_down = later_w
    slice_axis = (1, 2, 1)
    later_in, later_out, later_shape = [], [], []
    for w, ax in zip(later_w, slice_axis):
        blk = list(w.shape[1:])
        blk[ax - 1] //= n_tiles
        index = lambda i, ax=ax: (tile(i),) * (ax == 1) + (0,) + (tile(i),) * (ax == 2)
        later_in.append(pl.BlockSpec((None,) + tuple(blk), lambda i, f=index: (layer,) + f(i)))
        later_out.append(pl.BlockSpec(tuple(blk), index))
        later_shape.append(jax.ShapeDtypeStruct(w.shape[1:], BF16))
    tab_blocks = tabs_long[0].shape[0] // tm
    tiles_per_seq = seq_len // tm
    at_layer = functools.partial(_layer_spec, layer=layer)
    tile = lambda i: jnp.minimum(i, n_tiles - 1)
    row = lambda w: pl.BlockSpec((tm, w), lambda i: (tile(i), 0))
    tab = pl.BlockSpec((tm, LANES), lambda i: (tile(i) % tab_blocks, 0))
    whole = lambda w: pl.BlockSpec((ms, w), lambda i: (0, 0))
    n_prev = 0 if prev_kv is None else prev_kv[0].shape[0]
    stacked = lambda n: pl.BlockSpec((n, None, W_ATT, tm),
                                     lambda i: (0, tile(i) // tiles_per_seq, 0, 0))
    prev_specs = [stacked(n_prev)] * 2 if n_prev else []
    groups = (4 * W_RET, 4 * W_GLA, W_GLA, 3 * W_ATT)
    dtypes = (BF16, BF16, F32, BF16)
    kern = functools.partial(_in_proj_kernel, n_long_tiles=n_tiles, tiles_per_seq=tiles_per_seq,
                             n_prev=n_prev)
    outs = pl.pallas_call(
        kern,
        grid=(n_tiles + 1,),
        in_specs=[row(D_MODEL), whole(D_MODEL), at_layer(n1w), at_layer(w_in), at_layer(w_a2),
                  at_layer(ba), tab, tab, whole(LANES), whole(LANES)] + later_in + prev_specs,
        out_specs=([row(w) for w in groups] + [stacked(n_prev + 1)] * 2
                   + [whole(w) for w in groups] + [whole(W_ATT)] * 2 + later_out),
        out_shape=([jax.ShapeDtypeStruct((m, w), d) for w, d in zip(groups, dtypes)]
                   + [jax.ShapeDtypeStruct((n_prev + 1, m // seq_len, W_ATT, tm), F32)] * 2
                   + [jax.ShapeDtypeStruct((ms, w), d) for w, d in zip(groups, dtypes)]
                   + [jax.ShapeDtypeStruct((ms, W_ATT), F32)] * 2 + later_shape),
        scratch_shapes=[pltpu.VMEM((4 * W_RET, D_MODEL), BF16),
                        pltpu.VMEM((4 * W_GLA + LOW_PAD, D_MODEL), BF16),
                        pltpu.VMEM((3 * W_ATT, D_MODEL), BF16),
                        pltpu.VMEM((LOW_PAD, W_GLA), BF16)],
        compiler_params=_params(1),
        name="in_proj",
    )(x_long, x_short, n1w, w_in, w_a2, ba, *tabs_long, *tabs_short, w_out, w_up, w_down,
      *(prev_kv if n_prev else ()))
    return outs[:6], outs[6:12], outs[12:]


def _init_states(s_scr, s0_ref, n_seq, n_heads):
    if s0_ref is None:
        s_scr[...] = jnp.zeros_like(s_scr)
        return
    zero = jnp.zeros((HEAD_DIM, HEAD_DIM), F32)
    for i in range(n_seq):
        s_scr[i] = jnp.concatenate(
            [jnp.concatenate([s0_ref[i, h] if j == h else zero for j in range(n_heads)], axis=1)
             for h in range(n_heads)], axis=0)


def _store_states(s_scr, out_ref, n_seq, n_heads):
    for i in range(n_seq):
        s = s_scr[i]
        for h in range(n_heads):
            sl = slice(h * HEAD_DIM, (h + 1) * HEAD_DIM)
            out_ref[i, h] = s[sl, sl]


def _chunk_rows(lc, n_chunks, n_seq):
    return [[slice((i * n_chunks + c) * lc, (i * n_chunks + c + 1) * lc)
             for c in range(n_chunks)] for i in range(n_seq)]


def _retention_body(q_ref, k_ref, v_ref, s0_ref, tab_refs, o_ref, sfin_ref, s_scr,
                    *, lc, n_chunks, n_seq):
    dtab_ref, qdec_ref, kdec_ref, gtab_ref, bd_ref = tab_refs
    g = pl.program_id(1)

    @pl.when(g == 0)
    def _():
        _init_states(s_scr, s0_ref, n_seq, H_RET)

    dtab = dtab_ref[...]
    flat = lambda ref: ref[...].reshape(n_seq * n_chunks * lc, ref.shape[-1])
    q = flat(q_ref)
    k = flat(k_ref)
    v = flat(v_ref)
    kd = (k.astype(F32) * kdec_ref[...]).astype(BF16)
    gtab = gtab_ref[...]
    bd = bd_ref[...]
    for i, rows in enumerate(_chunk_rows(lc, n_chunks, n_seq)):
        states = [s_scr[i]]
        for r in rows:
            upd = _dot_tn(kd[r, :], v[r, :])
            states.append(states[-1] * gtab + upd * bd)
        s_scr[i] = states[-1]
        for c, r in enumerate(rows):
            scores = _dot_nt(q[r, :], _block_diag_rows(k[r, :], H_RET))
            p = (scores * dtab).astype(BF16)
            intra = _dot(p, _block_diag_rows(v[r, :], H_RET))
            inter = _dot(q[r, :], states[c].astype(BF16))
            o_ref[i, c * lc:(c + 1) * lc, :] = intra + inter * qdec_ref[r, :]

    @pl.when(g == pl.num_programs(1) - 1)
    def _():
        _store_states(s_scr, sfin_ref, n_seq, H_RET)


def _retention_tables(lc, n_chunks):
    lg = np.log1p(-np.exp2(-5.0 - np.arange(H_RET, dtype=np.float64)))
    idx = np.arange(lc)
    diff = idx[:, None] - idx[None, :]
    causal = diff >= 0
    dpos = np.where(causal, diff, 0).astype(np.float64)
    decay = np.where(causal[None], np.exp(dpos[None] * lg[:, None, None]), 0.0)
    dtab = np.transpose(decay, (1, 0, 2)).reshape(lc, H_RET * lc)
    q_decay = np.exp((idx + 1).astype(np.float64)[:, None] * lg[None, :])
    k_decay = np.exp((lc - 1 - idx).astype(np.float64)[:, None] * lg[None, :])
    qdec = np.tile(np.repeat(q_decay, HEAD_DIM, axis=1), (n_chunks, 1))
    kdec = np.tile(np.repeat(k_decay, HEAD_DIM, axis=1), (n_chunks, 1))
    gtab = np.broadcast_to(np.repeat(np.exp(lc * lg), HEAD_DIM)[:, None], (W_RET, W_RET))
    head = np.arange(W_RET) // HEAD_DIM
    bd = head[:, None] == head[None, :]
    return tuple(jnp.asarray(t, F32) for t in (dtab, qdec, kdec, gtab, bd))


def _split_bf16(x):
    hi = x.astype(BF16)
    lo = (x - hi.astype(F32)).astype(BF16)
    return jnp.concatenate([hi, lo], axis=1)


def _group_reference(b, row, half):
    n, w = b.shape
    group = 2 * half
    if group >= 8:
        b3 = b.reshape(n // group, group, w)
        return jnp.broadcast_to(b3[:, half - 1:half, :], b3.shape).reshape(n, w)
    if group == 4:
        i = row & 3
        return jnp.where(i == 0, pltpu.roll(b, n - 1, 0),
                         jnp.where(i == 1, b,
                                   jnp.where(i == 2, pltpu.roll(b, 1, 0), pltpu.roll(b, 2, 0))))
    return jnp.where((row & 1) == 1, pltpu.roll(b, 1, 0), b)


def _gla_body(q_ref, k_ref, v_ref, lf_ref, s0_ref, tab_refs, o_ref, sfin_ref, s_scr, sc_scr,
              *, lc, n_chunks, n_seq):
    tri_ref, bd_ref, onehot_ref = tab_refs
    g = pl.program_id(1)

    @pl.when(g == 0)
    def _():
        _init_states(s_scr, s0_ref, n_seq, H_GLA)

    n_lev = lc.bit_length() - 1
    w = W_GLA
    n = lc * n_chunks * n_seq
    seq_rows = _chunk_rows(lc, n_chunks, n_seq)
    chunk_rows = [r for rows in seq_rows for r in rows]
    row = lax.broadcasted_iota(jnp.int32, (n, w), 0)
    trow = lax.broadcasted_iota(jnp.int32, (lc, H_GLA * lc), 0)
    scol = lax.broadcasted_iota(jnp.int32, (lc, H_GLA * lc), 1) & (lc - 1)
    flat = lambda ref: ref[...].reshape(n, ref.shape[-1])
    q = flat(q_ref)
    k = flat(k_ref)
    v = flat(v_ref)
    q32 = q.astype(F32)
    k32 = k.astype(F32)
    tri = tri_ref[...]
    lfs = _split_bf16(flat(lf_ref))
    cs = jnp.concatenate([_dot(tri, lfs[r, :]) for r in chunk_rows], axis=0)
    b = cs[:, :w] + cs[:, w:]

    qe = (q32 * jnp.exp(b)).astype(BF16)
    mild = jnp.min(b) >= -GLA_SAFE_LOG_DECAY

    @pl.when(mild)
    def _():
        ke = (k32 * jnp.exp(-b)).astype(BF16)
        for r in chunk_rows:
            sc = _dot_nt(qe[r, :], _block_diag_rows(ke[r, :], H_GLA))
            sc_scr[r, :] = jnp.where(trow >= scol, sc, 0.0)

    @pl.when(jnp.logical_not(mild))
    def _():
        for r in chunk_rows:
            sc = _dot_nt(q[r, :], _block_diag_rows(k[r, :], H_GLA))
            sc_scr[r, :] = jnp.where(trow == scol, sc, 0.0)
        for lev in range(n_lev):
            half = lc >> (lev + 1)
            shift = half.bit_length()
            upper = (row & half) != 0
            e = jnp.exp(-jnp.abs(b - _group_reference(b, row, half)))
            z = (jnp.where(upper, q32, k32) * e).astype(BF16)
            valid = (((trow >> shift) == (scol >> shift))
                     & ((trow & half) != 0) & ((scol & half) == 0))
            for r in chunk_rows:
                zc = z[r, :]
                sc = _dot_nt(zc, _block_diag_rows(zc, H_GLA))
                sc_scr[r, :] = jnp.where(valid, sc, sc_scr[r, :])

    b3 = b.reshape(n_chunks * n_seq, lc, w)
    b_last = b3[:, lc - 1:lc, :]
    kd = (k32 * jnp.exp(jnp.broadcast_to(b_last, b3.shape).reshape(n, w) - b)).astype(BF16)
    col_sum = _dot_tn(lfs, onehot_ref[...])
    decay_col = jnp.exp(col_sum[:w, :] + col_sum[w:, :])
    bd = bd_ref[...]
    for i, rows in enumerate(seq_rows):
        states = [s_scr[i]]
        for c, r in enumerate(rows):
            j = i * n_chunks + c
            upd = _dot_tn(kd[r, :], v[r, :])
            states.append(states[-1] * decay_col[:, j:j + 1] + upd * bd)
        s_scr[i] = states[-1]
        for c, r in enumerate(rows):
            inter = _dot(qe[r, :], states[c].astype(BF16))
            intra = _dot(sc_scr[r, :].astype(BF16), _block_diag_rows(v[r, :], H_GLA))
            o_ref[i, c * lc:(c + 1) * lc, :] = intra + inter

    @pl.when(g == pl.num_programs(1) - 1)
    def _():
        _store_states(s_scr, sfin_ref, n_seq, H_GLA)


def _linear_mixers_kernel(*refs, lc, n_chunks, n_seq, has_init):
    refs = list(refs)
    take = lambda k: [refs.pop(0) for _ in range(k)]
    rq, rk, rv, gq, gk, gv, lf = take(7)
    s0r, s0g = take(2) if has_init else (None, None)
    ret_tabs = take(5)
    gla_tabs = take(3)
    oa_ref, sr_ref, ob_ref, sg_ref, sr_scr, sg_scr, sc_scr = take(7)
    dims = dict(lc=lc, n_chunks=n_chunks, n_seq=n_seq)
    _retention_body(rq, rk, rv, s0r, ret_tabs, oa_ref, sr_ref, sr_scr, **dims)
    _gla_body(gq, gk, gv, lf, s0g, gla_tabs, ob_ref, sg_ref, sg_scr, sc_scr, **dims)


def _linear_mixers(ret, gla, lf, states, layer, *, batch, seq_len, lc, n_chunks, n_seq):
    rows_seq = lc * n_chunks
    rows = rows_seq * n_seq
    ng = seq_len // rows_seq
    m = batch * seq_len
    width = W_RET
    by_seq = lambda a: a.reshape(batch, seq_len, a.shape[-1])
    ret, gla, lf = by_seq(ret), by_seq(gla), by_seq(lf)
    col = lambda c: pl.BlockSpec((n_seq, rows_seq, width), lambda b, g: (b, g, c))
    full = lambda a: pl.BlockSpec(a.shape, lambda b, g: (0,) * a.ndim)
    blk = (n_seq, H_RET, HEAD_DIM, HEAD_DIM)
    s_out_spec = pl.BlockSpec(blk, lambda b, g: (b, 0, 0, 0))
    s_out_shape = jax.ShapeDtypeStruct((batch, H_RET, HEAD_DIM, HEAD_DIM), F32)
    s_in_specs = []
    if states is not None:
        s_in_specs = [pl.BlockSpec((None,) + blk, lambda b, g: (layer, b, 0, 0, 0))] * 2
    ret_tabs = _retention_tables(lc, n_chunks * n_seq)
    gla_tabs = _gla_tables(lc, n_chunks * n_seq)
    kern = functools.partial(_linear_mixers_kernel, lc=lc, n_chunks=n_chunks, n_seq=n_seq,
                             has_init=states is not None)
    out_rows = col(0)
    out_shape = jax.ShapeDtypeStruct((batch, seq_len, width), F32)
    oa, s_ret, ob, s_gla = pl.pallas_call(
        kern,
        grid=(batch // n_seq, ng),
        in_specs=([col(0), col(1), col(2), col(0), col(1), col(2), col(0)] + s_in_specs
                  + [full(t) for t in ret_tabs + gla_tabs]),
        out_specs=[out_rows, s_out_spec, out_rows, s_out_spec],
        out_shape=[out_shape, s_out_shape, out_shape, s_out_shape],
        scratch_shapes=[pltpu.VMEM((n_seq, width, width), F32),
                        pltpu.VMEM((n_seq, width, width), F32),
                        pltpu.VMEM((rows, H_GLA * lc), F32)],
        compiler_params=_params(2),
        name="linear_mixers",
    )(ret, ret, ret, gla, gla, gla, lf, *(states or ()), *ret_tabs, *gla_tabs)
    return oa.reshape(m, width), s_ret, ob.reshape(m, width), s_gla


def _gla_tables(lc, n_chunks):
    idx = np.arange(lc)
    tri = idx[:, None] >= idx[None, :]
    head = np.arange(W_GLA) // HEAD_DIM
    bd = head[:, None] == head[None, :]
    chunk_of_row = np.arange(lc * n_chunks) // lc
    onehot = chunk_of_row[:, None] == np.arange(LANES)[None, :]
    return jnp.asarray(tri, BF16), jnp.asarray(bd, F32), jnp.asarray(onehot, BF16)


def _band_attn_kernel(q_ref, kc_ref, vc_ref, gv_ref, hones_ref, o_ref,
                      kwin_t, vwin, bias_scr, s_scr, m_scr, ksq_prev, *, lc, rpi, n_iter):
    b_id = pl.program_id(0)
    g = pl.program_id(1)
    win = ATT_REACH + rpi
    rows = rpi * n_iter

    @pl.when((b_id == 0) & (g == 0))
    def _():
        kwin_t[:, 0:rows] = jnp.zeros((W_ATT, rows), BF16)
        vwin[0:rows, :] = jnp.zeros((rows, W_ATT), BF16)
        ksq_prev[0] = 0.0
        row = lax.broadcasted_iota(jnp.int32, (rpi, BIAS_W), 0)
        rel = lax.broadcasted_iota(jnp.int32, (rpi, BIAS_W), 1) - (row // lc) * lc
        in_band = (rel >= 0) & (rel < ATT_REACH + lc)
        for h in range(H_ATT):
            x = jnp.broadcast_to(gv_ref[h:h + 1, :], (rpi, BIAS_W))
            x = pltpu.roll(x, BIAS_W - (lc - 1), 1, stride=1, stride_axis=0)
            bias_scr[h // 2, (h % 2) * rpi:(h % 2 + 1) * rpi, :] = jnp.where(
                in_band, x * LOG2E, NEG_BIG)

    base = pl.multiple_of(g * rows, rows)
    k_cur = kc_ref[...].astype(F32)
    kwin_t[:, pl.ds(base + rows, rows)] = k_cur.T.astype(BF16)
    vwin[pl.ds(base + rows, rows), :] = vc_ref[...]

    head_ones = hones_ref[...]
    q_cur = q_ref[...].astype(F32)
    q_sq = jnp.max(_dot((q_cur * q_cur).astype(BF16), head_ones))
    k_sq_cur = jnp.max(_dot((k_cur * k_cur).astype(BF16), head_ones))
    k_sq = jnp.maximum(k_sq_cur, jnp.where(g == 0, 0.0, ksq_prev[0]))
    ksq_prev[0] = k_sq_cur
    room = ATT_SAFE_LOG2 - jnp.max(jnp.abs(gv_ref[...])) * LOG2E
    bounded = (room > 0.0) & (q_sq * k_sq * NORM_ROUNDING_MARGIN <= room * room)

    lane = lax.broadcasted_iota(jnp.int32, (rpi, LANES), 1)
    lane2 = lax.broadcasted_iota(jnp.int32, (2 * rpi, LANES), 1)
    head2 = lax.broadcasted_iota(jnp.int32, (2 * rpi, LANES), 0) // rpi
    col = lax.broadcasted_iota(jnp.int32, (2 * rpi, win), 1)

    def step(i, carry, *, masked, small):
        r0 = pl.multiple_of(i * rpi, rpi)
        first_valid = ATT_REACH - (g * rows + r0)

        def scores(pair):
            lanes = slice(pair * LANES, (pair + 1) * LANES)
            qp = q_ref[pl.ds(r0, rpi), lanes]
            q2 = jnp.concatenate([qp, qp], axis=0)
            q2 = jnp.where((lane2 // HEAD_DIM) == head2, q2, jnp.zeros_like(q2))
            s = _dot(q2, kwin_t[lanes, pl.ds(base + r0, win)]) + bias_scr[pair][:, :win]
            if masked:
                s = jnp.where(col >= first_valid, s, NEG_BIG)
            return s

        def finish(pair, e):
            lanes = slice(pair * LANES, (pair + 1) * LANES)
            den = jnp.sum(e, axis=-1, keepdims=True)
            pv = _dot(e.astype(BF16), vwin[pl.ds(base + r0, win), lanes]) / den
            o_ref[pl.ds(r0, rpi), lanes] = jnp.where(
                lane < HEAD_DIM, pv[:rpi, :], pv[rpi:, :]).astype(BF16)

        if small:
            for pair in range(H_ATT // 2):
                finish(pair, jnp.exp2(scores(pair)))
        else:
            for pair in range(H_ATT // 2):
                s = scores(pair)
                s_scr[pair, :, :win] = s
                m_scr[pair] = jnp.broadcast_to(
                    jnp.max(s, axis=-1, keepdims=True), (2 * rpi, LANES))
            for pair in range(H_ATT // 2):
                finish(pair, jnp.exp2(s_scr[pair, :, :win] - m_scr[pair][:, :1]))
        return carry

    def run(masked):
        @pl.when(bounded)
        def _():
            lax.fori_loop(0, n_iter, functools.partial(step, masked=masked, small=True), 0,
                          unroll=4)

        @pl.when(jnp.logical_not(bounded))
        def _():
            lax.fori_loop(0, n_iter, functools.partial(step, masked=masked, small=False), 0)

    @pl.when(g == 0)
    def _():
        run(True)

    @pl.when(g != 0)
    def _():
        run(False)


def _band_attn(att, gv, layer, *, batch, seq_len, lc, rpi, n_iter):
    rows = rpi * n_iter
    assert rows == ATT_REACH, "the carried window is exactly one row group"
    ng = seq_len // rows
    m = batch * seq_len
    cur = lambda c: pl.BlockSpec((rows, W_ATT), lambda b, g: (b * ng + g, c))
    kern = functools.partial(_band_attn_kernel, lc=lc, rpi=rpi, n_iter=n_iter)
    head_ones = _head_ones()
    return pl.pallas_call(
        kern,
        grid=(batch, ng),
        in_specs=[cur(0), cur(1), cur(2), _layer_spec(gv, layer),
                  pl.BlockSpec(head_ones.shape, lambda b, g: (0, 0))],
        out_specs=pl.BlockSpec((rows, W_ATT), lambda b, g: (b * ng + g, 0)),
        out_shape=jax.ShapeDtypeStruct((m, W_ATT), BF16),
        scratch_shapes=[pltpu.VMEM((W_ATT, (ng + 1) * rows), BF16),
                        pltpu.VMEM(((ng + 1) * rows, W_ATT), BF16),
                        pltpu.VMEM((H_ATT // 2, 2 * rpi, BIAS_W), F32),
                        pltpu.VMEM((H_ATT // 2, 2 * rpi, BIAS_W), F32),
                        pltpu.VMEM((H_ATT // 2, 2 * rpi, LANES), F32),
                        pltpu.SMEM((1,), F32)],
        compiler_params=_params(2),
        name="band_attn",
    )(att, att, att, gv, head_ones)


def _sample_attn_kernel(q_ref, kn_ref, vn_ref, kt_ref, vt_ref, gv_ref, hones_ref, o_ref,
                        *, ls, n_seq):
    win = ATT_REACH + ls
    lane = lax.broadcasted_iota(jnp.int32, (ls, LANES), 1)
    lane2 = lax.broadcasted_iota(jnp.int32, (2 * ls, LANES), 1)
    head2 = lax.broadcasted_iota(jnp.int32, (2 * ls, LANES), 0) // ls

    head_ones = hones_ref[...]
    q32 = q_ref[...].astype(F32)
    kn32 = kn_ref[...].astype(F32)
    kc32 = kt_ref[...]
    q_sq = jnp.max(_dot((q32 * q32).astype(BF16), head_ones))
    k_sq = jnp.maximum(jnp.max(_dot((kn32 * kn32).astype(BF16), head_ones)),
                       jnp.max(jnp.sum(kc32 * kc32, axis=2)))
    room = ATT_SAFE_LOG2 - jnp.max(jnp.abs(gv_ref[...])) * LOG2E
    bounded = (room > 0.0) & (q_sq * k_sq * NORM_ROUNDING_MARGIN <= room * room)

    def head_pair(seq, pair, small):
        lanes = slice(pair * LANES, (pair + 1) * LANES)
        rows = slice(seq * ls, (seq + 1) * ls)
        qp = q_ref[rows, lanes]
        q2 = jnp.concatenate([qp, qp], axis=0)
        q2 = jnp.where((lane2 // HEAD_DIM) == head2, q2, jnp.zeros_like(q2))
        heads = slice(2 * pair, 2 * pair + 2)
        kt = kt_ref[seq, heads].reshape(2 * HEAD_DIM, ATT_REACH).astype(BF16)
        vt = vt_ref[seq, heads].reshape(2 * HEAD_DIM, ATT_REACH).astype(BF16)
        bias = jnp.concatenate(
            [pltpu.roll(jnp.broadcast_to(gv_ref[h:h + 1, :], (ls, BIAS_W)),
                        BIAS_W - (ls - 1), 1, stride=1, stride_axis=0)
             for h in (2 * pair, 2 * pair + 1)], axis=0) * LOG2E
        s_old = _dot(q2, kt) + bias[:, :ATT_REACH]
        s_new = _dot_nt(q2, kn_ref[rows, lanes]) + bias[:, ATT_REACH:win]
        if not small:
            mx = jnp.maximum(jnp.max(s_old, axis=-1, keepdims=True),
                             jnp.max(s_new, axis=-1, keepdims=True))
            s_old = s_old - mx
            s_new = s_new - mx
        e_old = jnp.exp2(s_old)
        e_new = jnp.exp2(s_new)
        den = jnp.sum(e_old, axis=-1, keepdims=True) + jnp.sum(e_new, axis=-1, keepdims=True)
        pv = (_dot_nt(e_old.astype(BF16), vt)
              + _dot(e_new.astype(BF16), vn_ref[rows, lanes])) / den
        o_ref[rows, lanes] = jnp.where(lane < HEAD_DIM, pv[:ls, :], pv[ls:, :]).astype(BF16)

    @pl.when(bounded)
    def _():
        for seq in range(n_seq):
            for pair in range(H_ATT // 2):
                head_pair(seq, pair, True)

    @pl.when(jnp.logical_not(bounded))
    def _():
        for seq in range(n_seq):
            for pair in range(H_ATT // 2):
                head_pair(seq, pair, False)


def _head_ones():
    lane_head = np.arange(W_ATT) // HEAD_DIM
    return jnp.asarray(lane_head[:, None] == np.arange(LANES)[None, :], BF16)


def _sample_attn(att, cache_kt, cache_vt, layer, gv, *, batch, ls):
    n_seq = SAMPLE_ATT_SEQS_PER_STEP
    new = lambda c: pl.BlockSpec((n_seq * ls, W_ATT), lambda b: (b, c))
    old = pl.BlockSpec((None, n_seq, H_ATT, HEAD_DIM, ATT_REACH),
                       lambda b: (layer, b, 0, 0, 0))
    head_ones = _head_ones()
    kern = functools.partial(_sample_attn_kernel, ls=ls, n_seq=n_seq)
    return pl.pallas_call(
        kern,
        grid=(batch // n_seq,),
        in_specs=[new(0), new(1), new(2), old, old, _layer_spec(gv, layer),
                  pl.BlockSpec(head_ones.shape, lambda b: (0, 0))],
        out_specs=pl.BlockSpec((n_seq * ls, W_ATT), lambda b: (b, 0)),
        out_shape=jax.ShapeDtypeStruct((batch * ls, W_ATT), BF16),
        compiler_params=_params(1),
        name="sample_attn",
    )(att, att, att, cache_kt, cache_vt, gv, head_ones)


def _bias_rows(rel_bias_l, lc):
    wv = np.arange(BIAS_W)
    dist = ATT_REACH + lc - 1 - wv
    ridx = np.clip(dist, -(CHUNK - 1), REL_MAX) + (CHUNK - 1)
    n_far = int(np.argmax(ridx < N_REL - 1)) - 1
    n_near = BIAS_W - n_far - N_REL
    assert np.array_equal(
        ridx, np.concatenate([np.full(n_far, N_REL - 1), np.arange(N_REL - 1, -1, -1),
                              np.zeros(n_near, np.int64)]))
    lead = rel_bias_l.shape[:-1]
    return jnp.concatenate(
        [jnp.broadcast_to(rel_bias_l[..., N_REL - 1:], lead + (n_far,)),
         jnp.flip(rel_bias_l, axis=-1),
         jnp.broadcast_to(rel_bias_l[..., :1], lead + (n_near,))], axis=-1)


def _mix_rows(x_ref, oa_ref, ob_ref, oc_ref, sga_ref, sgb_ref, params, y_ref, *, final_norm):
    gnw_ref, glw_ref, n2w_ref, fnw_ref, ones_ref, wout_ref, wup_ref, wdown_ref = params
    ones = ones_ref[...]

    def head_sum(t):
        return _dot(t.astype(BF16), ones)

    inv_d = 1.0 / HEAD_DIM
    oa = oa_ref[...]
    mu = head_sum(oa) * inv_d
    da = oa - mu
    var = head_sum(da * da) * inv_d
    a = da * lax.rsqrt(var + EPS) * gnw_ref[...] * sga_ref[...].astype(F32)
    ob = ob_ref[...]
    ms = head_sum(ob * ob) * inv_d
    bb = ob * lax.rsqrt(ms + EPS) * glw_ref[...] * sgb_ref[...].astype(F32)
    cat = jnp.concatenate([a.astype(BF16), bb.astype(BF16), oc_ref[...]], axis=1)
    h1 = x_ref[...] + _dot(cat, wout_ref[...])
    hn = _rmsnorm(h1, n2w_ref[...]).astype(BF16)
    acc = jnp.zeros_like(h1)
    n_slab = D_FF // D_MODEL
    for j in range(n_slab):
        sl = slice(j * D_MODEL, (j + 1) * D_MODEL)
        u = jnp.maximum(_dot(hn, wup_ref[:, sl]), 0.0)
        acc = acc + _dot((u * u).astype(BF16), wdown_ref[sl, :])
    h2 = h1 + acc
    if final_norm:
        h2 = _rmsnorm(h2, fnw_ref[...])
    y_ref[...] = h2


def _out_ffn_kernel(*refs, n_long_tiles, final_norm):
    rows_l, rows_s, params = refs[0:6], refs[6:12], refs[12:20]
    y_l, y_s = refs[20:22]
    i = pl.program_id(0)

    @pl.when(i < n_long_tiles)
    def _():
        _mix_rows(*rows_l, params, y_l, final_norm=final_norm)

    @pl.when(i == n_long_tiles)
    def _():
        _mix_rows(*rows_s, params, y_s, final_norm=final_norm)


def _out_ffn(long_in, short_in, gnw, glw, n2w, fnw, ones, wout, wup, wdown,
             *, layer, tm, final_norm):
    m = long_in[0].shape[0]
    ms = short_in[0].shape[0]
    n_tiles = m // tm
    full = lambda a: pl.BlockSpec(a.shape, lambda i: (0,) * a.ndim)
    at_layer = functools.partial(_layer_spec, layer=layer)
    tile = lambda i: jnp.minimum(i, n_tiles - 1)
    row = lambda w, c=0: pl.BlockSpec((tm, w), lambda i: (tile(i), c))
    whole = lambda w, c=0: pl.BlockSpec((ms, w), lambda i: (0, c))
    row_specs = lambda spec: [spec(D_MODEL), spec(W_RET), spec(W_GLA), spec(W_ATT),
                              spec(W_RET, 3), spec(W_GLA, 3)]
    kern = functools.partial(_out_ffn_kernel, n_long_tiles=n_tiles, final_norm=final_norm)
    return pl.pallas_call(
        kern,
        grid=(n_tiles + 1,),
        in_specs=(row_specs(row) + row_specs(whole)
                  + [at_layer(gnw), at_layer(glw), at_layer(n2w), full(fnw), full(ones),
                     full(wout), full(wup), full(wdown)]),
        out_specs=[row(D_MODEL), whole(D_MODEL)],
        out_shape=[jax.ShapeDtypeStruct((m, D_MODEL), F32),
                   jax.ShapeDtypeStruct((ms, D_MODEL), F32)],
        compiler_params=_params(1),
        name="out_ffn",
    )(*long_in, *short_in, gnw, glw, n2w, fnw, ones, wout, wup, wdown)


def _rotary_tables(pos, rows):
    half = HEAD_DIM // 2
    inv_freq = ROPE_BASE ** (-np.arange(half, dtype=np.float64) / half)
    ang = np.asarray(pos, np.float64)[:, None] * inv_freq[None, :]
    reps = (rows // len(pos), LANES // half)
    return (jnp.asarray(np.tile(np.cos(ang), reps), F32),
            jnp.asarray(np.tile(np.sin(ang), reps), F32))


def _mixers(proj, rel_bias, *, batch, seq_len, lc, n_chunks, layer, states, caches):
    ret, gla, lf, att = proj
    n_seq = batch if lc * n_chunks == seq_len else LINMIX_SEQS_PER_STEP
    oa, s_ret, ob, s_gla = _linear_mixers(
        ret, gla, lf, states, layer, batch=batch, seq_len=seq_len, lc=lc, n_chunks=n_chunks,
        n_seq=n_seq)
    gv = _bias_rows(rel_bias, lc)
    if caches is None:
        rpi = min(lc * n_chunks, ATT_ROWS_PER_STEP)
        oc = _band_attn(att, gv, layer, batch=batch, seq_len=seq_len, lc=lc, rpi=rpi,
                        n_iter=lc * n_chunks // rpi)
    else:
        oc = _sample_attn(att, caches[0], caches[1], layer, gv, batch=batch, ls=seq_len)
    return (oa, ob, oc, ret, gla), s_ret, s_gla


def kernel(x_prompt, x_sample, state_ret, state_gla, cache_attn_k, cache_attn_v, norm1_w, norm2_w, final_norm_w, w_in, w_gla_a2, b_gla_a, ret_gn_w, gla_norm_w, rel_bias, w_out, w_up, w_down):
    depth = w_in.shape[0]
    bp, lp, _ = x_prompt.shape
    bs, ls, _ = x_sample.shape
    keep = min(ATT_REACH, lp)
    head = np.arange(W_RET) // HEAD_DIM
    ones = jnp.asarray(head[:, None] == head[None, :], BF16)
    hp = x_prompt.reshape(bp * lp, D_MODEL)
    hs = x_sample.reshape(bs * ls, D_MODEL)
    pos_p = np.arange(lp)
    pos_s = PAST_LEN + np.arange(ls)
    tm_p = keep
    to_head_major = lambda c: jnp.transpose(c, (0, 1, 3, 4, 2))
    caches = (to_head_major(cache_attn_k.astype(F32)), to_head_major(cache_attn_v.astype(F32)))
    from_head_major = lambda t: jnp.transpose(
        t.reshape(depth, bp, H_ATT, HEAD_DIM, keep), (0, 1, 4, 2, 3))
    states = (state_ret.astype(F32), state_gla.astype(F32))
    outs = {k: [] for k in ("ret_p", "gla_p", "ret_s", "gla_s", "ks", "vs")}
    prompt_kv = None
    rows3 = lambda v: v.astype(F32).reshape(depth, 1, v.shape[-1])
    n1w, n2w, fnw = rows3(norm1_w), rows3(norm2_w), final_norm_w.astype(F32)[None, :]
    gnw, glw, ba = rows3(ret_gn_w), rows3(gla_norm_w), rows3(b_gla_a)
    w_in_t = jnp.swapaxes(w_in.astype(F32), 1, 2)
    w_a2 = w_gla_a2.astype(F32)
    bias_tab = rel_bias.astype(F32)
    later_w = (w_out.astype(F32), w_up.astype(F32), w_down.astype(F32))
    tabs_p = _rotary_tables(pos_p, lp)
    tabs_s = _rotary_tables(pos_s, bs * ls)
    for l in range(depth):
        last = l == depth - 1
        proj_p, proj_s, (wout, wup, wdown) = _in_proj(
            hp, hs, n1w, w_in_t, w_a2, ba, tabs_p, tabs_s, prompt_kv, later_w,
            layer=l, tm=tm_p, seq_len=lp)
        prompt_kv = proj_p[4:6]
        mix_p, s_ret, s_gla = _mixers(
            proj_p[:4], bias_tab, batch=bp, seq_len=lp, lc=CHUNK, n_chunks=ATT_REACH // CHUNK,
            layer=l, states=None, caches=None)
        outs["ret_p"].append(s_ret)
        outs["gla_p"].append(s_gla)
        mix_s, s_ret, s_gla = _mixers(
            proj_s[:4], bias_tab, batch=bs, seq_len=ls, lc=ls, n_chunks=1,
            layer=l, states=states, caches=caches)
        outs["ret_s"].append(s_ret)
        outs["gla_s"].append(s_gla)
        outs["ks"].append(proj_s[4].reshape(bs, ls, H_ATT, HEAD_DIM))
        outs["vs"].append(proj_s[5].reshape(bs, ls, H_ATT, HEAD_DIM))
        hp, hs = _out_ffn((hp,) + mix_p, (hs,) + mix_s, gnw, glw, n2w, fnw, ones,
                          wout, wup, wdown, layer=l, tm=tm_p, final_norm=last)
    y_prompt = hp.reshape(bp, lp, D_MODEL)
    y_sample = hs.reshape(bs, ls, D_MODEL)
    st = lambda k: jnp.stack(outs[k])
    return (y_prompt, y_sample, st("ret_p"), st("gla_p"),
            from_head_major(prompt_kv[0]), from_head_major(prompt_kv[1]),
            st("ret_s"), st("gla_s"), st("ks"), st("vs"))
```

```python
import functools

import numpy as np
import jax
import jax.numpy as jnp
from jax import lax
from jax.experimental import pallas as pl
from jax.experimental.pallas import tpu as pltpu

D_MODEL = 1024
CHUNK = 64
HEAD_DIM = 64
H_RET = 4
H_GLA = 4
H_ATT = 8
W_RET = H_RET * HEAD_DIM
W_GLA = H_GLA * HEAD_DIM
W_ATT = H_ATT * HEAD_DIM
GLA_RANK = 16
GLA_TAU = 16.0
ATT_REACH = 512
REL_MAX = 256
N_REL = CHUNK + REL_MAX
D_FF = 4 * D_MODEL
IN_COLS = 4 * W_RET + 4 * W_GLA + GLA_RANK + 3 * W_ATT
ROPE_BASE = 10000.0
EPS = 1e-6
PAST_LEN = 4096

LANES = 128
LOW_PAD = LANES
BIAS_W = 640
GLA_SAFE_LOG_DECAY = 50.0
ATT_SAFE_LOG2 = 80.0
NORM_ROUNDING_MARGIN = 1.05
LINMIX_SEQS_PER_STEP = 4
SAMPLE_ATT_SEQS_PER_STEP = 4
ATT_ROWS_PER_STEP = 128
LOG2E = 1.4426950408889634
NEG_BIG = -1e30
VMEM_LIMIT = 58 * 1024 * 1024

F32 = jnp.float32
BF16 = jnp.bfloat16

_NT = (((1,), (1,)), ((), ()))
_TN = (((0,), (0,)), ((), ()))


def _dot(a, b):
    return jnp.dot(a, b, preferred_element_type=F32)


def _dot_nt(a, b):
    return lax.dot_general(a, b, _NT, preferred_element_type=F32)


def _dot_tn(a, b):
    return lax.dot_general(a, b, _TN, preferred_element_type=F32)


def _rmsnorm(x, w):
    return x * lax.rsqrt(jnp.mean(x * x, axis=-1, keepdims=True) + EPS) * w


def _params(n_grid_dims):
    return pltpu.CompilerParams(
        dimension_semantics=("arbitrary",) * n_grid_dims,
        vmem_limit_bytes=VMEM_LIMIT,
    )


def _layer_spec(a, layer):
    index = (layer,) + (0,) * (a.ndim - 1)
    return pl.BlockSpec((None,) + a.shape[1:], lambda *_: index, pipeline_mode=pl.Buffered(1))


def _block_diag_rows(x, n_heads):
    lane_head = lax.broadcasted_iota(jnp.int32, x.shape, 1) // HEAD_DIM
    zero = jnp.zeros_like(x)
    return jnp.concatenate(
        [jnp.where(lane_head == h, x, zero) for h in range(n_heads)], axis=0)


def _project_rows(x_ref, cos_ref, sin_ref, n1w_ref, ba_ref, weights,
                  ret_ref, gla_ref, lf_ref, att_ref, keys_t_refs=(), values_t_ref=None):
    wret_ref, wgla_ref, watt_ref, wa2_ref = weights
    hn = _rmsnorm(x_ref[...], n1w_ref[...]).astype(BF16)
    cos = cos_ref[...]
    sin = sin_ref[...]
    first_half = (lax.broadcasted_iota(jnp.int32, cos.shape, 1) % HEAD_DIM) < (HEAD_DIM // 2)

    def rotary(t):
        swapped = jnp.where(first_half, -pltpu.roll(t, LANES - HEAD_DIM // 2, 1),
                            pltpu.roll(t, HEAD_DIM // 2, 1))
        return t * cos + swapped * sin

    scale = HEAD_DIM ** -0.5

    acc = _dot_nt(hn, watt_ref[...])
    att_ref[:, 0:W_ATT] = (acc[:, 0:W_ATT] * (scale * LOG2E)).astype(BF16)
    att_ref[:, W_ATT:3 * W_ATT] = acc[:, W_ATT:3 * W_ATT].astype(BF16)
    k32, v32 = acc[:, W_ATT:2 * W_ATT], acc[:, 2 * W_ATT:3 * W_ATT]
    if keys_t_refs:
        k32_t = k32.T
        for ref in keys_t_refs:
            ref[...] = k32_t.astype(ref.dtype)
    if values_t_ref is not None:
        values_t_ref[...] = v32.T

    acc = _dot_nt(hn, wret_ref[...])
    for j in range(W_RET // LANES):
        sl = slice(j * LANES, (j + 1) * LANES)
        ret_ref[:, sl] = rotary(acc[:, sl]).astype(BF16)
        sk = slice(W_RET + j * LANES, W_RET + (j + 1) * LANES)
        ret_ref[:, sk] = (rotary(acc[:, sk]) * scale).astype(BF16)
    ret_ref[:, 2 * W_RET:3 * W_RET] = acc[:, 2 * W_RET:3 * W_RET].astype(BF16)
    g = acc[:, 3 * W_RET:4 * W_RET]
    ret_ref[:, 3 * W_RET:4 * W_RET] = (g * jax.nn.sigmoid(g)).astype(BF16)

    acc = _dot_nt(hn, wgla_ref[...])
    gla_ref[:, 0:W_GLA] = (acc[:, 0:W_GLA] * scale).astype(BF16)
    gla_ref[:, W_GLA:3 * W_GLA] = acc[:, W_GLA:3 * W_GLA].astype(BF16)
    g = acc[:, 3 * W_GLA:4 * W_GLA]
    gla_ref[:, 3 * W_GLA:4 * W_GLA] = (g * jax.nn.sigmoid(g)).astype(BF16)
    low = acc[:, 4 * W_GLA:4 * W_GLA + LOW_PAD].astype(BF16)
    z = _dot(low, wa2_ref[...]) + ba_ref[...]
    log_sig = jnp.minimum(z, 0.0) - jnp.log(1.0 + jnp.exp(-jnp.abs(z)))
    lf_ref[...] = log_sig * (1.0 / GLA_TAU)
    return k32, v32


def _in_proj_kernel(*refs, n_long_tiles, tiles_per_seq, n_prev):
    refs = list(refs)
    take = lambda k: [refs.pop(0) for _ in range(k)]
    x_l, x_s, n1w_ref, win_ref, wa2in_ref, ba_ref, cos_l, sin_l, cos_s, sin_s = take(10)
    later_f32 = take(3)
    kprev_ref, vprev_ref = take(2) if n_prev else (None, None)
    outs_l = take(4)
    kc_l, vc_l, kt_l = take(3)
    outs_s = take(4)
    kc_s, vc_s = take(2)
    later_bf16 = take(3)
    weights = take(4)
    wret_ref, wgla_ref, watt_ref, wa2_ref = weights
    i = pl.program_id(0)

    @pl.when(i < n_long_tiles)
    def _():
        for src, dst in zip(later_f32, later_bf16):
            dst[...] = src[...].astype(BF16)

    @pl.when(i == 0)
    def _():
        n_main = 4 * W_RET + 4 * W_GLA
        wret_ref[...] = win_ref[0:4 * W_RET, :].astype(BF16)
        wgla_ref[0:4 * W_GLA, :] = win_ref[4 * W_RET:n_main, :].astype(BF16)
        wgla_ref[4 * W_GLA:4 * W_GLA + GLA_RANK, :] = (
            win_ref[n_main:n_main + GLA_RANK, :].astype(BF16))
        wgla_ref[4 * W_GLA + GLA_RANK:, :] = jnp.zeros((LOW_PAD - GLA_RANK, D_MODEL), BF16)
        watt_ref[...] = win_ref[n_main + GLA_RANK:IN_COLS, :].astype(BF16)
        wa2_ref[0:GLA_RANK, :] = wa2in_ref[...].astype(BF16)
        wa2_ref[GLA_RANK:, :] = jnp.zeros((LOW_PAD - GLA_RANK, W_GLA), BF16)

    @pl.when(i < n_long_tiles)
    def _():
        _project_rows(x_l, cos_l, sin_l, n1w_ref, ba_ref, weights, *outs_l,
                      keys_t_refs=(kt_l, kc_l.at[n_prev]), values_t_ref=vc_l.at[n_prev])
        if n_prev:
            @pl.when(i % tiles_per_seq == tiles_per_seq - 1)
            def _():
                kc_l[0:n_prev] = kprev_ref[...]
                vc_l[0:n_prev] = vprev_ref[...]

    @pl.when(i == n_long_tiles)
    def _():
        k32, v32 = _project_rows(x_s, cos_s, sin_s, n1w_ref, ba_ref, weights, *outs_s)
        kc_s[...] = k32
        vc_s[...] = v32


def _in_proj(x_long, x_short, n1w, w_in, w_a2, ba, tabs_long, tabs_short, prev_kv, later_w,
             *, layer, tm, seq_len):
    m = x_long.shape[0]
    ms = x_short.shape[0]
    n_tiles = m // tm
    w_out, w_up, w_down = later_w
    slice_axis = (1, 2, 1)
    later_in, later_out, later_shape = [], [], []
    for w, ax in zip(later_w, slice_axis):
        blk = list(w.shape[1:])
        blk[ax - 1] //= n_tiles
        index = lambda i, ax=ax: (tile(i),) * (ax == 1) + (0,) + (tile(i),) * (ax == 2)
        later_in.append(pl.BlockSpec((None,) + tuple(blk), lambda i, f=index: (layer,) + f(i)))
        later_out.append(pl.BlockSpec(tuple(blk), index))
        later_shape.append(jax.ShapeDtypeStruct(w.shape[1:], BF16))
    tab_blocks = tabs_long[0].shape[0] // tm
    tiles_per_seq = seq_len // tm
    at_layer = functools.partial(_layer_spec, layer=layer)
    tile = lambda i: jnp.minimum(i, n_tiles - 1)
    row = lambda w: pl.BlockSpec((tm, w), lambda i: (tile(i), 0))
    tab = pl.BlockSpec((tm, LANES), lambda i: (tile(i) % tab_blocks, 0))
    whole = lambda w: pl.BlockSpec((ms, w), lambda i: (0, 0))
    n_prev = 0 if prev_kv is None else prev_kv[0].shape[0]
    stacked = lambda n: pl.BlockSpec((n, None, W_ATT, tm),
                                     lambda i: (0, tile(i) // tiles_per_seq, 0, 0))
    prev_specs = [stacked(n_prev)] * 2 if n_prev else []
    groups = (4 * W_RET, 4 * W_GLA, W_GLA, 3 * W_ATT)
    dtypes = (BF16, BF16, F32, BF16)
    kern = functools.partial(_in_proj_kernel, n_long_tiles=n_tiles, tiles_per_seq=tiles_per_seq,
                             n_prev=n_prev)
    outs = pl.pallas_call(
        kern,
        grid=(n_tiles + 1,),
        in_specs=[row(D_MODEL), whole(D_MODEL), at_layer(n1w), at_layer(w_in), at_layer(w_a2),
                  at_layer(ba), tab, tab, whole(LANES), whole(LANES)] + later_in + prev_specs,
        out_specs=([row(w) for w in groups] + [stacked(n_prev + 1)] * 2
                   + [pl.BlockSpec((W_ATT, tm), lambda i: (0, tile(i)))]
                   + [whole(w) for w in groups] + [whole(W_ATT)] * 2 + later_out),
        out_shape=([jax.ShapeDtypeStruct((m, w), d) for w, d in zip(groups, dtypes)]
                   + [jax.ShapeDtypeStruct((n_prev + 1, m // seq_len, W_ATT, tm), F32)] * 2
                   + [jax.ShapeDtypeStruct((W_ATT, m), BF16)]
                   + [jax.ShapeDtypeStruct((ms, w), d) for w, d in zip(groups, dtypes)]
                   + [jax.ShapeDtypeStruct((ms, W_ATT), F32)] * 2 + later_shape),
        scratch_shapes=[pltpu.VMEM((4 * W_RET, D_MODEL), BF16),
                        pltpu.VMEM((4 * W_GLA + LOW_PAD, D_MODEL), BF16),
                        pltpu.VMEM((3 * W_ATT, D_MODEL), BF16),
                        pltpu.VMEM((LOW_PAD, W_GLA), BF16)],
        compiler_params=_params(1),
        name="in_proj",
    )(x_long, x_short, n1w, w_in, w_a2, ba, *tabs_long, *tabs_short, w_out, w_up, w_down,
      *(prev_kv if n_prev else ()))
    return outs[:7], outs[7:13], outs[13:]


def _init_states(s_scr, s0_ref, n_seq, n_heads):
    if s0_ref is None:
        s_scr[...] = jnp.zeros_like(s_scr)
        return
    zero = jnp.zeros((HEAD_DIM, HEAD_DIM), F32)
    for i in range(n_seq):
        s_scr[i] = jnp.concatenate(
            [jnp.concatenate([s0_ref[i, h] if j == h else zero for j in range(n_heads)], axis=1)
             for h in range(n_heads)], axis=0)


def _store_states(s_scr, out_ref, n_seq, n_heads):
    for i in range(n_seq):
        s = s_scr[i]
        for h in range(n_heads):
            sl = slice(h * HEAD_DIM, (h + 1) * HEAD_DIM)
            out_ref[i, h] = s[sl, sl]


def _chunk_rows(lc, n_chunks, n_seq):
    return [[slice((i * n_chunks + c) * lc, (i * n_chunks + c + 1) * lc)
             for c in range(n_chunks)] for i in range(n_seq)]


def _retention_body(q_ref, k_ref, v_ref, s0_ref, tab_refs, o_ref, sfin_ref, s_scr,
                    *, lc, n_chunks, n_seq):
    dtab_ref, qdec_ref, kdec_ref, gtab_ref, bd_ref = tab_refs
    g = pl.program_id(1)

    @pl.when(g == 0)
    def _():
        _init_states(s_scr, s0_ref, n_seq, H_RET)

    dtab = dtab_ref[...]
    flat = lambda ref: ref[...].reshape(n_seq * n_chunks * lc, ref.shape[-1])
    q = flat(q_ref)
    k = flat(k_ref)
    v = flat(v_ref)
    kd = (k.astype(F32) * kdec_ref[...]).astype(BF16)
    gtab = gtab_ref[...]
    bd = bd_ref[...]
    for i, rows in enumerate(_chunk_rows(lc, n_chunks, n_seq)):
        states = [s_scr[i]]
        for r in rows:
            upd = _dot_tn(kd[r, :], v[r, :])
            states.append(states[-1] * gtab + upd * bd)
        s_scr[i] = states[-1]
        for c, r in enumerate(rows):
            scores = _dot_nt(q[r, :], _block_diag_rows(k[r, :], H_RET))
            p = (scores * dtab).astype(BF16)
            intra = _dot(p, _block_diag_rows(v[r, :], H_RET))
            inter = _dot(q[r, :], states[c].astype(BF16))
            o_ref[i, c * lc:(c + 1) * lc, :] = intra + inter * qdec_ref[r, :]

    @pl.when(g == pl.num_programs(1) - 1)
    def _():
        _store_states(s_scr, sfin_ref, n_seq, H_RET)


def _retention_tables(lc, n_chunks):
    lg = np.log1p(-np.exp2(-5.0 - np.arange(H_RET, dtype=np.float64)))
    idx = np.arange(lc)
    diff = idx[:, None] - idx[None, :]
    causal = diff >= 0
    dpos = np.where(causal, diff, 0).astype(np.float64)
    decay = np.where(causal[None], np.exp(dpos[None] * lg[:, None, None]), 0.0)
    dtab = np.transpose(decay, (1, 0, 2)).reshape(lc, H_RET * lc)
    q_decay = np.exp((idx + 1).astype(np.float64)[:, None] * lg[None, :])
    k_decay = np.exp((lc - 1 - idx).astype(np.float64)[:, None] * lg[None, :])
    qdec = np.tile(np.repeat(q_decay, HEAD_DIM, axis=1), (n_chunks, 1))
    kdec = np.tile(np.repeat(k_decay, HEAD_DIM, axis=1), (n_chunks, 1))
    gtab = np.broadcast_to(np.repeat(np.exp(lc * lg), HEAD_DIM)[:, None], (W_RET, W_RET))
    head = np.arange(W_RET) // HEAD_DIM
    bd = head[:, None] == head[None, :]
    return tuple(jnp.asarray(t, F32) for t in (dtab, qdec, kdec, gtab, bd))


def _split_bf16(x):
    hi = x.astype(BF16)
    lo = (x - hi.astype(F32)).astype(BF16)
    return jnp.concatenate([hi, lo], axis=1)


def _group_reference(b, row, half):
    n, w = b.shape
    group = 2 * half
    if group >= 8:
        b3 = b.reshape(n // group, group, w)
        return jnp.broadcast_to(b3[:, half - 1:half, :], b3.shape).reshape(n, w)
    if group == 4:
        i = row & 3
        return jnp.where(i == 0, pltpu.roll(b, n - 1, 0),
                         jnp.where(i == 1, b,
                                   jnp.where(i == 2, pltpu.roll(b, 1, 0), pltpu.roll(b, 2, 0))))
    return jnp.where((row & 1) == 1, pltpu.roll(b, 1, 0), b)


def _gla_body(q_ref, k_ref, v_ref, lf_ref, s0_ref, tab_refs, o_ref, sfin_ref, s_scr, sc_scr,
              *, lc, n_chunks, n_seq):
    tri_ref, bd_ref, onehot_ref = tab_refs
    g = pl.program_id(1)

    @pl.when(g == 0)
    def _():
        _init_states(s_scr, s0_ref, n_seq, H_GLA)

    n_lev = lc.bit_length() - 1
    w = W_GLA
    n = lc * n_chunks * n_seq
    seq_rows = _chunk_rows(lc, n_chunks, n_seq)
    chunk_rows = [r for rows in seq_rows for r in rows]
    row = lax.broadcasted_iota(jnp.int32, (n, w), 0)
    trow = lax.broadcasted_iota(jnp.int32, (lc, H_GLA * lc), 0)
    scol = lax.broadcasted_iota(jnp.int32, (lc, H_GLA * lc), 1) & (lc - 1)
    flat = lambda ref: ref[...].reshape(n, ref.shape[-1])
    q = flat(q_ref)
    k = flat(k_ref)
    v = flat(v_ref)
    q32 = q.astype(F32)
    k32 = k.astype(F32)
    tri = tri_ref[...]
    lfs = _split_bf16(flat(lf_ref))
    cs = jnp.concatenate([_dot(tri, lfs[r, :]) for r in chunk_rows], axis=0)
    b = cs[:, :w] + cs[:, w:]

    qe = (q32 * jnp.exp(b)).astype(BF16)
    mild = jnp.min(b) >= -GLA_SAFE_LOG_DECAY

    @pl.when(mild)
    def _():
        ke = (k32 * jnp.exp(-b)).astype(BF16)
        for r in chunk_rows:
            sc = _dot_nt(qe[r, :], _block_diag_rows(ke[r, :], H_GLA))
            sc_scr[r, :] = jnp.where(trow >= scol, sc, 0.0)

    @pl.when(jnp.logical_not(mild))
    def _():
        for r in chunk_rows:
            sc = _dot_nt(q[r, :], _block_diag_rows(k[r, :], H_GLA))
            sc_scr[r, :] = jnp.where(trow == scol, sc, 0.0)
        for lev in range(n_lev):
            half = lc >> (lev + 1)
            shift = half.bit_length()
            upper = (row & half) != 0
            e = jnp.exp(-jnp.abs(b - _group_reference(b, row, half)))
            z = (jnp.where(upper, q32, k32) * e).astype(BF16)
            valid = (((trow >> shift) == (scol >> shift))
                     & ((trow & half) != 0) & ((scol & half) == 0))
            for r in chunk_rows:
                zc = z[r, :]
                sc = _dot_nt(zc, _block_diag_rows(zc, H_GLA))
                sc_scr[r, :] = jnp.where(valid, sc, sc_scr[r, :])

    b3 = b.reshape(n_chunks * n_seq, lc, w)
    b_last = b3[:, lc - 1:lc, :]
    kd = (k32 * jnp.exp(jnp.broadcast_to(b_last, b3.shape).reshape(n, w) - b)).astype(BF16)
    col_sum = _dot_tn(lfs, onehot_ref[...])
    decay_col = jnp.exp(col_sum[:w, :] + col_sum[w:, :])
    bd = bd_ref[...]
    for i, rows in enumerate(seq_rows):
        states = [s_scr[i]]
        for c, r in enumerate(rows):
            j = i * n_chunks + c
            upd = _dot_tn(kd[r, :], v[r, :])
            states.append(states[-1] * decay_col[:, j:j + 1] + upd * bd)
        s_scr[i] = states[-1]
        for c, r in enumerate(rows):
            inter = _dot(qe[r, :], states[c].astype(BF16))
            intra = _dot(sc_scr[r, :].astype(BF16), _block_diag_rows(v[r, :], H_GLA))
            o_ref[i, c * lc:(c + 1) * lc, :] = intra + inter

    @pl.when(g == pl.num_programs(1) - 1)
    def _():
        _store_states(s_scr, sfin_ref, n_seq, H_GLA)


def _linear_mixers_kernel(*refs, lc, n_chunks, n_seq, has_init):
    refs = list(refs)
    take = lambda k: [refs.pop(0) for _ in range(k)]
    rq, rk, rv, gq, gk, gv, lf = take(7)
    s0r, s0g = take(2) if has_init else (None, None)
    ret_tabs = take(5)
    gla_tabs = take(3)
    oa_ref, sr_ref, ob_ref, sg_ref, sr_scr, sg_scr, sc_scr = take(7)
    dims = dict(lc=lc, n_chunks=n_chunks, n_seq=n_seq)
    _retention_body(rq, rk, rv, s0r, ret_tabs, oa_ref, sr_ref, sr_scr, **dims)
    _gla_body(gq, gk, gv, lf, s0g, gla_tabs, ob_ref, sg_ref, sg_scr, sc_scr, **dims)


def _linear_mixers(ret, gla, lf, states, layer, *, batch, seq_len, lc, n_chunks, n_seq):
    rows_seq = lc * n_chunks
    rows = rows_seq * n_seq
    ng = seq_len // rows_seq
    m = batch * seq_len
    width = W_RET
    by_seq = lambda a: a.reshape(batch, seq_len, a.shape[-1])
    ret, gla, lf = by_seq(ret), by_seq(gla), by_seq(lf)
    col = lambda c: pl.BlockSpec((n_seq, rows_seq, width), lambda b, g: (b, g, c))
    full = lambda a: pl.BlockSpec(a.shape, lambda b, g: (0,) * a.ndim)
    blk = (n_seq, H_RET, HEAD_DIM, HEAD_DIM)
    s_out_spec = pl.BlockSpec(blk, lambda b, g: (b, 0, 0, 0))
    s_out_shape = jax.ShapeDtypeStruct((batch, H_RET, HEAD_DIM, HEAD_DIM), F32)
    s_in_specs = []
    if states is not None:
        s_in_specs = [pl.BlockSpec((None,) + blk, lambda b, g: (layer, b, 0, 0, 0))] * 2
    ret_tabs = _retention_tables(lc, n_chunks * n_seq)
    gla_tabs = _gla_tables(lc, n_chunks * n_seq)
    kern = functools.partial(_linear_mixers_kernel, lc=lc, n_chunks=n_chunks, n_seq=n_seq,
                             has_init=states is not None)
    out_rows = col(0)
    out_shape = jax.ShapeDtypeStruct((batch, seq_len, width), F32)
    oa, s_ret, ob, s_gla = pl.pallas_call(
        kern,
        grid=(batch // n_seq, ng),
        in_specs=([col(0), col(1), col(2), col(0), col(1), col(2), col(0)] + s_in_specs
                  + [full(t) for t in ret_tabs + gla_tabs]),
        out_specs=[out_rows, s_out_spec, out_rows, s_out_spec],
        out_shape=[out_shape, s_out_shape, out_shape, s_out_shape],
        scratch_shapes=[pltpu.VMEM((n_seq, width, width), F32),
                        pltpu.VMEM((n_seq, width, width), F32),
                        pltpu.VMEM((rows, H_GLA * lc), F32)],
        compiler_params=_params(2),
        name="linear_mixers",
    )(ret, ret, ret, gla, gla, gla, lf, *(states or ()), *ret_tabs, *gla_tabs)
    return oa.reshape(m, width), s_ret, ob.reshape(m, width), s_gla


def _gla_tables(lc, n_chunks):
    idx = np.arange(lc)
    tri = idx[:, None] >= idx[None, :]
    head = np.arange(W_GLA) // HEAD_DIM
    bd = head[:, None] == head[None, :]
    chunk_of_row = np.arange(lc * n_chunks) // lc
    onehot = chunk_of_row[:, None] == np.arange(LANES)[None, :]
    return jnp.asarray(tri, BF16), jnp.asarray(bd, F32), jnp.asarray(onehot, BF16)


def _band_attn_kernel(q_ref, kt_ref, vc_ref, gv_ref, hones_ref, o_ref,
                      kwin_t, vwin, bias_scr, s_scr, m_scr, ksq_prev, *, lc, rpi, n_iter):
    b_id = pl.program_id(0)
    g = pl.program_id(1)
    win = ATT_REACH + rpi
    rows = rpi * n_iter

    @pl.when((b_id == 0) & (g == 0))
    def _():
        kwin_t[:, 0:rows] = jnp.zeros((W_ATT, rows), BF16)
        vwin[0:rows, :] = jnp.zeros((rows, W_ATT), BF16)
        ksq_prev[0] = 0.0
        row = lax.broadcasted_iota(jnp.int32, (rpi, BIAS_W), 0)
        rel = lax.broadcasted_iota(jnp.int32, (rpi, BIAS_W), 1) - (row // lc) * lc
        in_band = (rel >= 0) & (rel < ATT_REACH + lc)
        for h in range(H_ATT):
            x = jnp.broadcast_to(gv_ref[h:h + 1, :], (rpi, BIAS_W))
            x = pltpu.roll(x, BIAS_W - (lc - 1), 1, stride=1, stride_axis=0)
            bias_scr[h // 2, (h % 2) * rpi:(h % 2 + 1) * rpi, :] = jnp.where(
                in_band, x * LOG2E, NEG_BIG)

    base = pl.multiple_of(g * rows, rows)
    kwin_t[:, pl.ds(base + rows, rows)] = kt_ref[...]
    vwin[pl.ds(base + rows, rows), :] = vc_ref[...]

    head_ones = hones_ref[...]
    q_cur = q_ref[...].astype(F32)
    q_sq = jnp.max(_dot((q_cur * q_cur).astype(BF16), head_ones))
    k_cur = kt_ref[...].astype(F32)
    k_sq_cur = jnp.max(jnp.sum((k_cur * k_cur).reshape(H_ATT, HEAD_DIM, rows), axis=1))
    k_sq = jnp.maximum(k_sq_cur, jnp.where(g == 0, 0.0, ksq_prev[0]))
    ksq_prev[0] = k_sq_cur
    room = ATT_SAFE_LOG2 - jnp.max(jnp.abs(gv_ref[...])) * LOG2E
    bounded = (room > 0.0) & (q_sq * k_sq * NORM_ROUNDING_MARGIN <= room * room)

    lane = lax.broadcasted_iota(jnp.int32, (rpi, LANES), 1)
    lane2 = lax.broadcasted_iota(jnp.int32, (2 * rpi, LANES), 1)
    head2 = lax.broadcasted_iota(jnp.int32, (2 * rpi, LANES), 0) // rpi
    col = lax.broadcasted_iota(jnp.int32, (2 * rpi, win), 1)

    def step(i, carry, *, masked, small):
        r0 = pl.multiple_of(i * rpi, rpi)
        first_valid = ATT_REACH - (g * rows + r0)

        def scores(pair):
            lanes = slice(pair * LANES, (pair + 1) * LANES)
            qp = q_ref[pl.ds(r0, rpi), lanes]
            q2 = jnp.concatenate([qp, qp], axis=0)
            q2 = jnp.where((lane2 // HEAD_DIM) == head2, q2, jnp.zeros_like(q2))
            s = _dot(q2, kwin_t[lanes, pl.ds(base + r0, win)]) + bias_scr[pair][:, :win]
            if masked:
                s = jnp.where(col >= first_valid, s, NEG_BIG)
            return s

        def finish(pair, e):
            lanes = slice(pair * LANES, (pair + 1) * LANES)
            den = jnp.sum(e, axis=-1, keepdims=True)
            pv = _dot(e.astype(BF16), vwin[pl.ds(base + r0, win), lanes]) / den
            o_ref[pl.ds(r0, rpi), lanes] = jnp.where(
                lane < HEAD_DIM, pv[:rpi, :], pv[rpi:, :]).astype(BF16)

        if small:
            for pair in range(H_ATT // 2):
                finish(pair, jnp.exp2(scores(pair)))
        else:
            for pair in range(H_ATT // 2):
                s = scores(pair)
                s_scr[pair, :, :win] = s
                m_scr[pair] = jnp.broadcast_to(
                    jnp.max(s, axis=-1, keepdims=True), (2 * rpi, LANES))
            for pair in range(H_ATT // 2):
                finish(pair, jnp.exp2(s_scr[pair, :, :win] - m_scr[pair][:, :1]))
        return carry

    def run(masked):
        @pl.when(bounded)
        def _():
            lax.fori_loop(0, n_iter, functools.partial(step, masked=masked, small=True), 0,
                          unroll=4)

        @pl.when(jnp.logical_not(bounded))
        def _():
            lax.fori_loop(0, n_iter, functools.partial(step, masked=masked, small=False), 0)

    @pl.when(g == 0)
    def _():
        run(True)

    @pl.when(g != 0)
    def _():
        run(False)


def _band_attn(att, keys_t, gv, layer, *, batch, seq_len, lc, rpi, n_iter):
    rows = rpi * n_iter
    assert rows == ATT_REACH, "the carried window is exactly one row group"
    ng = seq_len // rows
    m = batch * seq_len
    cur = lambda c: pl.BlockSpec((rows, W_ATT), lambda b, g: (b * ng + g, c))
    cur_t = pl.BlockSpec((W_ATT, rows), lambda b, g: (0, b * ng + g))
    kern = functools.partial(_band_attn_kernel, lc=lc, rpi=rpi, n_iter=n_iter)
    head_ones = _head_ones()
    return pl.pallas_call(
        kern,
        grid=(batch, ng),
        in_specs=[cur(0), cur_t, cur(2), _layer_spec(gv, layer),
                  pl.BlockSpec(head_ones.shape, lambda b, g: (0, 0))],
        out_specs=pl.BlockSpec((rows, W_ATT), lambda b, g: (b * ng + g, 0)),
        out_shape=jax.ShapeDtypeStruct((m, W_ATT), BF16),
        scratch_shapes=[pltpu.VMEM((W_ATT, (ng + 1) * rows), BF16),
                        pltpu.VMEM(((ng + 1) * rows, W_ATT), BF16),
                        pltpu.VMEM((H_ATT // 2, 2 * rpi, BIAS_W), F32),
                        pltpu.VMEM((H_ATT // 2, 2 * rpi, BIAS_W), F32),
                        pltpu.VMEM((H_ATT // 2, 2 * rpi, LANES), F32),
                        pltpu.SMEM((1,), F32)],
        compiler_params=_params(2),
        name="band_attn",
    )(att, keys_t, att, gv, head_ones)


def _sample_attn_kernel(q_ref, kn_ref, vn_ref, kt_ref, vt_ref, gv_ref, hones_ref, o_ref,
                        *, ls, n_seq):
    win = ATT_REACH + ls
    lane = lax.broadcasted_iota(jnp.int32, (ls, LANES), 1)
    lane2 = lax.broadcasted_iota(jnp.int32, (2 * ls, LANES), 1)
    head2 = lax.broadcasted_iota(jnp.int32, (2 * ls, LANES), 0) // ls

    head_ones = hones_ref[...]
    q32 = q_ref[...].astype(F32)
    kn32 = kn_ref[...].astype(F32)
    kc32 = kt_ref[...]
    q_sq = jnp.max(_dot((q32 * q32).astype(BF16), head_ones))
    k_sq = jnp.maximum(jnp.max(_dot((kn32 * kn32).astype(BF16), head_ones)),
                       jnp.max(jnp.sum(kc32 * kc32, axis=2)))
    room = ATT_SAFE_LOG2 - jnp.max(jnp.abs(gv_ref[...])) * LOG2E
    bounded = (room > 0.0) & (q_sq * k_sq * NORM_ROUNDING_MARGIN <= room * room)

    def head_pair(seq, pair, small):
        lanes = slice(pair * LANES, (pair + 1) * LANES)
        rows = slice(seq * ls, (seq + 1) * ls)
        qp = q_ref[rows, lanes]
        q2 = jnp.concatenate([qp, qp], axis=0)
        q2 = jnp.where((lane2 // HEAD_DIM) == head2, q2, jnp.zeros_like(q2))
        heads = slice(2 * pair, 2 * pair + 2)
        kt = kt_ref[seq, heads].reshape(2 * HEAD_DIM, ATT_REACH).astype(BF16)
        vt = vt_ref[seq, heads].reshape(2 * HEAD_DIM, ATT_REACH).astype(BF16)
        bias = jnp.concatenate(
            [pltpu.roll(jnp.broadcast_to(gv_ref[h:h + 1, :], (ls, BIAS_W)),
                        BIAS_W - (ls - 1), 1, stride=1, stride_axis=0)
             for h in (2 * pair, 2 * pair + 1)], axis=0) * LOG2E
        s_old = _dot(q2, kt) + bias[:, :ATT_REACH]
        s_new = _dot_nt(q2, kn_ref[rows, lanes]) + bias[:, ATT_REACH:win]
        if not small:
            mx = jnp.maximum(jnp.max(s_old, axis=-1, keepdims=True),
                             jnp.max(s_new, axis=-1, keepdims=True))
            s_old = s_old - mx
            s_new = s_new - mx
        e_old = jnp.exp2(s_old)
        e_new = jnp.exp2(s_new)
        den = jnp.sum(e_old, axis=-1, keepdims=True) + jnp.sum(e_new, axis=-1, keepdims=True)
        pv = (_dot_nt(e_old.astype(BF16), vt)
              + _dot(e_new.astype(BF16), vn_ref[rows, lanes])) / den
        o_ref[rows, lanes] = jnp.where(lane < HEAD_DIM, pv[:ls, :], pv[ls:, :]).astype(BF16)

    @pl.when(bounded)
    def _():
        for seq in range(n_seq):
            for pair in range(H_ATT // 2):
                head_pair(seq, pair, True)

    @pl.when(jnp.logical_not(bounded))
    def _():
        for seq in range(n_seq):
            for pair in range(H_ATT // 2):
                head_pair(seq, pair, False)


def _head_ones():
    lane_head = np.arange(W_ATT) // HEAD_DIM
    return jnp.asarray(lane_head[:, None] == np.arange(LANES)[None, :], BF16)


def _sample_attn(att, cache_kt, cache_vt, layer, gv, *, batch, ls):
    n_seq = SAMPLE_ATT_SEQS_PER_STEP
    new = lambda c: pl.BlockSpec((n_seq * ls, W_ATT), lambda b: (b, c))
    old = pl.BlockSpec((None, n_seq, H_ATT, HEAD_DIM, ATT_REACH),
                       lambda b: (layer, b, 0, 0, 0))
    head_ones = _head_ones()
    kern = functools.partial(_sample_attn_kernel, ls=ls, n_seq=n_seq)
    return pl.pallas_call(
        kern,
        grid=(batch // n_seq,),
        in_specs=[new(0), new(1), new(2), old, old, _layer_spec(gv, layer),
                  pl.BlockSpec(head_ones.shape, lambda b: (0, 0))],
        out_specs=pl.BlockSpec((n_seq * ls, W_ATT), lambda b: (b, 0)),
        out_shape=jax.ShapeDtypeStruct((batch * ls, W_ATT), BF16),
        compiler_params=_params(1),
        name="sample_attn",
    )(att, att, att, cache_kt, cache_vt, gv, head_ones)


def _bias_rows(rel_bias_l, lc):
    wv = np.arange(BIAS_W)
    dist = ATT_REACH + lc - 1 - wv
    ridx = np.clip(dist, -(CHUNK - 1), REL_MAX) + (CHUNK - 1)
    n_far = int(np.argmax(ridx < N_REL - 1)) - 1
    n_near = BIAS_W - n_far - N_REL
    assert np.array_equal(
        ridx, np.concatenate([np.full(n_far, N_REL - 1), np.arange(N_REL - 1, -1, -1),
                              np.zeros(n_near, np.int64)]))
    lead = rel_bias_l.shape[:-1]
    return jnp.concatenate(
        [jnp.broadcast_to(rel_bias_l[..., N_REL - 1:], lead + (n_far,)),
         jnp.flip(rel_bias_l, axis=-1),
         jnp.broadcast_to(rel_bias_l[..., :1], lead + (n_near,))], axis=-1)


def _mix_rows(x_ref, oa_ref, ob_ref, oc_ref, sga_ref, sgb_ref, params, y_ref, *, final_norm):
    gnw_ref, glw_ref, n2w_ref, fnw_ref, ones_ref, wout_ref, wup_ref, wdown_ref = params
    ones = ones_ref[...]

    def head_sum(t):
        return _dot(t.astype(BF16), ones)

    inv_d = 1.0 / HEAD_DIM
    oa = oa_ref[...]
    mu = head_sum(oa) * inv_d
    da = oa - mu
    var = head_sum(da * da) * inv_d
    a = da * lax.rsqrt(var + EPS) * gnw_ref[...] * sga_ref[...].astype(F32)
    ob = ob_ref[...]
    ms = head_sum(ob * ob) * inv_d
    bb = ob * lax.rsqrt(ms + EPS) * glw_ref[...] * sgb_ref[...].astype(F32)
    cat = jnp.concatenate([a.astype(BF16), bb.astype(BF16), oc_ref[...]], axis=1)
    h1 = x_ref[...] + _dot(cat, wout_ref[...])
    hn = _rmsnorm(h1, n2w_ref[...]).astype(BF16)
    acc = jnp.zeros_like(h1)
    n_slab = D_FF // D_MODEL
    for j in range(n_slab):
        sl = slice(j * D_MODEL, (j + 1) * D_MODEL)
        u = jnp.maximum(_dot(hn, wup_ref[:, sl]), 0.0)
        acc = acc + _dot((u * u).astype(BF16), wdown_ref[sl, :])
    h2 = h1 + acc
    if final_norm:
        h2 = _rmsnorm(h2, fnw_ref[...])
    y_ref[...] = h2


def _out_ffn_kernel(*refs, n_long_tiles, final_norm):
    rows_l, rows_s, params = refs[0:6], refs[6:12], refs[12:20]
    y_l, y_s = refs[20:22]
    i = pl.program_id(0)

    @pl.when(i < n_long_tiles)
    def _():
        _mix_rows(*rows_l, params, y_l, final_norm=final_norm)

    @pl.when(i == n_long_tiles)
    def _():
        _mix_rows(*rows_s, params, y_s, final_norm=final_norm)


def _out_ffn(long_in, short_in, gnw, glw, n2w, fnw, ones, wout, wup, wdown,
             *, layer, tm, final_norm):
    m = long_in[0].shape[0]
    ms = short_in[0].shape[0]
    n_tiles = m // tm
    full = lambda a: pl.BlockSpec(a.shape, lambda i: (0,) * a.ndim)
    at_layer = functools.partial(_layer_spec, layer=layer)
    tile = lambda i: jnp.minimum(i, n_tiles - 1)
    row = lambda w, c=0: pl.BlockSpec((tm, w), lambda i: (tile(i), c))
    whole = lambda w, c=0: pl.BlockSpec((ms, w), lambda i: (0, c))
    row_specs = lambda spec: [spec(D_MODEL), spec(W_RET), spec(W_GLA), spec(W_ATT),
                              spec(W_RET, 3), spec(W_GLA, 3)]
    kern = functools.partial(_out_ffn_kernel, n_long_tiles=n_tiles, final_norm=final_norm)
    return pl.pallas_call(
        kern,
        grid=(n_tiles + 1,),
        in_specs=(row_specs(row) + row_specs(whole)
                  + [at_layer(gnw), at_layer(glw), at_layer(n2w), full(fnw), full(ones),
                     full(wout), full(wup), full(wdown)]),
        out_specs=[row(D_MODEL), whole(D_MODEL)],
        out_shape=[jax.ShapeDtypeStruct((m, D_MODEL), F32),
                   jax.ShapeDtypeStruct((ms, D_MODEL), F32)],
        compiler_params=_params(1),
        name="out_ffn",
    )(*long_in, *short_in, gnw, glw, n2w, fnw, ones, wout, wup, wdown)


def _rotary_tables(pos, rows):
    half = HEAD_DIM // 2
    inv_freq = ROPE_BASE ** (-np.arange(half, dtype=np.float64) / half)
    ang = np.asarray(pos, np.float64)[:, None] * inv_freq[None, :]
    reps = (rows // len(pos), LANES // half)
    return (jnp.asarray(np.tile(np.cos(ang), reps), F32),
            jnp.asarray(np.tile(np.sin(ang), reps), F32))


def _mixers(proj, rel_bias, *, batch, seq_len, lc, n_chunks, layer, states, caches, keys_t=None):
    ret, gla, lf, att = proj
    n_seq = batch if lc * n_chunks == seq_len else LINMIX_SEQS_PER_STEP
    oa, s_ret, ob, s_gla = _linear_mixers(
        ret, gla, lf, states, layer, batch=batch, seq_len=seq_len, lc=lc, n_chunks=n_chunks,
        n_seq=n_seq)
    gv = _bias_rows(rel_bias, lc)
    if caches is None:
        rpi = min(lc * n_chunks, ATT_ROWS_PER_STEP)
        oc = _band_attn(att, keys_t, gv, layer, batch=batch, seq_len=seq_len, lc=lc, rpi=rpi,
                        n_iter=lc * n_chunks // rpi)
    else:
        oc = _sample_attn(att, caches[0], caches[1], layer, gv, batch=batch, ls=seq_len)
    return (oa, ob, oc, ret, gla), s_ret, s_gla


def kernel(x_prompt, x_sample, state_ret, state_gla, cache_attn_k, cache_attn_v, norm1_w, norm2_w, final_norm_w, w_in, w_gla_a2, b_gla_a, ret_gn_w, gla_norm_w, rel_bias, w_out, w_up, w_down):
    depth = w_in.shape[0]
    bp, lp, _ = x_prompt.shape
    bs, ls, _ = x_sample.shape
    keep = min(ATT_REACH, lp)
    head = np.arange(W_RET) // HEAD_DIM
    ones = jnp.asarray(head[:, None] == head[None, :], BF16)
    hp = x_prompt.reshape(bp * lp, D_MODEL)
    hs = x_sample.reshape(bs * ls, D_MODEL)
    pos_p = np.arange(lp)
    pos_s = PAST_LEN + np.arange(ls)
    tm_p = keep
    to_head_major = lambda c: jnp.transpose(c, (0, 1, 3, 4, 2))
    caches = (to_head_major(cache_attn_k.astype(F32)), to_head_major(cache_attn_v.astype(F32)))
    from_head_major = lambda t: jnp.transpose(
        t.reshape(depth, bp, H_ATT, HEAD_DIM, keep), (0, 1, 4, 2, 3))
    states = (state_ret.astype(F32), state_gla.astype(F32))
    outs = {k: [] for k in ("ret_p", "gla_p", "ret_s", "gla_s", "ks", "vs")}
    prompt_kv = None
    rows3 = lambda v: v.astype(F32).reshape(depth, 1, v.shape[-1])
    n1w, n2w, fnw = rows3(norm1_w), rows3(norm2_w), final_norm_w.astype(F32)[None, :]
    gnw, glw, ba = rows3(ret_gn_w), rows3(gla_norm_w), rows3(b_gla_a)
    w_in_t = jnp.swapaxes(w_in.astype(F32), 1, 2)
    w_a2 = w_gla_a2.astype(F32)
    bias_tab = rel_bias.astype(F32)
    later_w = (w_out.astype(F32), w_up.astype(F32), w_down.astype(F32))
    tabs_p = _rotary_tables(pos_p, lp)
    tabs_s = _rotary_tables(pos_s, bs * ls)
    for l in range(depth):
        last = l == depth - 1
        proj_p, proj_s, (wout, wup, wdown) = _in_proj(
            hp, hs, n1w, w_in_t, w_a2, ba, tabs_p, tabs_s, prompt_kv, later_w,
            layer=l, tm=tm_p, seq_len=lp)
        prompt_kv = proj_p[4:6]
        mix_p, s_ret, s_gla = _mixers(
            proj_p[:4], bias_tab, batch=bp, seq_len=lp, lc=CHUNK, n_chunks=ATT_REACH // CHUNK,
            layer=l, states=None, caches=None, keys_t=proj_p[6])
        outs["ret_p"].append(s_ret)
        outs["gla_p"].append(s_gla)
        mix_s, s_ret, s_gla = _mixers(
            proj_s[:4], bias_tab, batch=bs, seq_len=ls, lc=ls, n_chunks=1,
            layer=l, states=states, caches=caches)
        outs["ret_s"].append(s_ret)
        outs["gla_s"].append(s_gla)
        outs["ks"].append(proj_s[4].reshape(bs, ls, H_ATT, HEAD_DIM))
        outs["vs"].append(proj_s[5].reshape(bs, ls, H_ATT, HEAD_DIM))
        hp, hs = _out_ffn((hp,) + mix_p, (hs,) + mix_s, gnw, glw, n2w, fnw, ones,
                          wout, wup, wdown, layer=l, tm=tm_p, final_norm=last)
    y_prompt = hp.reshape(bp, lp, D_MODEL)
    y_sample = hs.reshape(bs, ls, D_MODEL)
    st = lambda k: jnp.stack(outs[k])
    return (y_prompt, y_sample, st("ret_p"), st("gla_p"),
            from_head_major(prompt_kv[0]), from_head_major(prompt_kv[1]),
            st("ret_s"), st("gla_s"), st("ks"), st("vs"))
```

```python
import functools

import numpy as np
import jax
import jax.numpy as jnp
from jax import lax
from jax.experimental import pallas as pl
from jax.experimental.pallas import tpu as pltpu

D_MODEL = 1024
CHUNK = 64
HEAD_DIM = 64
H_RET = 4
H_GLA = 4
H_ATT = 8
W_RET = H_RET * HEAD_DIM
W_GLA = H_GLA * HEAD_DIM
W_ATT = H_ATT * HEAD_DIM
GLA_RANK = 16
GLA_TAU = 16.0
ATT_REACH = 512
REL_MAX = 256
N_REL = CHUNK + REL_MAX
D_FF = 4 * D_MODEL
IN_COLS = 4 * W_RET + 4 * W_GLA + GLA_RANK + 3 * W_ATT
ROPE_BASE = 10000.0
EPS = 1e-6
PAST_LEN = 4096

LANES = 128
LOW_PAD = LANES
BIAS_W = 640
GLA_SAFE_LOG_DECAY = 50.0
ATT_SAFE_LOG2 = 80.0
NORM_ROUNDING_MARGIN = 1.05
LINMIX_SEQS_PER_STEP = 4
SAMPLE_ATT_SEQS_PER_STEP = 4
ATT_ROWS_PER_STEP = 128
LOG2E = 1.4426950408889634
NEG_BIG = -1e30
VMEM_LIMIT = 58 * 1024 * 1024

F32 = jnp.float32
BF16 = jnp.bfloat16

_NT = (((1,), (1,)), ((), ()))
_TN = (((0,), (0,)), ((), ()))


def _dot(a, b):
    return jnp.dot(a, b, preferred_element_type=F32)


def _dot_nt(a, b):
    return lax.dot_general(a, b, _NT, preferred_element_type=F32)


def _dot_tn(a, b):
    return lax.dot_general(a, b, _TN, preferred_element_type=F32)


def _rmsnorm(x, w):
    return x * lax.rsqrt(jnp.mean(x * x, axis=-1, keepdims=True) + EPS) * w


def _params(n_grid_dims):
    return pltpu.CompilerParams(
        dimension_semantics=("arbitrary",) * n_grid_dims,
        vmem_limit_bytes=VMEM_LIMIT,
    )


def _layer_spec(a, layer):
    index = (layer,) + (0,) * (a.ndim - 1)
    return pl.BlockSpec((None,) + a.shape[1:], lambda *_: index, pipeline_mode=pl.Buffered(1))


def _block_diag_rows(x, n_heads):
    lane_head = lax.broadcasted_iota(jnp.int32, x.shape, 1) // HEAD_DIM
    zero = jnp.zeros_like(x)
    return jnp.concatenate(
        [jnp.where(lane_head == h, x, zero) for h in range(n_heads)], axis=0)


def _project_rows(x_ref, cos_ref, sin_ref, n1w_ref, ba_ref, weights,
                  ret_ref, gla_ref, lf_ref, att_ref, keys_t_refs=(), values_t_ref=None):
    wret_ref, wgla_ref, watt_ref, wa2_ref = weights
    hn = _rmsnorm(x_ref[...], n1w_ref[...]).astype(BF16)
    cos = cos_ref[...]
    sin = sin_ref[...]
    first_half = (lax.broadcasted_iota(jnp.int32, cos.shape, 1) % HEAD_DIM) < (HEAD_DIM // 2)

    def rotary(t):
        swapped = jnp.where(first_half, -pltpu.roll(t, LANES - HEAD_DIM // 2, 1),
                            pltpu.roll(t, HEAD_DIM // 2, 1))
        return t * cos + swapped * sin

    scale = HEAD_DIM ** -0.5

    acc = _dot_nt(hn, watt_ref[...])
    att_ref[:, 0:W_ATT] = (acc[:, 0:W_ATT] * (scale * LOG2E)).astype(BF16)
    att_ref[:, W_ATT:3 * W_ATT] = acc[:, W_ATT:3 * W_ATT].astype(BF16)
    k32, v32 = acc[:, W_ATT:2 * W_ATT], acc[:, 2 * W_ATT:3 * W_ATT]
    if keys_t_refs:
        k32_t = k32.T
        for ref in keys_t_refs:
            ref[...] = k32_t.astype(ref.dtype)
    if values_t_ref is not None:
        values_t_ref[...] = v32.T

    acc = _dot_nt(hn, wgla_ref[...])
    gla_ref[:, 0:W_GLA] = (acc[:, 0:W_GLA] * scale).astype(BF16)
    gla_ref[:, W_GLA:3 * W_GLA] = acc[:, W_GLA:3 * W_GLA].astype(BF16)
    g = acc[:, 3 * W_GLA:4 * W_GLA]
    gla_ref[:, 3 * W_GLA:4 * W_GLA] = (g * jax.nn.sigmoid(g)).astype(BF16)
    low = acc[:, 4 * W_GLA:4 * W_GLA + LOW_PAD].astype(BF16)
    z = _dot(low, wa2_ref[...]) + ba_ref[...]
    log_sig = jnp.minimum(z, 0.0) - jnp.log(1.0 + jnp.exp(-jnp.abs(z)))
    lf_ref[...] = log_sig * (1.0 / GLA_TAU)

    acc = _dot_nt(hn, wret_ref[...])
    for j in range(W_RET // LANES):
        sl = slice(j * LANES, (j + 1) * LANES)
        ret_ref[:, sl] = rotary(acc[:, sl]).astype(BF16)
        sk = slice(W_RET + j * LANES, W_RET + (j + 1) * LANES)
        ret_ref[:, sk] = (rotary(acc[:, sk]) * scale).astype(BF16)
    ret_ref[:, 2 * W_RET:3 * W_RET] = acc[:, 2 * W_RET:3 * W_RET].astype(BF16)
    g = acc[:, 3 * W_RET:4 * W_RET]
    ret_ref[:, 3 * W_RET:4 * W_RET] = (g * jax.nn.sigmoid(g)).astype(BF16)
    return k32, v32


def _in_proj_kernel(*refs, n_long_tiles, tiles_per_seq, n_prev):
    refs = list(refs)
    take = lambda k: [refs.pop(0) for _ in range(k)]
    x_l, x_s, n1w_ref, win_ref, wa2in_ref, ba_ref, cos_l, sin_l, cos_s, sin_s = take(10)
    later_f32 = take(3)
    kprev_ref, vprev_ref = take(2) if n_prev else (None, None)
    outs_l = take(4)
    kc_l, vc_l, kt_l = take(3)
    outs_s = take(4)
    kc_s, vc_s = take(2)
    later_bf16 = take(3)
    weights = take(4)
    wret_ref, wgla_ref, watt_ref, wa2_ref = weights
    i = pl.program_id(0)

    @pl.when(i < n_long_tiles)
    def _():
        for src, dst in zip(later_f32, later_bf16):
            dst[...] = src[...].astype(BF16)

    @pl.when(i == 0)
    def _():
        n_main = 4 * W_RET + 4 * W_GLA
        wret_ref[...] = win_ref[0:4 * W_RET, :].astype(BF16)
        wgla_ref[0:4 * W_GLA, :] = win_ref[4 * W_RET:n_main, :].astype(BF16)
        wgla_ref[4 * W_GLA:4 * W_GLA + GLA_RANK, :] = (
            win_ref[n_main:n_main + GLA_RANK, :].astype(BF16))
        wgla_ref[4 * W_GLA + GLA_RANK:, :] = jnp.zeros((LOW_PAD - GLA_RANK, D_MODEL), BF16)
        watt_ref[...] = win_ref[n_main + GLA_RANK:IN_COLS, :].astype(BF16)
        wa2_ref[0:GLA_RANK, :] = wa2in_ref[...].astype(BF16)
        wa2_ref[GLA_RANK:, :] = jnp.zeros((LOW_PAD - GLA_RANK, W_GLA), BF16)

    @pl.when(i < n_long_tiles)
    def _():
        _project_rows(x_l, cos_l, sin_l, n1w_ref, ba_ref, weights, *outs_l,
                      keys_t_refs=(kt_l, kc_l.at[n_prev]), values_t_ref=vc_l.at[n_prev])
        if n_prev:
            @pl.when(i % tiles_per_seq == tiles_per_seq - 1)
            def _():
                kc_l[0:n_prev] = kprev_ref[...]
                vc_l[0:n_prev] = vprev_ref[...]

    @pl.when(i == n_long_tiles)
    def _():
        k32, v32 = _project_rows(x_s, cos_s, sin_s, n1w_ref, ba_ref, weights, *outs_s)
        kc_s[...] = k32
        vc_s[...] = v32


def _in_proj(x_long, x_short, n1w, w_in, w_a2, ba, tabs_long, tabs_short, prev_kv, later_w,
             *, layer, tm, seq_len):
    m = x_long.shape[0]
    ms = x_short.shape[0]
    n_tiles = m // tm
    w_out, w_up, w_down = later_w
    slice_axis = (1, 2, 1)
    later_in, later_out, later_shape = [], [], []
    for w, ax in zip(later_w, slice_axis):
        blk = list(w.shape[1:])
        blk[ax - 1] //= n_tiles
        index = lambda i, ax=ax: (tile(i),) * (ax == 1) + (0,) + (tile(i),) * (ax == 2)
        later_in.append(pl.BlockSpec((None,) + tuple(blk), lambda i, f=index: (layer,) + f(i)))
        later_out.append(pl.BlockSpec(tuple(blk), index))
        later_shape.append(jax.ShapeDtypeStruct(w.shape[1:], BF16))
    tab_blocks = tabs_long[0].shape[0] // tm
    tiles_per_seq = seq_len // tm
    at_layer = functools.partial(_layer_spec, layer=layer)
    tile = lambda i: jnp.minimum(i, n_tiles - 1)
    row = lambda w: pl.BlockSpec((tm, w), lambda i: (tile(i), 0))
    tab = pl.BlockSpec((tm, LANES), lambda i: (tile(i) % tab_blocks, 0))
    whole = lambda w: pl.BlockSpec((ms, w), lambda i: (0, 0))
    n_prev = 0 if prev_kv is None else prev_kv[0].shape[0]
    stacked = lambda n: pl.BlockSpec((n, None, W_ATT, tm),
                                     lambda i: (0, tile(i) // tiles_per_seq, 0, 0))
    prev_specs = [stacked(n_prev)] * 2 if n_prev else []
    groups = (4 * W_RET, 4 * W_GLA, W_GLA, 3 * W_ATT)
    dtypes = (BF16, BF16, F32, BF16)
    kern = functools.partial(_in_proj_kernel, n_long_tiles=n_tiles, tiles_per_seq=tiles_per_seq,
                             n_prev=n_prev)
    outs = pl.pallas_call(
        kern,
        grid=(n_tiles + 1,),
        in_specs=[row(D_MODEL), whole(D_MODEL), at_layer(n1w), at_layer(w_in), at_layer(w_a2),
                  at_layer(ba), tab, tab, whole(LANES), whole(LANES)] + later_in + prev_specs,
        out_specs=([row(w) for w in groups] + [stacked(n_prev + 1)] * 2
                   + [pl.BlockSpec((W_ATT, tm), lambda i: (0, tile(i)))]
                   + [whole(w) for w in groups] + [whole(W_ATT)] * 2 + later_out),
        out_shape=([jax.ShapeDtypeStruct((m, w), d) for w, d in zip(groups, dtypes)]
                   + [jax.ShapeDtypeStruct((n_prev + 1, m // seq_len, W_ATT, tm), F32)] * 2
                   + [jax.ShapeDtypeStruct((W_ATT, m), BF16)]
                   + [jax.ShapeDtypeStruct((ms, w), d) for w, d in zip(groups, dtypes)]
                   + [jax.ShapeDtypeStruct((ms, W_ATT), F32)] * 2 + later_shape),
        scratch_shapes=[pltpu.VMEM((4 * W_RET, D_MODEL), BF16),
                        pltpu.VMEM((4 * W_GLA + LOW_PAD, D_MODEL), BF16),
                        pltpu.VMEM((3 * W_ATT, D_MODEL), BF16),
                        pltpu.VMEM((LOW_PAD, W_GLA), BF16)],
        compiler_params=_params(1),
        name="in_proj",
    )(x_long, x_short, n1w, w_in, w_a2, ba, *tabs_long, *tabs_short, w_out, w_up, w_down,
      *(prev_kv if n_prev else ()))
    return outs[:7], outs[7:13], outs[13:]


def _init_states(s_scr, s0_ref, n_seq, n_heads):
    if s0_ref is None:
        s_scr[...] = jnp.zeros_like(s_scr)
        return
    zero = jnp.zeros((HEAD_DIM, HEAD_DIM), F32)
    for i in range(n_seq):
        s_scr[i] = jnp.concatenate(
            [jnp.concatenate([s0_ref[i, h] if j == h else zero for j in range(n_heads)], axis=1)
             for h in range(n_heads)], axis=0)


def _store_states(s_scr, out_ref, n_seq, n_heads):
    for i in range(n_seq):
        s = s_scr[i]
        for h in range(n_heads):
            sl = slice(h * HEAD_DIM, (h + 1) * HEAD_DIM)
            out_ref[i, h] = s[sl, sl]


def _chunk_rows(lc, n_chunks, n_seq):
    return [[slice((i * n_chunks + c) * lc, (i * n_chunks + c + 1) * lc)
             for c in range(n_chunks)] for i in range(n_seq)]


def _retention_body(q_ref, k_ref, v_ref, s0_ref, tab_refs, o_ref, sfin_ref, s_scr,
                    *, lc, n_chunks, n_seq):
    dtab_ref, qdec_ref, kdec_ref, gtab_ref, bd_ref = tab_refs
    g = pl.program_id(1)

    @pl.when(g == 0)
    def _():
        _init_states(s_scr, s0_ref, n_seq, H_RET)

    dtab = dtab_ref[...]
    flat = lambda ref: ref[...].reshape(n_seq * n_chunks * lc, ref.shape[-1])
    q = flat(q_ref)
    k = flat(k_ref)
    v = flat(v_ref)
    kd = (k.astype(F32) * kdec_ref[...]).astype(BF16)
    gtab = gtab_ref[...]
    bd = bd_ref[...]
    for i, rows in enumerate(_chunk_rows(lc, n_chunks, n_seq)):
        states = [s_scr[i]]
        for r in rows:
            upd = _dot_tn(kd[r, :], v[r, :])
            states.append(states[-1] * gtab + upd * bd)
        s_scr[i] = states[-1]
        for c, r in enumerate(rows):
            scores = _dot_nt(q[r, :], _block_diag_rows(k[r, :], H_RET))
            p = (scores * dtab).astype(BF16)
            intra = _dot(p, _block_diag_rows(v[r, :], H_RET))
            inter = _dot(q[r, :], states[c].astype(BF16))
            o_ref[i, c * lc:(c + 1) * lc, :] = intra + inter * qdec_ref[r, :]

    @pl.when(g == pl.num_programs(1) - 1)
    def _():
        _store_states(s_scr, sfin_ref, n_seq, H_RET)


def _retention_tables(lc, n_chunks):
    lg = np.log1p(-np.exp2(-5.0 - np.arange(H_RET, dtype=np.float64)))
    idx = np.arange(lc)
    diff = idx[:, None] - idx[None, :]
    causal = diff >= 0
    dpos = np.where(causal, diff, 0).astype(np.float64)
    decay = np.where(causal[None], np.exp(dpos[None] * lg[:, None, None]), 0.0)
    dtab = np.transpose(decay, (1, 0, 2)).reshape(lc, H_RET * lc)
    q_decay = np.exp((idx + 1).astype(np.float64)[:, None] * lg[None, :])
    k_decay = np.exp((lc - 1 - idx).astype(np.float64)[:, None] * lg[None, :])
    qdec = np.tile(np.repeat(q_decay, HEAD_DIM, axis=1), (n_chunks, 1))
    kdec = np.tile(np.repeat(k_decay, HEAD_DIM, axis=1), (n_chunks, 1))
    gtab = np.broadcast_to(np.repeat(np.exp(lc * lg), HEAD_DIM)[:, None], (W_RET, W_RET))
    head = np.arange(W_RET) // HEAD_DIM
    bd = head[:, None] == head[None, :]
    return tuple(jnp.asarray(t, F32) for t in (dtab, qdec, kdec, gtab, bd))


def _split_bf16(x):
    hi = x.astype(BF16)
    lo = (x - hi.astype(F32)).astype(BF16)
    return jnp.concatenate([hi, lo], axis=1)


def _group_reference(b, row, half):
    n, w = b.shape
    group = 2 * half
    if group >= 8:
        b3 = b.reshape(n // group, group, w)
        return jnp.broadcast_to(b3[:, half - 1:half, :], b3.shape).reshape(n, w)
    if group == 4:
        i = row & 3
        return jnp.where(i == 0, pltpu.roll(b, n - 1, 0),
                         jnp.where(i == 1, b,
                                   jnp.where(i == 2, pltpu.roll(b, 1, 0), pltpu.roll(b, 2, 0))))
    return jnp.where((row & 1) == 1, pltpu.roll(b, 1, 0), b)


def _gla_body(q_ref, k_ref, v_ref, lf_ref, s0_ref, tab_refs, o_ref, sfin_ref, s_scr, sc_scr,
              *, lc, n_chunks, n_seq):
    tri_ref, bd_ref, onehot_ref = tab_refs
    g = pl.program_id(1)

    @pl.when(g == 0)
    def _():
        _init_states(s_scr, s0_ref, n_seq, H_GLA)

    n_lev = lc.bit_length() - 1
    w = W_GLA
    n = lc * n_chunks * n_seq
    seq_rows = _chunk_rows(lc, n_chunks, n_seq)
    chunk_rows = [r for rows in seq_rows for r in rows]
    row = lax.broadcasted_iota(jnp.int32, (n, w), 0)
    trow = lax.broadcasted_iota(jnp.int32, (lc, H_GLA * lc), 0)
    scol = lax.broadcasted_iota(jnp.int32, (lc, H_GLA * lc), 1) & (lc - 1)
    flat = lambda ref: ref[...].reshape(n, ref.shape[-1])
    q = flat(q_ref)
    k = flat(k_ref)
    v = flat(v_ref)
    q32 = q.astype(F32)
    k32 = k.astype(F32)
    tri = tri_ref[...]
    lfs = _split_bf16(flat(lf_ref))
    cs = jnp.concatenate([_dot(tri, lfs[r, :]) for r in chunk_rows], axis=0)
    b = cs[:, :w] + cs[:, w:]

    qe = (q32 * jnp.exp(b)).astype(BF16)
    mild = jnp.min(b) >= -GLA_SAFE_LOG_DECAY

    @pl.when(mild)
    def _():
        ke = (k32 * jnp.exp(-b)).astype(BF16)
        for r in chunk_rows:
            sc = _dot_nt(qe[r, :], _block_diag_rows(ke[r, :], H_GLA))
            sc_scr[r, :] = jnp.where(trow >= scol, sc, 0.0)

    @pl.when(jnp.logical_not(mild))
    def _():
        for r in chunk_rows:
            sc = _dot_nt(q[r, :], _block_diag_rows(k[r, :], H_GLA))
            sc_scr[r, :] = jnp.where(trow == scol, sc, 0.0)
        for lev in range(n_lev):
            half = lc >> (lev + 1)
            shift = half.bit_length()
            upper = (row & half) != 0
            e = jnp.exp(-jnp.abs(b - _group_reference(b, row, half)))
            z = (jnp.where(upper, q32, k32) * e).astype(BF16)
            valid = (((trow >> shift) == (scol >> shift))
                     & ((trow & half) != 0) & ((scol & half) == 0))
            for r in chunk_rows:
                zc = z[r, :]
                sc = _dot_nt(zc, _block_diag_rows(zc, H_GLA))
                sc_scr[r, :] = jnp.where(valid, sc, sc_scr[r, :])

    b3 = b.reshape(n_chunks * n_seq, lc, w)
    b_last = b3[:, lc - 1:lc, :]
    kd = (k32 * jnp.exp(jnp.broadcast_to(b_last, b3.shape).reshape(n, w) - b)).astype(BF16)
    col_sum = _dot_tn(lfs, onehot_ref[...])
    decay_col = jnp.exp(col_sum[:w, :] + col_sum[w:, :])
    bd = bd_ref[...]
    for i, rows in enumerate(seq_rows):
        states = [s_scr[i]]
        for c, r in enumerate(rows):
            j = i * n_chunks + c
            upd = _dot_tn(kd[r, :], v[r, :])
            states.append(states[-1] * decay_col[:, j:j + 1] + upd * bd)
        s_scr[i] = states[-1]
        for c, r in enumerate(rows):
            inter = _dot(qe[r, :], states[c].astype(BF16))
            intra = _dot(sc_scr[r, :].astype(BF16), _block_diag_rows(v[r, :], H_GLA))
            o_ref[i, c * lc:(c + 1) * lc, :] = intra + inter

    @pl.when(g == pl.num_programs(1) - 1)
    def _():
        _store_states(s_scr, sfin_ref, n_seq, H_GLA)


def _linear_mixers_kernel(*refs, lc, n_chunks, n_seq, has_init):
    refs = list(refs)
    take = lambda k: [refs.pop(0) for _ in range(k)]
    rq, rk, rv, gq, gk, gv, lf = take(7)
    s0r, s0g = take(2) if has_init else (None, None)
    ret_tabs = take(5)
    gla_tabs = take(3)
    oa_ref, sr_ref, ob_ref, sg_ref, sr_scr, sg_scr, sc_scr = take(7)
    dims = dict(lc=lc, n_chunks=n_chunks, n_seq=n_seq)
    _retention_body(rq, rk, rv, s0r, ret_tabs, oa_ref, sr_ref, sr_scr, **dims)
    _gla_body(gq, gk, gv, lf, s0g, gla_tabs, ob_ref, sg_ref, sg_scr, sc_scr, **dims)


def _linear_mixers(ret, gla, lf, states, layer, *, batch, seq_len, lc, n_chunks, n_seq):
    rows_seq = lc * n_chunks
    rows = rows_seq * n_seq
    ng = seq_len // rows_seq
    m = batch * seq_len
    width = W_RET
    by_seq = lambda a: a.reshape(batch, seq_len, a.shape[-1])
    ret, gla, lf = by_seq(ret), by_seq(gla), by_seq(lf)
    col = lambda c: pl.BlockSpec((n_seq, rows_seq, width), lambda b, g: (b, g, c))
    full = lambda a: pl.BlockSpec(a.shape, lambda b, g: (0,) * a.ndim)
    blk = (n_seq, H_RET, HEAD_DIM, HEAD_DIM)
    s_out_spec = pl.BlockSpec(blk, lambda b, g: (b, 0, 0, 0))
    s_out_shape = jax.ShapeDtypeStruct((batch, H_RET, HEAD_DIM, HEAD_DIM), F32)
    s_in_specs = []
    if states is not None:
        s_in_specs = [pl.BlockSpec((None,) + blk, lambda b, g: (layer, b, 0, 0, 0))] * 2
    ret_tabs = _retention_tables(lc, n_chunks * n_seq)
    gla_tabs = _gla_tables(lc, n_chunks * n_seq)
    kern = functools.partial(_linear_mixers_kernel, lc=lc, n_chunks=n_chunks, n_seq=n_seq,
                             has_init=states is not None)
    out_rows = col(0)
    out_shape = jax.ShapeDtypeStruct((batch, seq_len, width), F32)
    oa, s_ret, ob, s_gla = pl.pallas_call(
        kern,
        grid=(batch // n_seq, ng),
        in_specs=([col(0), col(1), col(2), col(0), col(1), col(2), col(0)] + s_in_specs
                  + [full(t) for t in ret_tabs + gla_tabs]),
        out_specs=[out_rows, s_out_spec, out_rows, s_out_spec],
        out_shape=[out_shape, s_out_shape, out_shape, s_out_shape],
        scratch_shapes=[pltpu.VMEM((n_seq, width, width), F32),
                        pltpu.VMEM((n_seq, width, width), F32),
                        pltpu.VMEM((rows, H_GLA * lc), F32)],
        compiler_params=_params(2),
        name="linear_mixers",
    )(ret, ret, ret, gla, gla, gla, lf, *(states or ()), *ret_tabs, *gla_tabs)
    return oa.reshape(m, width), s_ret, ob.reshape(m, width), s_gla


def _gla_tables(lc, n_chunks):
    idx = np.arange(lc)
    tri = idx[:, None] >= idx[None, :]
    head = np.arange(W_GLA) // HEAD_DIM
    bd = head[:, None] == head[None, :]
    chunk_of_row = np.arange(lc * n_chunks) // lc
    onehot = chunk_of_row[:, None] == np.arange(LANES)[None, :]
    return jnp.asarray(tri, BF16), jnp.asarray(bd, F32), jnp.asarray(onehot, BF16)


def _band_attn_kernel(q_ref, kt_ref, vc_ref, gv_ref, hones_ref, o_ref,
                      kwin_t, vwin, bias_scr, s_scr, m_scr, ksq_prev, *, lc, rpi, n_iter):
    b_id = pl.program_id(0)
    g = pl.program_id(1)
    win = ATT_REACH + rpi
    rows = rpi * n_iter

    @pl.when((b_id == 0) & (g == 0))
    def _():
        kwin_t[:, 0:rows] = jnp.zeros((W_ATT, rows), BF16)
        vwin[0:rows, :] = jnp.zeros((rows, W_ATT), BF16)
        ksq_prev[0] = 0.0
        row = lax.broadcasted_iota(jnp.int32, (rpi, BIAS_W), 0)
        rel = lax.broadcasted_iota(jnp.int32, (rpi, BIAS_W), 1) - (row // lc) * lc
        in_band = (rel >= 0) & (rel < ATT_REACH + lc)
        for h in range(H_ATT):
            x = jnp.broadcast_to(gv_ref[h:h + 1, :], (rpi, BIAS_W))
            x = pltpu.roll(x, BIAS_W - (lc - 1), 1, stride=1, stride_axis=0)
            bias_scr[h // 2, (h % 2) * rpi:(h % 2 + 1) * rpi, :] = jnp.where(
                in_band, x * LOG2E, NEG_BIG)

    base = pl.multiple_of(g * rows, rows)
    kwin_t[:, pl.ds(base + rows, rows)] = kt_ref[...]
    vwin[pl.ds(base + rows, rows), :] = vc_ref[...]

    head_ones = hones_ref[...]
    q_cur = q_ref[...].astype(F32)
    q_sq = jnp.max(_dot((q_cur * q_cur).astype(BF16), head_ones))
    k_cur = kt_ref[...].astype(F32)
    k_sq_cur = jnp.max(jnp.sum((k_cur * k_cur).reshape(H_ATT, HEAD_DIM, rows), axis=1))
    k_sq = jnp.maximum(k_sq_cur, jnp.where(g == 0, 0.0, ksq_prev[0]))
    ksq_prev[0] = k_sq_cur
    room = ATT_SAFE_LOG2 - jnp.max(jnp.abs(gv_ref[...])) * LOG2E
    bounded = (room > 0.0) & (q_sq * k_sq * NORM_ROUNDING_MARGIN <= room * room)

    lane = lax.broadcasted_iota(jnp.int32, (rpi, LANES), 1)
    lane2 = lax.broadcasted_iota(jnp.int32, (2 * rpi, LANES), 1)
    head2 = lax.broadcasted_iota(jnp.int32, (2 * rpi, LANES), 0) // rpi
    col = lax.broadcasted_iota(jnp.int32, (2 * rpi, win), 1)

    def step(i, carry, *, masked, small):
        r0 = pl.multiple_of(i * rpi, rpi)
        first_valid = ATT_REACH - (g * rows + r0)

        def scores(pair):
            lanes = slice(pair * LANES, (pair + 1) * LANES)
            qp = q_ref[pl.ds(r0, rpi), lanes]
            q2 = jnp.concatenate([qp, qp], axis=0)
            q2 = jnp.where((lane2 // HEAD_DIM) == head2, q2, jnp.zeros_like(q2))
            s = _dot(q2, kwin_t[lanes, pl.ds(base + r0, win)]) + bias_scr[pair][:, :win]
            if masked:
                s = jnp.where(col >= first_valid, s, NEG_BIG)
            return s

        def finish(pair, e):
            lanes = slice(pair * LANES, (pair + 1) * LANES)
            den = jnp.sum(e, axis=-1, keepdims=True)
            pv = _dot(e.astype(BF16), vwin[pl.ds(base + r0, win), lanes]) / den
            o_ref[pl.ds(r0, rpi), lanes] = jnp.where(
                lane < HEAD_DIM, pv[:rpi, :], pv[rpi:, :]).astype(BF16)

        if small:
            for pair in range(H_ATT // 2):
                finish(pair, jnp.exp2(scores(pair)))
        else:
            for pair in range(H_ATT // 2):
                s = scores(pair)
                s_scr[pair, :, :win] = s
                m_scr[pair] = jnp.broadcast_to(
                    jnp.max(s, axis=-1, keepdims=True), (2 * rpi, LANES))
            for pair in range(H_ATT // 2):
                finish(pair, jnp.exp2(s_scr[pair, :, :win] - m_scr[pair][:, :1]))
        return carry

    def run(masked):
        @pl.when(bounded)
        def _():
            lax.fori_loop(0, n_iter, functools.partial(step, masked=masked, small=True), 0,
                          unroll=4)

        @pl.when(jnp.logical_not(bounded))
        def _():
            lax.fori_loop(0, n_iter, functools.partial(step, masked=masked, small=False), 0)

    @pl.when(g == 0)
    def _():
        run(True)

    @pl.when(g != 0)
    def _():
        run(False)


def _band_attn(att, keys_t, gv, layer, *, batch, seq_len, lc, rpi, n_iter):
    rows = rpi * n_iter
    assert rows == ATT_REACH, "the carried window is exactly one row group"
    ng = seq_len // rows
    m = batch * seq_len
    cur = lambda c: pl.BlockSpec((rows, W_ATT), lambda b, g: (b * ng + g, c))
    cur_t = pl.BlockSpec((W_ATT, rows), lambda b, g: (0, b * ng + g))
    kern = functools.partial(_band_attn_kernel, lc=lc, rpi=rpi, n_iter=n_iter)
    head_ones = _head_ones()
    return pl.pallas_call(
        kern,
        grid=(batch, ng),
        in_specs=[cur(0), cur_t, cur(2), _layer_spec(gv, layer),
                  pl.BlockSpec(head_ones.shape, lambda b, g: (0, 0))],
        out_specs=pl.BlockSpec((rows, W_ATT), lambda b, g: (b * ng + g, 0)),
        out_shape=jax.ShapeDtypeStruct((m, W_ATT), BF16),
        scratch_shapes=[pltpu.VMEM((W_ATT, (ng + 1) * rows), BF16),
                        pltpu.VMEM(((ng + 1) * rows, W_ATT), BF16),
                        pltpu.VMEM((H_ATT // 2, 2 * rpi, BIAS_W), F32),
                        pltpu.VMEM((H_ATT // 2, 2 * rpi, BIAS_W), F32),
                        pltpu.VMEM((H_ATT // 2, 2 * rpi, LANES), F32),
                        pltpu.SMEM((1,), F32)],
        compiler_params=_params(2),
        name="band_attn",
    )(att, keys_t, att, gv, head_ones)


def _sample_attn_kernel(q_ref, kn_ref, vn_ref, kt_ref, vt_ref, gv_ref, hones_ref, o_ref,
                        *, ls, n_seq):
    win = ATT_REACH + ls
    lane = lax.broadcasted_iota(jnp.int32, (ls, LANES), 1)
    lane2 = lax.broadcasted_iota(jnp.int32, (2 * ls, LANES), 1)
    head2 = lax.broadcasted_iota(jnp.int32, (2 * ls, LANES), 0) // ls

    head_ones = hones_ref[...]
    q32 = q_ref[...].astype(F32)
    kn32 = kn_ref[...].astype(F32)
    kc32 = kt_ref[...]
    q_sq = jnp.max(_dot((q32 * q32).astype(BF16), head_ones))
    k_sq = jnp.maximum(jnp.max(_dot((kn32 * kn32).astype(BF16), head_ones)),
                       jnp.max(jnp.sum(kc32 * kc32, axis=2)))
    room = ATT_SAFE_LOG2 - jnp.max(jnp.abs(gv_ref[...])) * LOG2E
    bounded = (room > 0.0) & (q_sq * k_sq * NORM_ROUNDING_MARGIN <= room * room)

    def head_pair(seq, pair, small):
        lanes = slice(pair * LANES, (pair + 1) * LANES)
        rows = slice(seq * ls, (seq + 1) * ls)
        qp = q_ref[rows, lanes]
        q2 = jnp.concatenate([qp, qp], axis=0)
        q2 = jnp.where((lane2 // HEAD_DIM) == head2, q2, jnp.zeros_like(q2))
        heads = slice(2 * pair, 2 * pair + 2)
        kt = kt_ref[seq, heads].reshape(2 * HEAD_DIM, ATT_REACH).astype(BF16)
        vt = vt_ref[seq, heads].reshape(2 * HEAD_DIM, ATT_REACH).astype(BF16)
        bias = jnp.concatenate(
            [pltpu.roll(jnp.broadcast_to(gv_ref[h:h + 1, :], (ls, BIAS_W)),
                        BIAS_W - (ls - 1), 1, stride=1, stride_axis=0)
             for h in (2 * pair, 2 * pair + 1)], axis=0) * LOG2E
        s_old = _dot(q2, kt) + bias[:, :ATT_REACH]
        s_new = _dot_nt(q2, kn_ref[rows, lanes]) + bias[:, ATT_REACH:win]
        if not small:
            mx = jnp.maximum(jnp.max(s_old, axis=-1, keepdims=True),
                             jnp.max(s_new, axis=-1, keepdims=True))
            s_old = s_old - mx
            s_new = s_new - mx
        e_old = jnp.exp2(s_old)
        e_new = jnp.exp2(s_new)
        den = jnp.sum(e_old, axis=-1, keepdims=True) + jnp.sum(e_new, axis=-1, keepdims=True)
        pv = (_dot_nt(e_old.astype(BF16), vt)
              + _dot(e_new.astype(BF16), vn_ref[rows, lanes])) / den
        o_ref[rows, lanes] = jnp.where(lane < HEAD_DIM, pv[:ls, :], pv[ls:, :]).astype(BF16)

    @pl.when(bounded)
    def _():
        for seq in range(n_seq):
            for pair in range(H_ATT // 2):
                head_pair(seq, pair, True)

    @pl.when(jnp.logical_not(bounded))
    def _():
        for seq in range(n_seq):
            for pair in range(H_ATT // 2):
                head_pair(seq, pair, False)


def _head_ones():
    lane_head = np.arange(W_ATT) // HEAD_DIM
    return jnp.asarray(lane_head[:, None] == np.arange(LANES)[None, :], BF16)


def _sample_attn(att, cache_kt, cache_vt, layer, gv, *, batch, ls):
    n_seq = SAMPLE_ATT_SEQS_PER_STEP
    new = lambda c: pl.BlockSpec((n_seq * ls, W_ATT), lambda b: (b, c))
    old = pl.BlockSpec((None, n_seq, H_ATT, HEAD_DIM, ATT_REACH),
                       lambda b: (layer, b, 0, 0, 0))
    head_ones = _head_ones()
    kern = functools.partial(_sample_attn_kernel, ls=ls, n_seq=n_seq)
    return pl.pallas_call(
        kern,
        grid=(batch // n_seq,),
        in_specs=[new(0), new(1), new(2), old, old, _layer_spec(gv, layer),
                  pl.BlockSpec(head_ones.shape, lambda b: (0, 0))],
        out_specs=pl.BlockSpec((n_seq * ls, W_ATT), lambda b: (b, 0)),
        out_shape=jax.ShapeDtypeStruct((batch * ls, W_ATT), BF16),
        compiler_params=_params(1),
        name="sample_attn",
    )(att, att, att, cache_kt, cache_vt, gv, head_ones)


def _bias_rows(rel_bias_l, lc):
    wv = np.arange(BIAS_W)
    dist = ATT_REACH + lc - 1 - wv
    ridx = np.clip(dist, -(CHUNK - 1), REL_MAX) + (CHUNK - 1)
    n_far = int(np.argmax(ridx < N_REL - 1)) - 1
    n_near = BIAS_W - n_far - N_REL
    assert np.array_equal(
        ridx, np.concatenate([np.full(n_far, N_REL - 1), np.arange(N_REL - 1, -1, -1),
                              np.zeros(n_near, np.int64)]))
    lead = rel_bias_l.shape[:-1]
    return jnp.concatenate(
        [jnp.broadcast_to(rel_bias_l[..., N_REL - 1:], lead + (n_far,)),
         jnp.flip(rel_bias_l, axis=-1),
         jnp.broadcast_to(rel_bias_l[..., :1], lead + (n_near,))], axis=-1)


def _mix_rows(x_ref, oa_ref, ob_ref, oc_ref, sga_ref, sgb_ref, params, y_ref, *, final_norm):
    gnw_ref, glw_ref, n2w_ref, fnw_ref, ones_ref, wout_ref, wup_ref, wdown_ref = params
    ones = ones_ref[...]

    def head_sum(t):
        return _dot(t.astype(BF16), ones)

    inv_d = 1.0 / HEAD_DIM
    oa = oa_ref[...]
    mu = head_sum(oa) * inv_d
    da = oa - mu
    var = head_sum(da * da) * inv_d
    a = da * lax.rsqrt(var + EPS) * gnw_ref[...] * sga_ref[...].astype(F32)
    ob = ob_ref[...]
    ms = head_sum(ob * ob) * inv_d
    bb = ob * lax.rsqrt(ms + EPS) * glw_ref[...] * sgb_ref[...].astype(F32)
    cat = jnp.concatenate([a.astype(BF16), bb.astype(BF16), oc_ref[...]], axis=1)
    h1 = x_ref[...] + _dot(cat, wout_ref[...])
    hn = _rmsnorm(h1, n2w_ref[...]).astype(BF16)
    acc = jnp.zeros_like(h1)
    n_slab = D_FF // D_MODEL
    for j in range(n_slab):
        sl = slice(j * D_MODEL, (j + 1) * D_MODEL)
        u = jnp.maximum(_dot(hn, wup_ref[:, sl]), 0.0)
        acc = acc + _dot((u * u).astype(BF16), wdown_ref[sl, :])
    h2 = h1 + acc
    if final_norm:
        h2 = _rmsnorm(h2, fnw_ref[...])
    y_ref[...] = h2


def _out_ffn_kernel(*refs, n_long_tiles, final_norm):
    rows_l, rows_s, params = refs[0:6], refs[6:12], refs[12:20]
    y_l, y_s = refs[20:22]
    i = pl.program_id(0)

    @pl.when(i < n_long_tiles)
    def _():
        _mix_rows(*rows_l, params, y_l, final_norm=final_norm)

    @pl.when(i == n_long_tiles)
    def _():
        _mix_rows(*rows_s, params, y_s, final_norm=final_norm)


def _out_ffn(long_in, short_in, gnw, glw, n2w, fnw, ones, wout, wup, wdown,
             *, layer, tm, final_norm):
    m = long_in[0].shape[0]
    ms = short_in[0].shape[0]
    n_tiles = m // tm
    full = lambda a: pl.BlockSpec(a.shape, lambda i: (0,) * a.ndim)
    at_layer = functools.partial(_layer_spec, layer=layer)
    tile = lambda i: jnp.minimum(i, n_tiles - 1)
    row = lambda w, c=0: pl.BlockSpec((tm, w), lambda i: (tile(i), c))
    whole = lambda w, c=0: pl.BlockSpec((ms, w), lambda i: (0, c))
    row_specs = lambda spec: [spec(D_MODEL), spec(W_RET), spec(W_GLA), spec(W_ATT),
                              spec(W_RET, 3), spec(W_GLA, 3)]
    kern = functools.partial(_out_ffn_kernel, n_long_tiles=n_tiles, final_norm=final_norm)
    return pl.pallas_call(
        kern,
        grid=(n_tiles + 1,),
        in_specs=(row_specs(row) + row_specs(whole)
                  + [at_layer(gnw), at_layer(glw), at_layer(n2w), full(fnw), full(ones),
                     full(wout), full(wup), full(wdown)]),
        out_specs=[row(D_MODEL), whole(D_MODEL)],
        out_shape=[jax.ShapeDtypeStruct((m, D_MODEL), F32),
                   jax.ShapeDtypeStruct((ms, D_MODEL), F32)],
        compiler_params=_params(1),
        name="out_ffn",
    )(*long_in, *short_in, gnw, glw, n2w, fnw, ones, wout, wup, wdown)


def _rotary_tables(pos, rows):
    half = HEAD_DIM // 2
    inv_freq = ROPE_BASE ** (-np.arange(half, dtype=np.float64) / half)
    ang = np.asarray(pos, np.float64)[:, None] * inv_freq[None, :]
    reps = (rows // len(pos), LANES // half)
    return (jnp.asarray(np.tile(np.cos(ang), reps), F32),
            jnp.asarray(np.tile(np.sin(ang), reps), F32))


def _mixers(proj, rel_bias, *, batch, seq_len, lc, n_chunks, layer, states, caches, keys_t=None):
    ret, gla, lf, att = proj
    n_seq = batch if lc * n_chunks == seq_len else LINMIX_SEQS_PER_STEP
    oa, s_ret, ob, s_gla = _linear_mixers(
        ret, gla, lf, states, layer, batch=batch, seq_len=seq_len, lc=lc, n_chunks=n_chunks,
        n_seq=n_seq)
    gv = _bias_rows(rel_bias, lc)
    if caches is None:
        rpi = min(lc * n_chunks, ATT_ROWS_PER_STEP)
        oc = _band_attn(att, keys_t, gv, layer, batch=batch, seq_len=seq_len, lc=lc, rpi=rpi,
                        n_iter=lc * n_chunks // rpi)
    else:
        oc = _sample_attn(att, caches[0], caches[1], layer, gv, batch=batch, ls=seq_len)
    return (oa, ob, oc, ret, gla), s_ret, s_gla


def kernel(x_prompt, x_sample, state_ret, state_gla, cache_attn_k, cache_attn_v, norm1_w, norm2_w, final_norm_w, w_in, w_gla_a2, b_gla_a, ret_gn_w, gla_norm_w, rel_bias, w_out, w_up, w_down):
    depth = w_in.shape[0]
    bp, lp, _ = x_prompt.shape
    bs, ls, _ = x_sample.shape
    keep = min(ATT_REACH, lp)
    head = np.arange(W_RET) // HEAD_DIM
    ones = jnp.asarray(head[:, None] == head[None, :], BF16)
    hp = x_prompt.reshape(bp * lp, D_MODEL)
    hs = x_sample.reshape(bs * ls, D_MODEL)
    pos_p = np.arange(lp)
    pos_s = PAST_LEN + np.arange(ls)
    tm_p = keep
    to_head_major = lambda c: jnp.transpose(c, (0, 1, 3, 4, 2))
    caches = (to_head_major(cache_attn_k.astype(F32)), to_head_major(cache_attn_v.astype(F32)))
    from_head_major = lambda t: jnp.transpose(
        t.reshape(depth, bp, H_ATT, HEAD_DIM, keep), (0, 1, 4, 2, 3))
    states = (state_ret.astype(F32), state_gla.astype(F32))
    outs = {k: [] for k in ("ret_p", "gla_p", "ret_s", "gla_s", "ks", "vs")}
    prompt_kv = None
    rows3 = lambda v: v.astype(F32).reshape(depth, 1, v.shape[-1])
    n1w, n2w, fnw = rows3(norm1_w), rows3(norm2_w), final_norm_w.astype(F32)[None, :]
    gnw, glw, ba = rows3(ret_gn_w), rows3(gla_norm_w), rows3(b_gla_a)
    w_in_t = jnp.swapaxes(w_in.astype(F32), 1, 2)
    w_a2 = w_gla_a2.astype(F32)
    bias_tab = rel_bias.astype(F32)
    later_w = (w_out.astype(F32), w_up.astype(F32), w_down.astype(F32))
    tabs_p = _rotary_tables(pos_p, lp)
    tabs_s = _rotary_tables(pos_s, bs * ls)
    for l in range(depth):
        last = l == depth - 1
        proj_p, proj_s, (wout, wup, wdown) = _in_proj(
            hp, hs, n1w, w_in_t, w_a2, ba, tabs_p, tabs_s, prompt_kv, later_w,
            layer=l, tm=tm_p, seq_len=lp)
        prompt_kv = proj_p[4:6]
        mix_p, s_ret, s_gla = _mixers(
            proj_p[:4], bias_tab, batch=bp, seq_len=lp, lc=CHUNK, n_chunks=ATT_REACH // CHUNK,
            layer=l, states=None, caches=None, keys_t=proj_p[6])
        outs["ret_p"].append(s_ret)
        outs["gla_p"].append(s_gla)
        mix_s, s_ret, s_gla = _mixers(
            proj_s[:4], bias_tab, batch=bs, seq_len=ls, lc=ls, n_chunks=1,
            layer=l, states=states, caches=caches)
        outs["ret_s"].append(s_ret)
        outs["gla_s"].append(s_gla)
        outs["ks"].append(proj_s[4].reshape(bs, ls, H_ATT, HEAD_DIM))
        outs["vs"].append(proj_s[5].reshape(bs, ls, H_ATT, HEAD_DIM))
        hp, hs = _out_ffn((hp,) + mix_p, (hs,) + mix_s, gnw, glw, n2w, fnw, ones,
                          wout, wup, wdown, layer=l, tm=tm_p, final_norm=last)
    y_prompt = hp.reshape(bp, lp, D_MODEL)
    y_sample = hs.reshape(bs, ls, D_MODEL)
    st = lambda k: jnp.stack(outs[k])
    return (y_prompt, y_sample, st("ret_p"), st("gla_p"),
            from_head_major(prompt_kv[0]), from_head_major(prompt_kv[1]),
            st("ret_s"), st("gla_s"), st("ks"), st("vs"))
```
